```python
import math
import jax, jax.numpy as jnp
from jax import lax
import numpy as np

D_MODEL = 2048
BATCH = 1
SEQ = 8192
DEPTH = 4

M_HEADS = 4
M_WIDTH = D_MODEL // 2
M_V_DIM = M_WIDTH // M_HEADS
M_QK_DIM = M_V_DIM // 2
M_CHUNK = 64
CONV_WIDTH = 5
A_HEADS = 8
A_WIDTH = D_MODEL // 2
A_V_DIM = A_WIDTH // A_HEADS
A_HEAD_DIM = A_V_DIM // 2
ROPE_THETA = 500000.0
ROPE_DIM = A_HEAD_DIM // 4
Q_BLOCK = 128
N_GROUPS = 4
EXPERTS_PER_GROUP = 8
N_EXPERTS = N_GROUPS * EXPERTS_PER_GROUP
TOP_K = 2
D_EXPERT = D_MODEL // 4
EPS = 1e-6

COL_SIZES = (
    M_HEADS * M_QK_DIM,
    M_HEADS * M_QK_DIM,
    M_WIDTH,
    4 * M_HEADS,
    M_WIDTH,
    A_HEADS * 2 * A_HEAD_DIM,
    A_HEADS * 2 * A_HEAD_DIM,
    A_WIDTH,
    D_MODEL,
    D_MODEL,
)
N_IN = sum(COL_SIZES)

kernel_name = "hybrid_mlstm_diffattn_hmoe_encoder"


def rms_norm(x, g=None):
    xf = x.astype(jnp.float32)
    y = xf * lax.rsqrt(jnp.mean(xf * xf, axis=-1, keepdims=True) + EPS)
    if g is not None:
        y = y * g.astype(jnp.float32)
    return y.astype(x.dtype)


def rope_tables(seq):
    pos = jnp.arange(seq, dtype=jnp.float32)
    inv = ROPE_THETA ** (-jnp.arange(0, ROPE_DIM, 2, dtype=jnp.float32) / ROPE_DIM)
    ang = pos[:, None] * inv[None, :]
    return jnp.cos(ang), jnp.sin(ang)


def partial_rope(x, cos, sin):
    half = ROPE_DIM // 2
    x1 = x[..., :half].astype(jnp.float32)
    x2 = x[..., half:ROPE_DIM].astype(jnp.float32)
    rot = jnp.concatenate([x1 * cos - x2 * sin, x2 * cos + x1 * sin], axis=-1).astype(x.dtype)
    return jnp.concatenate([rot, x[..., ROPE_DIM:]], axis=-1)


def centred_depthwise_conv(x, w):
    pad = w.shape[0] // 2
    return lax.conv_general_dilated(
        x, w[:, None, :].astype(x.dtype), window_strides=(1,), padding=[(pad, pad)],
        dimension_numbers=("NWC", "WIO", "NWC"), feature_group_count=x.shape[-1])


def mlstm_chunkwise(q, k, v, log_i, log_f):
    B, H, S, dk = q.shape
    dv = v.shape[-1]
    nc = S // M_CHUNK

    def to_chunks(t):
        t = t.reshape((B, H, nc, M_CHUNK) + t.shape[3:])
        return jnp.moveaxis(t, 2, 0)

    scan_order = jnp.tril(jnp.ones((M_CHUNK, M_CHUNK), dtype=bool))

    def step(carry, xs):
        C, n, m = carry
        qc, kc, vc, ic, fc = xs
        b = jnp.cumsum(fc, axis=-1)
        dmat = b[..., :, None] - b[..., None, :] + ic[..., None, :]
        dmat = jnp.where(scan_order, dmat, -jnp.inf)
        m_inter = b + m[..., None]
        m_t = jnp.maximum(m_inter, jnp.max(dmat, axis=-1))
        s = jnp.einsum('bhld,bhsd->bhls', qc, kc) * jnp.exp(dmat - m_t[..., None])
        inter = jnp.exp(m_inter - m_t)
        num = inter[..., None] * jnp.einsum('bhld,bhdv->bhlv', qc, C) + jnp.einsum('bhls,bhsv->bhlv', s, vc)
        den = inter * jnp.einsum('bhld,bhd->bhl', qc, n) + jnp.sum(s, axis=-1)
        h = num / jnp.maximum(jnp.abs(den), jnp.exp(-m_t))[..., None]
        b_end = b[..., -1]
        g = b_end[..., None] - b + ic
        m_new = jnp.maximum(b_end + m, jnp.max(g, axis=-1))
        w = jnp.exp(g - m_new[..., None])
        decay = jnp.exp(b_end + m - m_new)
        C = decay[..., None, None] * C + jnp.einsum('bhs,bhsd,bhsv->bhdv', w, kc, vc)
        n = decay[..., None] * n + jnp.einsum('bhs,bhsd->bhd', w, kc)
        return (C, n, m_new), h

    init = (jnp.zeros((B, H, dk, dv), jnp.float32), jnp.zeros((B, H, dk), jnp.float32),
            jnp.zeros((B, H), jnp.float32))
    _, h = lax.scan(step, init, (to_chunks(q), to_chunks(k), to_chunks(v),
                                 to_chunks(log_i), to_chunks(log_f)))
    return jnp.moveaxis(h, 0, 2).reshape(B, H, S, dv)


def mlstm_branch(zq, zk, zv, zg, zo, conv_w, gate_b, norm_g):
    B, S, _ = zq.shape
    dtype = zq.dtype
    qk = jax.nn.silu(centred_depthwise_conv(jnp.concatenate([zq, zk], axis=-1), conv_w))
    zq, zk = jnp.split(qk, 2, axis=-1)
    heads = lambda t, d: t.reshape(B, S, M_HEADS, d).transpose(0, 2, 1, 3).astype(jnp.float32)
    q = heads(zq, M_QK_DIM) * (M_QK_DIM ** -0.5)
    k = heads(zk, M_QK_DIM)
    v = heads(zv, M_V_DIM)
    gates = (zg.reshape(B, S, 4, M_HEADS) + gate_b).astype(jnp.float32).transpose(2, 0, 3, 1)
    i_fwd, f_fwd, i_bwd, f_bwd = gates[0], gates[1], gates[2], gates[3]
    h_fwd = mlstm_chunkwise(q, k, v, i_fwd, jax.nn.log_sigmoid(f_fwd))
    flip = lambda t: jnp.flip(t, axis=2)
    h_bwd = flip(mlstm_chunkwise(flip(q), flip(k), flip(v), flip(i_bwd), flip(jax.nn.log_sigmoid(f_bwd))))
    h = (h_fwd + h_bwd).transpose(0, 2, 1, 3)
    h = rms_norm(h).reshape(B, S, M_WIDTH) * norm_g.astype(jnp.float32)
    return (h * jax.nn.sigmoid(zo.astype(jnp.float32))).astype(dtype)


def diff_attention_branch(zq, zk, zv, qn_g, kn_g, lam, subln_g, lam_init, cos, sin):
    B, S, _ = zq.shape
    nb = S // Q_BLOCK
    q = zq.reshape(B, S, A_HEADS, 2, A_HEAD_DIM).transpose(0, 2, 3, 1, 4)
    k = zk.reshape(B, S, A_HEADS, 2, A_HEAD_DIM).transpose(0, 2, 3, 1, 4)
    v = zv.reshape(B, S, A_HEADS, A_V_DIM).transpose(0, 2, 1, 3)
    q = partial_rope(rms_norm(q, qn_g), cos, sin) * (A_HEAD_DIM ** -0.5)
    k = partial_rope(rms_norm(k, kn_g), cos, sin)
    lamf = lam.astype(jnp.float32)
    lam_full = jnp.exp(jnp.sum(lamf[0] * lamf[1])) - jnp.exp(jnp.sum(lamf[2] * lamf[3])) + lam_init
    qb = q.reshape(B, A_HEADS, 2, nb, Q_BLOCK, A_HEAD_DIM).transpose(3, 0, 1, 2, 4, 5)

    def block(q_blk):
        s = jnp.einsum('bhcqd,bhckd->bhcqk', q_blk, k).astype(jnp.float32)
        p = jax.nn.softmax(s, axis=-1)
        a = p[:, :, 0] - lam_full * p[:, :, 1]
        return jnp.einsum('bhqk,bhkv->bhqv', a.astype(v.dtype), v)

    o = lax.map(block, qb)
    o = o.transpose(1, 0, 3, 2, 4).reshape(B, S, A_HEADS, A_V_DIM)
    o = rms_norm(o, subln_g) * (1.0 - lam_init)
    return o.reshape(B, S, A_WIDTH)


def hierarchical_moe(h, rg_w, rg_b, re_w, re_b, w_gate, w_up, w_down):
    B, S, D = h.shape
    t = h.reshape(B * S, D)
    g_logits = (t @ rg_w + rg_b).astype(jnp.float32)
    g_val, g_idx = lax.top_k(jax.nn.softmax(g_logits, axis=-1), 1)
    e_logits = (t @ re_w + re_b).astype(jnp.float32).reshape(-1, N_GROUPS, EXPERTS_PER_GROUP)
    within = jnp.einsum('tg,tge->te', jax.nn.one_hot(g_idx[:, 0], N_GROUPS, dtype=jnp.float32), e_logits)
    e_val, e_idx = lax.top_k(within, TOP_K)
    e_w = jax.nn.softmax(e_val, axis=-1) * g_val
    global_idx = g_idx * EXPERTS_PER_GROUP + e_idx
    combine = jnp.einsum('tk,tke->te', e_w, jax.nn.one_hot(global_idx, N_EXPERTS, dtype=jnp.float32))
    hg = jnp.einsum('td,edf->tef', t, w_gate)
    hu = jnp.einsum('td,edf->tef', t, w_up)
    a = jax.nn.silu(hg) * hu * combine.astype(t.dtype)[:, :, None]
    y = jnp.einsum('tef,efd->td', a, w_down)
    return y.reshape(B, S, D)


def setup_inputs(seed: int = 0) -> dict:
    key = jax.random.key(seed)
    ks = jax.random.split(key, 32)
    nrm = lambda k, shape, scale: jax.random.normal(k, shape, jnp.float32) * scale
    L, D = DEPTH, D_MODEL
    f_base = jnp.linspace(3.0, 6.0, M_HEADS, dtype=jnp.float32)
    zero_h = jnp.zeros((M_HEADS,), jnp.float32)
    gate_base = jnp.stack([zero_h, f_base, zero_h, f_base])
    return {
        "x": nrm(ks[0], (BATCH, SEQ, D), 1.0),
        "c": nrm(ks[1], (BATCH, D), 1.0),
        "ada_w": nrm(ks[2], (L, D, 6 * D), 0.5 * D ** -0.5),
        "ada_b": nrm(ks[3], (L, 6 * D), 0.02),
        "norm1_g": 1.0 + nrm(ks[4], (L, D), 0.02),
        "norm2_g": 1.0 + nrm(ks[5], (L, D), 0.02),
        "w_in": nrm(ks[6], (L, D, N_IN), D ** -0.5),
        "m_conv_w": nrm(ks[7], (L, CONV_WIDTH, 2 * M_HEADS * M_QK_DIM), CONV_WIDTH ** -0.5),
        "m_gate_b": gate_base[None] + nrm(ks[8], (L, 4, M_HEADS), 0.1),
        "m_norm_g": 1.0 + nrm(ks[9], (L, M_WIDTH), 0.02),
        "a_qnorm_g": 1.0 + nrm(ks[10], (L, A_HEAD_DIM), 0.02),
        "a_knorm_g": 1.0 + nrm(ks[11], (L, A_HEAD_DIM), 0.02),
        "a_lambda": nrm(ks[12], (L, 4, A_HEAD_DIM), 0.1),
        "a_subln_g": 1.0 + nrm(ks[13], (L, A_V_DIM), 0.02),
        "w_branch_m": nrm(ks[14], (L, M_WIDTH, D), M_WIDTH ** -0.5),
        "w_branch_a": nrm(ks[15], (L, A_WIDTH, D), A_WIDTH ** -0.5),
        "w_out": nrm(ks[16], (L, D, D), D ** -0.5),
        "rg_w": nrm(ks[17], (L, D, N_GROUPS), D ** -0.5),
        "rg_b": nrm(ks[18], (L, N_GROUPS), 0.01),
        "re_w": nrm(ks[19], (L, D, N_EXPERTS), D ** -0.5),
        "re_b": nrm(ks[20], (L, N_EXPERTS), 0.01),
        "e_w_gate": nrm(ks[21], (L, N_EXPERTS, D, D_EXPERT), D ** -0.5),
        "e_w_up": nrm(ks[22], (L, N_EXPERTS, D, D_EXPERT), D ** -0.5),
        "e_w_down": nrm(ks[23], (L, N_EXPERTS, D_EXPERT, D), D_EXPERT ** -0.5),
    }


def reference(x, c, ada_w, ada_b, norm1_g, norm2_g, w_in, m_conv_w, m_gate_b, m_norm_g,
              a_qnorm_g, a_knorm_g, a_lambda, a_subln_g, w_branch_m, w_branch_a, w_out,
              rg_w, rg_b, re_w, re_b, e_w_gate, e_w_up, e_w_down):
    B, S, D = x.shape
    cos, sin = rope_tables(S)
    c_act = jax.nn.silu(c)
    split_at = [int(v) for v in np.cumsum(COL_SIZES)[:-1]]
    for l in range(DEPTH):
        lam_init = 0.8 - 0.6 * math.exp(-0.3 * l)
        mod = (c_act @ ada_w[l] + ada_b[l])[:, None, :]
        sh1, sc1, g1, sh2, sc2, g2 = jnp.split(mod, 6, axis=-1)
        h = rms_norm(x, norm1_g[l]) * (1.0 + sc1) + sh1
        z = h @ w_in[l]
        mq, mk, mv, mg, mo, aq, ak, av, gm, ga = jnp.split(z, split_at, axis=-1)
        y_m = mlstm_branch(mq, mk, mv, mg, mo, m_conv_w[l], m_gate_b[l], m_norm_g[l])
        y_a = diff_attention_branch(aq, ak, av, a_qnorm_g[l], a_knorm_g[l], a_lambda[l],
                                    a_subln_g[l], lam_init, cos, sin)
        merged = jax.nn.sigmoid(gm) * (y_m @ w_branch_m[l]) + jax.nn.sigmoid(ga) * (y_a @ w_branch_a[l])
        x = x + g1 * (merged @ w_out[l])
        h2 = rms_norm(x, norm2_g[l]) * (1.0 + sc2) + sh2
        x = x + g2 * hierarchical_moe(h2, rg_w[l], rg_b[l], re_w[l], re_b[l],
                                      e_w_gate[l], e_w_up[l], e_w_down[l])
    return x
```

```python
import functools
import math

import jax
import jax.numpy as jnp
from jax import lax
from jax.experimental import pallas as pl
from jax.experimental.pallas import tpu as pltpu

F32 = jnp.float32
BF16 = jnp.bfloat16

M_HEADS = 4
A_HEADS = 8
CONV_WIDTH = 5
ROPE_THETA = 500000.0
N_GROUPS = 4
EXPERTS_PER_GROUP = 8
EPS = 1e-6
NEG = -1e30
LANES = 128
BF16_SUBLANES = 16
VMEM_LIMIT = 56 * 1024 * 1024


def _sigmoid(x):
    return 1.0 / (1.0 + jnp.exp(-x))


def _log_sigmoid(x):
    return jnp.minimum(x, 0.0) - jnp.log(1.0 + jnp.exp(-jnp.abs(x)))


def _params(*sem):
    return pltpu.CompilerParams(dimension_semantics=sem, vmem_limit_bytes=VMEM_LIMIT)


def _mod_kernel(c_ref, w_ref, b_ref, o_ref):
    c = c_ref[...]
    ca = c * _sigmoid(c)
    o_ref[0] = jnp.sum(ca * w_ref[0], axis=0, keepdims=True) + b_ref[0]


def _modulation(c, ada_w, ada_b):
    L, D, N = ada_w.shape
    tn = min(1024, N)
    return pl.pallas_call(
        _mod_kernel,
        grid=(L, N // tn),
        in_specs=[pl.BlockSpec((D, 1), lambda l, j: (0, 0)),
                  pl.BlockSpec((1, D, tn), lambda l, j: (l, 0, j)),
                  pl.BlockSpec((1, 1, tn), lambda l, j: (l, 0, j))],
        out_specs=pl.BlockSpec((1, 1, tn), lambda l, j: (l, 0, j)),
        out_shape=jax.ShapeDtypeStruct((L, 1, N), F32),
        compiler_params=_params("arbitrary", "arbitrary"),
        name="adaln_mod",
    )(c.reshape(D, 1), ada_w, ada_b.reshape(L, 1, N))


def _inproj_kernel(x_ref, g_ref, sc_ref, sh_ref, w_ref, wg_ref, z_ref, gate_ref, h_scr):
    @pl.when(pl.program_id(1) == 0)
    def _():
        x = x_ref[...]
        ms = jnp.mean(x * x, axis=-1, keepdims=True)
        h = x * lax.rsqrt(ms + EPS) * g_ref[...] * (1.0 + sc_ref[...]) + sh_ref[...]
        hb = h.astype(BF16)
        h_scr[...] = hb
        gate_ref[...] = jnp.dot(hb, wg_ref[...], preferred_element_type=F32)

    z_ref[...] = jnp.dot(h_scr[...], w_ref[...], preferred_element_type=F32).astype(BF16)


def _in_projection(x, g, sc, sh, w_main, w_gate):
    S, D = x.shape
    N = w_main.shape[1]
    tm, tn = min(512, S), 1024
    vec = pl.BlockSpec((1, D), lambda i, j: (0, 0))
    return pl.pallas_call(
        _inproj_kernel,
        grid=(S // tm, N // tn),
        in_specs=[pl.BlockSpec((tm, D), lambda i, j: (i, 0)), vec, vec, vec,
                  pl.BlockSpec((D, tn), lambda i, j: (0, j)),
                  pl.BlockSpec((D, LANES), lambda i, j: (0, 0))],
        out_specs=[pl.BlockSpec((tm, tn), lambda i, j: (i, j)),
                   pl.BlockSpec((tm, LANES), lambda i, j: (i, 0))],
        out_shape=[jax.ShapeDtypeStruct((S, N), BF16), jax.ShapeDtypeStruct((S, LANES), F32)],
        scratch_shapes=[pltpu.VMEM((tm, D), BF16)],
        compiler_params=_params("arbitrary", "arbitrary"),
        name="norm_inproj",
    )(x, g, sc, sh, w_main, w_gate)


def _conv_kernel(prev_ref, cur_ref, next_ref, w_ref, o_ref, ext_scr, *, ts, q_scale):
    i = pl.program_id(0)
    halo = BF16_SUBLANES
    pad = CONV_WIDTH // 2
    ext_scr[0:halo, :] = jnp.where(i > 0, prev_ref[...].astype(F32), 0.0)
    ext_scr[halo:halo + ts, :] = cur_ref[...].astype(F32)
    ext_scr[halo + ts:2 * halo + ts, :] = jnp.where(i < pl.num_programs(0) - 1,
                                                    next_ref[...].astype(F32), 0.0)
    acc = w_ref[0:1, :] * ext_scr[pl.ds(halo - pad, ts), :]
    for k in range(1, CONV_WIDTH):
        acc = acc + w_ref[k:k + 1, :] * ext_scr[pl.ds(halo - pad + k, ts), :]
    y = acc * _sigmoid(acc)
    C = y.shape[1]
    col = lax.broadcasted_iota(jnp.int32, (1, C), 1)
    o_ref[...] = (y * jnp.where(col < C // 2, q_scale, 1.0)).astype(BF16)


def _qk_conv(z, conv_w, q_scale):
    S = z.shape[0]
    C = conv_w.shape[1]
    ts = min(512, S)
    halo = BF16_SUBLANES
    r = ts // halo
    nblk = S // halo
    return pl.pallas_call(
        functools.partial(_conv_kernel, ts=ts, q_scale=q_scale),
        grid=(S // ts,),
        in_specs=[pl.BlockSpec((halo, C), lambda i: (jnp.maximum(i * r - 1, 0), 0)),
                  pl.BlockSpec((ts, C), lambda i: (i, 0)),
                  pl.BlockSpec((halo, C), lambda i: (jnp.minimum((i + 1) * r, nblk - 1), 0)),
                  pl.BlockSpec((CONV_WIDTH, C), lambda i: (0, 0))],
        out_specs=pl.BlockSpec((ts, C), lambda i: (i, 0)),
        out_shape=jax.ShapeDtypeStruct((S, C), BF16),
        scratch_shapes=[pltpu.VMEM((ts + 2 * halo, C), F32)],
        compiler_params=_params("arbitrary"),
        name="qk_conv",
    )(z, z, z, conv_w)


def _mlstm_kernel(qk_ref, v_ref, g_ref, gt_ref, gb_ref, gbt_ref, *rest, reverse, dk, dv, L):
    H = M_HEADS
    if reverse:
        hf_ref, mo_ref, ng_ref, y_ref, C_scr, n_scr, m_scr = rest
    else:
        h_ref, C_scr, n_scr, m_scr = rest

    @pl.when(pl.program_id(0) == 0)
    def _():
        C_scr[...] = jnp.zeros_like(C_scr)
        n_scr[...] = jnp.zeros_like(n_scr)
        m_scr[...] = jnp.zeros_like(m_scr)

    gates = g_ref[...] + gb_ref[...]
    gates_t = gt_ref[...] + gbt_ref[...]
    row = lax.broadcasted_iota(jnp.int32, (L, L), 0)
    col = lax.broadcasted_iota(jnp.int32, (L, L), 1)
    mask = (col >= row) if reverse else (col <= row)
    mask_t = (row >= col) if reverse else (row <= col)
    nt = (((1,), (1,)), ((), ()))
    tn = (((0,), (0,)), ((), ()))

    for h in range(H):
        gi = (2 * H if reverse else 0) + h
        gf = gi + H
        i_col = gates[:, gi:gi + 1]
        f_col = _log_sigmoid(gates[:, gf:gf + 1])
        i_row = gates_t[gi:gi + 1, :]
        f_row = _log_sigmoid(gates_t[gf:gf + 1, :])
        b_col = jnp.sum(jnp.where(mask, f_row, 0.0), axis=1, keepdims=True)
        b_row = jnp.sum(jnp.where(mask_t, f_col, 0.0), axis=0, keepdims=True)
        b_end = jnp.sum(f_row, axis=1, keepdims=True)
        m_prev = m_scr[h][:, 0:1]

        dmat = jnp.where(mask, b_col - b_row + i_row, NEG)
        m_inter = b_col + m_prev
        m_t = jnp.maximum(m_inter, jnp.max(dmat, axis=1, keepdims=True))
        q = qk_ref[:, h * dk:(h + 1) * dk]
        k = qk_ref[:, (H + h) * dk:(H + h + 1) * dk]
        v = v_ref[:, h * dv:(h + 1) * dv]
        s = lax.dot_general(q, k, nt, preferred_element_type=F32) * jnp.exp(dmat - m_t)
        inter = jnp.exp(m_inter - m_t)
        c_prev = C_scr[h]
        n_prev = n_scr[h]
        num = (inter * jnp.dot(q, c_prev.astype(BF16), preferred_element_type=F32)
               + jnp.dot(s.astype(BF16), v, preferred_element_type=F32))
        den = (inter * jnp.sum(q.astype(F32) * n_prev, axis=1, keepdims=True)
               + jnp.sum(s, axis=1, keepdims=True))
        hout = num / jnp.maximum(jnp.abs(den), jnp.exp(-m_t))

        g_col = b_end - b_col + i_col
        m_new = jnp.maximum(b_end + m_prev, jnp.max(g_col, axis=0, keepdims=True))
        w_col = jnp.exp(g_col - m_new)
        decay = jnp.exp(b_end + m_prev - m_new)
        kw = k.astype(F32) * w_col
        C_scr[h] = decay * c_prev + lax.dot_general(kw.astype(BF16), v, tn, preferred_element_type=F32)
        n_scr[h] = decay * n_prev + jnp.sum(kw, axis=0, keepdims=True)
        m_scr[h] = jnp.broadcast_to(m_new, (1, LANES))

        if reverse:
            hs = hout + hf_ref[:, h * dv:(h + 1) * dv]
            ms = jnp.mean(hs * hs, axis=-1, keepdims=True)
            y = hs * lax.rsqrt(ms + EPS) * ng_ref[:, h * dv:(h + 1) * dv]
            y = y * _sigmoid(mo_ref[:, h * dv:(h + 1) * dv].astype(F32))
            y_ref[:, h * dv:(h + 1) * dv] = y.astype(BF16)
        else:
            h_ref[:, h * dv:(h + 1) * dv] = hout


def _mlstm(qk, z, gates, gates_t, gate_b, norm_g, *, v_blk, mo_blk):
    S = qk.shape[0]
    H = M_HEADS
    dk = qk.shape[1] // (2 * H)
    W = norm_g.shape[1]
    dv = W // H
    L = min(256, S)
    nc = S // L
    gb = jnp.zeros((1, LANES), F32).at[0, :4 * H].set(gate_b.reshape(-1))
    gbt = gb[0, :4 * H].reshape(4 * H, 1)
    scratch = [pltpu.VMEM((H, dk, dv), F32), pltpu.VMEM((H, 1, dk), F32), pltpu.VMEM((H, 1, LANES), F32)]

    def specs(cmap):
        return [pl.BlockSpec((L, 2 * H * dk), lambda c: (cmap(c), 0)),
                pl.BlockSpec((L, W), lambda c: (cmap(c), v_blk)),
                pl.BlockSpec((L, LANES), lambda c: (cmap(c), 0)),
                pl.BlockSpec((4 * H, L), lambda c: (0, cmap(c))),
                pl.BlockSpec((1, LANES), lambda c: (0, 0)),
                pl.BlockSpec((4 * H, 1), lambda c: (0, 0))]

    fwd = lambda c: c
    h_fwd = pl.pallas_call(
        functools.partial(_mlstm_kernel, reverse=False, dk=dk, dv=dv, L=L),
        grid=(nc,),
        in_specs=specs(fwd),
        out_specs=pl.BlockSpec((L, W), lambda c: (c, 0)),
        out_shape=jax.ShapeDtypeStruct((S, W), F32),
        scratch_shapes=scratch,
        compiler_params=_params("arbitrary"),
        name="mlstm_fwd",
    )(qk, z, gates, gates_t, gb, gbt)

    bwd = lambda c: nc - 1 - c
    return pl.pallas_call(
        functools.partial(_mlstm_kernel, reverse=True, dk=dk, dv=dv, L=L),
        grid=(nc,),
        in_specs=specs(bwd) + [pl.BlockSpec((L, W), lambda c: (bwd(c), 0)),
                               pl.BlockSpec((L, W), lambda c: (bwd(c), mo_blk)),
                               pl.BlockSpec((1, W), lambda c: (0, 0))],
        out_specs=pl.BlockSpec((L, W), lambda c: (bwd(c), 0)),
        out_shape=jax.ShapeDtypeStruct((S, W), BF16),
        scratch_shapes=scratch,
        compiler_params=_params("arbitrary"),
        name="mlstm_bwd",
    )(qk, z, gates, gates_t, gb, gbt, h_fwd, z, norm_g)


def _qkprep_kernel(q_ref, k_ref, cos_ref, sin_ref, qg_ref, kg_ref, q2_ref, kn_ref, *, hd, q_scale):
    ts, W = q_ref.shape
    lane = lax.broadcasted_iota(jnp.int32, (ts, LANES), 1)
    lo = lane < hd
    first_half = (lane % hd) < (hd // 8)
    cos = cos_ref[...]
    sin = sin_ref[...]

    def norm_rope(x, g):
        sq = x * x
        s_lo = jnp.sum(jnp.where(lo, sq, 0.0), axis=1, keepdims=True)
        s_hi = jnp.sum(jnp.where(lo, 0.0, sq), axis=1, keepdims=True)
        inv = lax.rsqrt(jnp.where(lo, s_lo, s_hi) * (1.0 / hd) + EPS)
        xn = x * inv * g
        up = pltpu.roll(xn, LANES - hd // 8, axis=1)
        dn = pltpu.roll(xn, hd // 8, axis=1)
        return xn * cos + jnp.where(first_half, up, dn) * sin

    for j in range(W // LANES):
        sl = slice(j * LANES, (j + 1) * LANES)
        qn = norm_rope(q_ref[:, sl].astype(F32), qg_ref[:, sl]) * q_scale
        q2_ref[0, :, sl] = jnp.where(lo, qn, 0.0).astype(BF16)
        q2_ref[1, :, sl] = jnp.where(lo, 0.0, qn).astype(BF16)
        kn_ref[:, sl] = norm_rope(k_ref[:, sl].astype(F32), kg_ref[:, sl]).astype(BF16)


def _qk_prep(z, cos_t, sin_t, qg, kg, *, q_blk, k_blk, hd, q_scale):
    S = z.shape[0]
    W = qg.shape[1]
    ts = min(512, S)
    return pl.pallas_call(
        functools.partial(_qkprep_kernel, hd=hd, q_scale=q_scale),
        grid=(S // ts,),
        in_specs=[pl.BlockSpec((ts, W), lambda i: (i, q_blk)),
                  pl.BlockSpec((ts, W), lambda i: (i, k_blk)),
                  pl.BlockSpec((ts, LANES), lambda i: (i, 0)),
                  pl.BlockSpec((ts, LANES), lambda i: (i, 0)),
                  pl.BlockSpec((1, W), lambda i: (0, 0)),
                  pl.BlockSpec((1, W), lambda i: (0, 0))],
        out_specs=[pl.BlockSpec((2, ts, W), lambda i: (0, i, 0)),
                   pl.BlockSpec((ts, W), lambda i: (i, 0))],
        out_shape=[jax.ShapeDtypeStruct((2, S, W), BF16), jax.ShapeDtypeStruct((S, W), BF16)],
        compiler_params=_params("arbitrary"),
        name="attn_qk_prep",
    )(z, z, cos_t, sin_t, qg, kg)


def _flash_kernel(qa_ref, qb_ref, k_ref, v_ref, lam_ref, sg_ref, o_ref, q_scr, m_scr, l_scr, acc_scr,
                  *, tq, lam_init):
    kj = pl.program_id(2)

    @pl.when(kj == 0)
    def _():
        q_scr[0:tq, :] = qa_ref[0]
        q_scr[tq:2 * tq, :] = qb_ref[0]
        m_scr[...] = jnp.full_like(m_scr, NEG)
        l_scr[...] = jnp.zeros_like(l_scr)
        acc_scr[...] = jnp.zeros_like(acc_scr)

    s = lax.dot_general(q_scr[...], k_ref[...], (((1,), (1,)), ((), ())), preferred_element_type=F32)
    m_prev = m_scr[...]
    m_new = jnp.maximum(m_prev, jnp.max(s, axis=1, keepdims=True))
    alpha = jnp.exp2(m_prev - m_new)
    p = jnp.exp2(s - m_new)
    l_scr[...] = alpha * l_scr[...] + jnp.sum(p, axis=1, keepdims=True)
    acc_scr[...] = alpha * acc_scr[...] + jnp.dot(p.astype(BF16), v_ref[...], preferred_element_type=F32)
    m_scr[...] = m_new

    @pl.when(kj == pl.num_programs(2) - 1)
    def _():
        lam = lam_ref[...]
        lam_full = (jnp.exp(jnp.sum(lam[0:1] * lam[1:2], axis=1, keepdims=True))
                    - jnp.exp(jnp.sum(lam[2:3] * lam[3:4], axis=1, keepdims=True)) + lam_init)
        o = acc_scr[0:tq, :] / l_scr[0:tq, :] - lam_full * (acc_scr[tq:2 * tq, :] / l_scr[tq:2 * tq, :])
        ms = jnp.mean(o * o, axis=-1, keepdims=True)
        o_ref[...] = (o * lax.rsqrt(ms + EPS) * sg_ref[...] * (1.0 - lam_init)).astype(BF16)


def _flash_diff_attention(q2, kn, z, lam, subln_g, *, v_blk0, lam_init):
    _, S, W = q2.shape
    H = A_HEADS
    dv = W // H
    tq = min(512, S)
    tk = min(512, S)
    return pl.pallas_call(
        functools.partial(_flash_kernel, tq=tq, lam_init=lam_init),
        grid=(H, S // tq, S // tk),
        in_specs=[pl.BlockSpec((1, tq, dv), lambda h, i, j: (0, i, h)),
                  pl.BlockSpec((1, tq, dv), lambda h, i, j: (1, i, h)),
                  pl.BlockSpec((tk, dv), lambda h, i, j: (j, h)),
                  pl.BlockSpec((tk, dv), lambda h, i, j: (j, v_blk0 + h)),
                  pl.BlockSpec(lam.shape, lambda h, i, j: (0, 0)),
                  pl.BlockSpec((1, dv), lambda h, i, j: (0, 0))],
        out_specs=pl.BlockSpec((tq, dv), lambda h, i, j: (i, h)),
        out_shape=jax.ShapeDtypeStruct((S, W), BF16),
        scratch_shapes=[pltpu.VMEM((2 * tq, dv), BF16), pltpu.VMEM((2 * tq, 1), F32),
                        pltpu.VMEM((2 * tq, 1), F32), pltpu.VMEM((2 * tq, dv), F32)],
        compiler_params=_params("arbitrary", "arbitrary", "arbitrary"),
        name="flash_diff_attn",
    )(q2, q2, kn, z, lam, subln_g)


def _merge_kernel(ym_ref, ya_ref, wm_ref, wa_ref, gm_ref, ga_ref, o_ref):
    a = jnp.dot(ym_ref[...], wm_ref[...], preferred_element_type=F32)
    b = jnp.dot(ya_ref[...], wa_ref[...], preferred_element_type=F32)
    o_ref[...] = (_sigmoid(gm_ref[...].astype(F32)) * a + _sigmoid(ga_ref[...].astype(F32)) * b).astype(BF16)


def _merge(y_m, y_a, w_m, w_a, z, *, gm_col, ga_col):
    S, W = y_m.shape
    D = w_m.shape[1]
    tm, tn = min(512, S), 1024
    return pl.pallas_call(
        _merge_kernel,
        grid=(S // tm, D // tn),
        in_specs=[pl.BlockSpec((tm, W), lambda i, j: (i, 0)),
                  pl.BlockSpec((tm, W), lambda i, j: (i, 0)),
                  pl.BlockSpec((W, tn), lambda i, j: (0, j)),
                  pl.BlockSpec((W, tn), lambda i, j: (0, j)),
                  pl.BlockSpec((tm, tn), lambda i, j: (i, gm_col // tn + j)),
                  pl.BlockSpec((tm, tn), lambda i, j: (i, ga_col // tn + j))],
        out_specs=pl.BlockSpec((tm, tn), lambda i, j: (i, j)),
        out_shape=jax.ShapeDtypeStruct((S, D), BF16),
        compiler_params=_params("arbitrary", "arbitrary"),
        name="branch_merge",
    )(y_m, y_a, w_m, w_a, z, z)


def _route(logits):
    G, E = N_GROUPS, EXPERTS_PER_GROUP
    lane = lax.broadcasted_iota(jnp.int32, logits.shape, 1)
    lanef = lane.astype(F32)
    is_g = lane < G
    gl = jnp.where(is_g, logits, NEG)
    ge = jnp.where(is_g, jnp.exp(gl - jnp.max(gl, axis=1, keepdims=True)), 0.0)
    probs = ge / jnp.sum(ge, axis=1, keepdims=True)
    g_val = jnp.max(probs, axis=1, keepdims=True)
    g_idx = jnp.min(jnp.where(is_g & (probs == g_val), lanef, float(LANES)), axis=1, keepdims=True)
    lo = G + E * g_idx
    in_grp = (lanef >= lo) & (lanef < lo + E)
    el = jnp.where(in_grp, logits, NEG)
    v1 = jnp.max(el, axis=1, keepdims=True)
    i1 = jnp.min(jnp.where(in_grp & (el == v1), lanef, float(LANES)), axis=1, keepdims=True)
    rest = in_grp & (lanef != i1)
    el2 = jnp.where(rest, logits, NEG)
    v2 = jnp.max(el2, axis=1, keepdims=True)
    i2 = jnp.min(jnp.where(rest & (el2 == v2), lanef, float(LANES)), axis=1, keepdims=True)
    t = jnp.exp(v2 - v1)
    w1 = (1.0 / (1.0 + t)) * g_val
    w2 = (t / (1.0 + t)) * g_val
    return jnp.where(lane == 0, i1 - G, jnp.where(lane == 1, i2 - G,
                     jnp.where(lane == 2, w1, jnp.where(lane == 3, w2, 0.0))))


def _out_kernel(mg_ref, w_ref, x_ref, g1_ref, n2_ref, sc_ref, sh_ref, wr_ref, br_ref,
                x1_ref, h2_ref, r_ref):
    y = jnp.dot(mg_ref[...], w_ref[...], preferred_element_type=F32)
    x1 = x_ref[...] + g1_ref[...] * y
    x1_ref[...] = x1
    ms = jnp.mean(x1 * x1, axis=-1, keepdims=True)
    h2 = x1 * lax.rsqrt(ms + EPS) * n2_ref[...] * (1.0 + sc_ref[...]) + sh_ref[...]
    h2_ref[...] = h2
    logits = jnp.dot(h2, wr_ref[...], preferred_element_type=F32,
                     precision=lax.Precision.HIGHEST) + br_ref[...]
    r_ref[...] = _route(logits)


def _out_projection(merged, w_out, x, g1, n2g, sc2, sh2, w_router, b_router):
    S, D = x.shape
    tm = min(256, S)
    vec = pl.BlockSpec((1, D), lambda i: (0, 0))
    row = pl.BlockSpec((tm, D), lambda i: (i, 0))
    return pl.pallas_call(
        _out_kernel,
        grid=(S // tm,),
        in_specs=[row, pl.BlockSpec((D, D), lambda i: (0, 0)), row, vec, vec, vec, vec,
                  pl.BlockSpec((D, LANES), lambda i: (0, 0)), pl.BlockSpec((1, LANES), lambda i: (0, 0))],
        out_specs=[row, row, pl.BlockSpec((tm, LANES), lambda i: (i, 0))],
        out_shape=[jax.ShapeDtypeStruct((S, D), F32), jax.ShapeDtypeStruct((S, D), F32),
                   jax.ShapeDtypeStruct((S, LANES), F32)],
        compiler_params=_params("arbitrary"),
        name="out_proj_router",
    )(merged, w_out, x, g1, n2g, sc2, sh2, w_router, b_router)


def _expert_kernel(te_ref, src_ref, na_ref, h2_hbm, rw_ref, wg_ref, wu_ref, wd_ref, o_ref,
                   xbuf, sem, wg_b, wu_b, wd_b, *, tm):
    i = pl.program_id(0)
    n_active = na_ref[0]

    def issue(tile, slot):
        def body(r, carry):
            tok = src_ref[tile * tm + r]
            pltpu.make_async_copy(h2_hbm.at[pl.ds(tok, 1), :], xbuf.at[slot, pl.ds(r, 1), :],
                                  sem.at[slot]).start()
            return carry
        lax.fori_loop(0, tm, body, 0)

    @pl.when(i == 0)
    def _():
        issue(0, 0)

    @pl.when(i + 1 < n_active)
    def _():
        issue(i + 1, (i + 1) % 2)

    @pl.when(i < n_active)
    def _():
        slot = i % 2
        pltpu.make_async_copy(h2_hbm.at[pl.ds(0, tm), :], xbuf.at[slot], sem.at[slot]).wait()

        @pl.when((i == 0) | (te_ref[i] != te_ref[jnp.maximum(i - 1, 0)]))
        def _():
            wg_b[...] = wg_ref[0, 0].astype(BF16)
            wu_b[...] = wu_ref[0, 0].astype(BF16)
            wd_b[...] = wd_ref[0, 0].astype(BF16)

        x = xbuf[slot].astype(BF16)
        g = jnp.dot(x, wg_b[...], preferred_element_type=F32)
        u = jnp.dot(x, wu_b[...], preferred_element_type=F32)
        a = (g * _sigmoid(g) * u).astype(BF16)
        o_ref[...] = jnp.dot(a, wd_b[...], preferred_element_type=F32) * rw_ref[...]

    @pl.when(i >= n_active)
    def _():
        o_ref[...] = jnp.zeros_like(o_ref)


def _experts(h2, tile_expert, src_tok, n_active, row_w, w_gate, w_up, w_down, *, layer, tm):
    T, D = h2.shape
    Fe = w_gate.shape[-1]
    P = src_tok.shape[0]
    n_tiles = P // tm
    grid_spec = pltpu.PrefetchScalarGridSpec(
        num_scalar_prefetch=3,
        grid=(n_tiles,),
        in_specs=[pl.BlockSpec(memory_space=pl.ANY),
                  pl.BlockSpec((tm, 1), lambda i, te, src, na: (i, 0)),
                  pl.BlockSpec((1, 1, D, Fe), lambda i, te, src, na: (layer, te[i], 0, 0)),
                  pl.BlockSpec((1, 1, D, Fe), lambda i, te, src, na: (layer, te[i], 0, 0)),
                  pl.BlockSpec((1, 1, Fe, D), lambda i, te, src, na: (layer, te[i], 0, 0))],
        out_specs=pl.BlockSpec((tm, D), lambda i, te, src, na: (i, 0)),
        scratch_shapes=[pltpu.VMEM((2, tm, D), F32), pltpu.SemaphoreType.DMA((2,)),
                        pltpu.VMEM((D, Fe), BF16), pltpu.VMEM((D, Fe), BF16), pltpu.VMEM((Fe, D), BF16)],
    )
    return pl.pallas_call(
        functools.partial(_expert_kernel, tm=tm),
        grid_spec=grid_spec,
        out_shape=jax.ShapeDtypeStruct((P, D), F32),
        compiler_params=_params("arbitrary"),
        name="routed_experts",
    )(tile_expert, src_tok, n_active, h2, row_w, w_gate, w_up, w_down)


def _combine_kernel(pos_ref, ys_hbm, x_ref, g2_ref, o_ref, ybuf, sem, *, tm, T):
    i = pl.program_id(0)

    def body(r, carry):
        for slot in range(2):
            p = pos_ref[slot * T + i * tm + r]
            pltpu.make_async_copy(ys_hbm.at[pl.ds(p, 1), :], ybuf.at[slot, pl.ds(r, 1), :], sem.at[0]).start()
        return carry

    lax.fori_loop(0, tm, body, 0)
    for slot in range(2):
        pltpu.make_async_copy(ys_hbm.at[pl.ds(0, tm), :], ybuf.at[slot], sem.at[0]).wait()
    o_ref[...] = x_ref[...] + g2_ref[...] * (ybuf[0] + ybuf[1])


def _combine(y_sorted, pos, x1, g2, *, tm):
    T, D = x1.shape
    grid_spec = pltpu.PrefetchScalarGridSpec(
        num_scalar_prefetch=1,
        grid=(T // tm,),
        in_specs=[pl.BlockSpec(memory_space=pl.ANY),
                  pl.BlockSpec((tm, D), lambda i, pos: (i, 0)),
                  pl.BlockSpec((1, D), lambda i, pos: (0, 0))],
        out_specs=pl.BlockSpec((tm, D), lambda i, pos: (i, 0)),
        scratch_shapes=[pltpu.VMEM((2, tm, D), F32), pltpu.SemaphoreType.DMA((1,))],
    )
    return pl.pallas_call(
        functools.partial(_combine_kernel, tm=tm, T=T),
        grid_spec=grid_spec,
        out_shape=jax.ShapeDtypeStruct((T, D), F32),
        compiler_params=_params("arbitrary"),
        name="moe_combine",
    )(pos, y_sorted, x1, g2)


def _dispatch_plan(route, n_experts, tm):
    T = route.shape[0]
    e_flat = jnp.concatenate([route[:, 0], route[:, 1]]).astype(jnp.int32)
    w_flat = jnp.concatenate([route[:, 2], route[:, 3]])
    tok_flat = jnp.concatenate([jnp.arange(T, dtype=jnp.int32)] * 2)
    order = jnp.argsort(e_flat, stable=True)
    sorted_e = e_flat[order]
    counts = jnp.zeros((n_experts,), jnp.int32).at[e_flat].add(1)
    padded = ((counts + tm - 1) // tm) * tm
    pad_end = jnp.cumsum(padded)
    pad_start = pad_end - padded
    seg_start = jnp.cumsum(counts) - counts
    ppos = pad_start[sorted_e] + (jnp.arange(2 * T, dtype=jnp.int32) - seg_start[sorted_e])
    P = 2 * T + n_experts * tm
    src_tok = jnp.zeros((P,), jnp.int32).at[ppos].set(tok_flat[order])
    row_w = jnp.zeros((P,), F32).at[ppos].set(w_flat[order])
    pos = jnp.zeros((2 * T,), jnp.int32).at[order].set(ppos)
    n_active = (pad_end[-1] // tm).astype(jnp.int32)
    tile_start = jnp.arange(P // tm, dtype=jnp.int32) * tm
    tile_expert = jnp.searchsorted(pad_end, jnp.minimum(tile_start, pad_end[-1] - 1), side="right")
    return (tile_expert.astype(jnp.int32), src_tok, n_active.reshape(1), row_w.reshape(P, 1), pos)


def _rope_tables(S, hd):
    rope_dim = hd // 4
    half = rope_dim // 2
    posn = jnp.arange(S, dtype=F32)
    inv = ROPE_THETA ** (-jnp.arange(0, rope_dim, 2, dtype=F32) / rope_dim)
    ang = posn[:, None] * inv[None, :]
    cos, sin = jnp.cos(ang), jnp.sin(ang)
    rest = hd - rope_dim
    cos_t = jnp.concatenate([cos, cos, jnp.ones((S, rest), F32)], axis=1)
    sin_t = jnp.concatenate([-sin, sin, jnp.zeros((S, rest), F32)], axis=1)
    reps = LANES // hd
    return jnp.tile(cos_t, (1, reps)), jnp.tile(sin_t, (1, reps))


def kernel(x, c, ada_w, ada_b, norm1_g, norm2_g, w_in, m_conv_w, m_gate_b, m_norm_g, a_qnorm_g,
           a_knorm_g, a_lambda, a_subln_g, w_branch_m, w_branch_a, w_out, rg_w, rg_b, re_w, re_b,
           e_w_gate, e_w_up, e_w_down):
    B, S, D = x.shape
    assert B == 1
    depth = ada_w.shape[0]
    Wm = w_branch_m.shape[1]
    Wa = w_branch_a.shape[1]
    qk_w = m_conv_w.shape[2]
    n_gate = 4 * M_HEADS
    hd = a_qnorm_g.shape[1]
    n_experts = re_w.shape[2]
    assert Wa // A_HEADS == LANES and 2 * hd == LANES
    c_mv, c_mo, c_aq = qk_w, qk_w + Wm, qk_w + 2 * Wm
    c_ak, c_av, c_gm = c_aq + Wa, c_aq + 2 * Wa, c_aq + 3 * Wa
    c_ga = c_gm + D
    g0 = qk_w + Wm

    xs = x[0]
    mod = _modulation(c, ada_w, ada_b)
    cos_t, sin_t = _rope_tables(S, hd)
    tm_moe = 256

    for l in range(depth):
        lam_init = 0.8 - 0.6 * math.exp(-0.3 * l)
        sh1, sc1, g1, sh2, sc2, g2 = [mod[l, :, i * D:(i + 1) * D] for i in range(6)]
        w_l = w_in[l]
        w_main = jnp.concatenate([w_l[:, :g0], w_l[:, g0 + n_gate:]], axis=1).astype(BF16)
        w_gate = jnp.pad(w_l[:, g0:g0 + n_gate], ((0, 0), (0, LANES - n_gate))).astype(BF16)
        z, gates = _in_projection(xs, norm1_g[l][None], sc1, sh1, w_main, w_gate)
        qk = _qk_conv(z, m_conv_w[l], float((qk_w // (2 * M_HEADS)) ** -0.5))
        y_m = _mlstm(qk, z, gates, gates[:, :n_gate].T, m_gate_b[l], m_norm_g[l][None],
                     v_blk=c_mv // Wm, mo_blk=c_mo // Wm)
        q2, kn = _qk_prep(z, cos_t, sin_t, jnp.tile(a_qnorm_g[l], Wa // hd)[None],
                          jnp.tile(a_knorm_g[l], Wa // hd)[None],
                          q_blk=c_aq // Wa, k_blk=c_ak // Wa, hd=hd,
                          q_scale=float(hd ** -0.5 * math.log2(math.e)))
        y_a = _flash_diff_attention(q2, kn, z, a_lambda[l], a_subln_g[l][None],
                                    v_blk0=c_av // LANES, lam_init=lam_init)
        merged = _merge(y_m, y_a, w_branch_m[l].astype(BF16), w_branch_a[l].astype(BF16), z,
                        gm_col=c_gm, ga_col=c_ga)
        w_router = jnp.pad(jnp.concatenate([rg_w[l], re_w[l]], axis=1),
                           ((0, 0), (0, LANES - N_GROUPS - n_experts)))
        b_router = jnp.pad(jnp.concatenate([rg_b[l], re_b[l]]), (0, LANES - N_GROUPS - n_experts))[None]
        x1, h2, route = _out_projection(merged, w_out[l].astype(BF16), xs, g1, norm2_g[l][None],
                                        sc2, sh2, w_router, b_router)
        tile_expert, src_tok, n_active, row_w, pos = _dispatch_plan(route, n_experts, tm_moe)
        y_sorted = _experts(h2, tile_expert, src_tok, n_active, row_w, e_w_gate, e_w_up, e_w_down,
                            layer=l, tm=tm_moe)
        xs = _combine(y_sorted, pos, x1, g2, tm=tm_moe)
    return xs[None]
```

```python
import functools
import math

import jax
import jax.numpy as jnp
from jax import lax
from jax.experimental import pallas as pl
from jax.experimental.pallas import tpu as pltpu

F32 = jnp.float32
BF16 = jnp.bfloat16

M_HEADS = 4
A_HEADS = 8
CONV_WIDTH = 5
ROPE_THETA = 500000.0
N_GROUPS = 4
EXPERTS_PER_GROUP = 8
EPS = 1e-6
NEG = -1e30
LANES = 128
BF16_SUBLANES = 16
VMEM_LIMIT = 56 * 1024 * 1024


def _sigmoid(x):
    return 1.0 / (1.0 + jnp.exp(-x))


def _log_sigmoid(x):
    return jnp.minimum(x, 0.0) - jnp.log(1.0 + jnp.exp(-jnp.abs(x)))


def _params(*sem):
    return pltpu.CompilerParams(dimension_semantics=sem, vmem_limit_bytes=VMEM_LIMIT)


def _mod_kernel(c_ref, w_ref, b_ref, o_ref):
    c = c_ref[...]
    ca = c * _sigmoid(c)
    o_ref[0] = jnp.sum(ca * w_ref[0], axis=0, keepdims=True) + b_ref[0]


def _modulation(c, ada_w, ada_b):
    L, D, N = ada_w.shape
    tn = min(1024, N)
    return pl.pallas_call(
        _mod_kernel,
        grid=(L, N // tn),
        in_specs=[pl.BlockSpec((D, 1), lambda l, j: (0, 0)),
                  pl.BlockSpec((1, D, tn), lambda l, j: (l, 0, j)),
                  pl.BlockSpec((1, 1, tn), lambda l, j: (l, 0, j))],
        out_specs=pl.BlockSpec((1, 1, tn), lambda l, j: (l, 0, j)),
        out_shape=jax.ShapeDtypeStruct((L, 1, N), F32),
        compiler_params=_params("arbitrary", "arbitrary"),
        name="adaln_mod",
    )(c.reshape(D, 1), ada_w, ada_b.reshape(L, 1, N))


def _inproj_kernel(x_ref, g_ref, sc_ref, sh_ref, w_ref, wg_ref, z_ref, gate_ref, h_scr):
    @pl.when(pl.program_id(1) == 0)
    def _():
        x = x_ref[...]
        ms = jnp.mean(x * x, axis=-1, keepdims=True)
        h = x * lax.rsqrt(ms + EPS) * g_ref[...] * (1.0 + sc_ref[...]) + sh_ref[...]
        hb = h.astype(BF16)
        h_scr[...] = hb
        gate_ref[...] = jnp.dot(hb, wg_ref[...], preferred_element_type=F32)

    z_ref[...] = jnp.dot(h_scr[...], w_ref[...], preferred_element_type=F32).astype(BF16)


def _in_projection(x, g, sc, sh, w_main, w_gate):
    S, D = x.shape
    N = w_main.shape[1]
    tm, tn = min(512, S), 1024
    vec = pl.BlockSpec((1, D), lambda i, j: (0, 0))
    return pl.pallas_call(
        _inproj_kernel,
        grid=(S // tm, N // tn),
        in_specs=[pl.BlockSpec((tm, D), lambda i, j: (i, 0)), vec, vec, vec,
                  pl.BlockSpec((D, tn), lambda i, j: (0, j)),
                  pl.BlockSpec((D, LANES), lambda i, j: (0, 0))],
        out_specs=[pl.BlockSpec((tm, tn), lambda i, j: (i, j)),
                   pl.BlockSpec((tm, LANES), lambda i, j: (i, 0))],
        out_shape=[jax.ShapeDtypeStruct((S, N), BF16), jax.ShapeDtypeStruct((S, LANES), F32)],
        scratch_shapes=[pltpu.VMEM((tm, D), BF16)],
        compiler_params=_params("arbitrary", "arbitrary"),
        name="norm_inproj",
    )(x, g, sc, sh, w_main, w_gate)


def _conv_kernel(prev_ref, cur_ref, next_ref, w_ref, o_ref, ext_scr, *, ts, q_scale):
    i = pl.program_id(0)
    halo = BF16_SUBLANES
    pad = CONV_WIDTH // 2
    ext_scr[0:halo, :] = jnp.where(i > 0, prev_ref[...].astype(F32), 0.0)
    ext_scr[halo:halo + ts, :] = cur_ref[...].astype(F32)
    ext_scr[halo + ts:2 * halo + ts, :] = jnp.where(i < pl.num_programs(0) - 1,
                                                    next_ref[...].astype(F32), 0.0)
    acc = w_ref[0:1, :] * ext_scr[pl.ds(halo - pad, ts), :]
    for k in range(1, CONV_WIDTH):
        acc = acc + w_ref[k:k + 1, :] * ext_scr[pl.ds(halo - pad + k, ts), :]
    y = acc * _sigmoid(acc)
    C = y.shape[1]
    col = lax.broadcasted_iota(jnp.int32, (1, C), 1)
    o_ref[...] = (y * jnp.where(col < C // 2, q_scale, 1.0)).astype(BF16)


def _qk_conv(z, conv_w, q_scale):
    S = z.shape[0]
    C = conv_w.shape[1]
    ts = min(512, S)
    halo = BF16_SUBLANES
    r = ts // halo
    nblk = S // halo
    return pl.pallas_call(
        functools.partial(_conv_kernel, ts=ts, q_scale=q_scale),
        grid=(S // ts,),
        in_specs=[pl.BlockSpec((halo, C), lambda i: (jnp.maximum(i * r - 1, 0), 0)),
                  pl.BlockSpec((ts, C), lambda i: (i, 0)),
                  pl.BlockSpec((halo, C), lambda i: (jnp.minimum((i + 1) * r, nblk - 1), 0)),
                  pl.BlockSpec((CONV_WIDTH, C), lambda i: (0, 0))],
        out_specs=pl.BlockSpec((ts, C), lambda i: (i, 0)),
        out_shape=jax.ShapeDtypeStruct((S, C), BF16),
        scratch_shapes=[pltpu.VMEM((ts + 2 * halo, C), F32)],
        compiler_params=_params("arbitrary"),
        name="qk_conv",
    )(z, z, z, conv_w)


def _mlstm_kernel(qk_ref, v_ref, g_ref, gt_ref, gb_ref, gbt_ref, *rest, reverse, dk, dv, L):
    H = M_HEADS
    if reverse:
        hf_ref, mo_ref, ng_ref, y_ref, C_scr, n_scr, m_scr = rest
    else:
        h_ref, C_scr, n_scr, m_scr = rest

    @pl.when(pl.program_id(0) == 0)
    def _():
        C_scr[...] = jnp.zeros_like(C_scr)
        n_scr[...] = jnp.zeros_like(n_scr)
        m_scr[...] = jnp.zeros_like(m_scr)

    gates = g_ref[...] + gb_ref[...]
    gates_t = gt_ref[...] + gbt_ref[...]
    row = lax.broadcasted_iota(jnp.int32, (L, L), 0)
    col = lax.broadcasted_iota(jnp.int32, (L, L), 1)
    mask = (col >= row) if reverse else (col <= row)
    mask_t = (row >= col) if reverse else (row <= col)
    nt = (((1,), (1,)), ((), ()))
    tn = (((0,), (0,)), ((), ()))

    for h in range(H):
        gi = (2 * H if reverse else 0) + h
        gf = gi + H
        i_col = gates[:, gi:gi + 1]
        f_col = _log_sigmoid(gates[:, gf:gf + 1])
        i_row = gates_t[gi:gi + 1, :]
        f_row = _log_sigmoid(gates_t[gf:gf + 1, :])
        b_col = jnp.sum(jnp.where(mask, f_row, 0.0), axis=1, keepdims=True)
        b_row = jnp.sum(jnp.where(mask_t, f_col, 0.0), axis=0, keepdims=True)
        b_end = jnp.sum(f_row, axis=1, keepdims=True)
        m_prev = m_scr[h][:, 0:1]

        dmat = jnp.where(mask, b_col - b_row + i_row, NEG)
        m_inter = b_col + m_prev
        m_t = jnp.maximum(m_inter, jnp.max(dmat, axis=1, keepdims=True))
        q = qk_ref[:, h * dk:(h + 1) * dk]
        k = qk_ref[:, (H + h) * dk:(H + h + 1) * dk]
        v = v_ref[:, h * dv:(h + 1) * dv]
        s = lax.dot_general(q, k, nt, preferred_element_type=F32) * jnp.exp(dmat - m_t)
        inter = jnp.exp(m_inter - m_t)
        c_prev = C_scr[h]
        n_prev = n_scr[h]
        num = (inter * jnp.dot(q, c_prev.astype(BF16), preferred_element_type=F32)
               + jnp.dot(s.astype(BF16), v, preferred_element_type=F32))
        den = (inter * jnp.sum(q.astype(F32) * n_prev, axis=1, keepdims=True)
               + jnp.sum(s, axis=1, keepdims=True))
        hout = num / jnp.maximum(jnp.abs(den), jnp.exp(-m_t))

        g_col = b_end - b_col + i_col
        m_new = jnp.maximum(b_end + m_prev, jnp.max(g_col, axis=0, keepdims=True))
        w_col = jnp.exp(g_col - m_new)
        decay = jnp.exp(b_end + m_prev - m_new)
        kw = k.astype(F32) * w_col
        C_scr[h] = decay * c_prev + lax.dot_general(kw.astype(BF16), v, tn, preferred_element_type=F32)
        n_scr[h] = decay * n_prev + jnp.sum(kw, axis=0, keepdims=True)
        m_scr[h] = jnp.broadcast_to(m_new, (1, LANES))

        if reverse:
            hs = hout + hf_ref[:, h * dv:(h + 1) * dv]
            ms = jnp.mean(hs * hs, axis=-1, keepdims=True)
            y = hs * lax.rsqrt(ms + EPS) * ng_ref[:, h * dv:(h + 1) * dv]
            y = y * _sigmoid(mo_ref[:, h * dv:(h + 1) * dv].astype(F32))
            y_ref[:, h * dv:(h + 1) * dv] = y.astype(BF16)
        else:
            h_ref[:, h * dv:(h + 1) * dv] = hout


def _mlstm(qk, z, gates, gates_t, gate_b, norm_g, *, v_blk, mo_blk):
    S = qk.shape[0]
    H = M_HEADS
    dk = qk.shape[1] // (2 * H)
    W = norm_g.shape[1]
    dv = W // H
    L = min(256, S)
    nc = S // L
    gb = jnp.zeros((1, LANES), F32).at[0, :4 * H].set(gate_b.reshape(-1))
    gbt = gb[0, :4 * H].reshape(4 * H, 1)
    scratch = [pltpu.VMEM((H, dk, dv), F32), pltpu.VMEM((H, 1, dk), F32), pltpu.VMEM((H, 1, LANES), F32)]

    def specs(cmap):
        return [pl.BlockSpec((L, 2 * H * dk), lambda c: (cmap(c), 0)),
                pl.BlockSpec((L, W), lambda c: (cmap(c), v_blk)),
                pl.BlockSpec((L, LANES), lambda c: (cmap(c), 0)),
                pl.BlockSpec((4 * H, L), lambda c: (0, cmap(c))),
                pl.BlockSpec((1, LANES), lambda c: (0, 0)),
                pl.BlockSpec((4 * H, 1), lambda c: (0, 0))]

    fwd = lambda c: c
    h_fwd = pl.pallas_call(
        functools.partial(_mlstm_kernel, reverse=False, dk=dk, dv=dv, L=L),
        grid=(nc,),
        in_specs=specs(fwd),
        out_specs=pl.BlockSpec((L, W), lambda c: (c, 0)),
        out_shape=jax.ShapeDtypeStruct((S, W), F32),
        scratch_shapes=scratch,
        compiler_params=_params("arbitrary"),
        name="mlstm_fwd",
    )(qk, z, gates, gates_t, gb, gbt)

    bwd = lambda c: nc - 1 - c
    return pl.pallas_call(
        functools.partial(_mlstm_kernel, reverse=True, dk=dk, dv=dv, L=L),
        grid=(nc,),
        in_specs=specs(bwd) + [pl.BlockSpec((L, W), lambda c: (bwd(c), 0)),
                               pl.BlockSpec((L, W), lambda c: (bwd(c), mo_blk)),
                               pl.BlockSpec((1, W), lambda c: (0, 0))],
        out_specs=pl.BlockSpec((L, W), lambda c: (bwd(c), 0)),
        out_shape=jax.ShapeDtypeStruct((S, W), BF16),
        scratch_shapes=scratch,
        compiler_params=_params("arbitrary"),
        name="mlstm_bwd",
    )(qk, z, gates, gates_t, gb, gbt, h_fwd, z, norm_g)


def _qkprep_kernel(q_ref, k_ref, v_ref, cos_ref, sin_ref, qg_ref, kg_ref, qt_ref, kn_ref, vt_ref,
                   *, hd, q_scale):
    ts, W = q_ref.shape
    lane = lax.broadcasted_iota(jnp.int32, (ts, LANES), 1)
    lo = lane < hd
    first_half = (lane % hd) < (hd // 8)
    row_lo = lax.broadcasted_iota(jnp.int32, (LANES, ts), 0) < hd
    cos = cos_ref[...]
    sin = sin_ref[...]

    def norm_rope(x, g):
        sq = x * x
        s_lo = jnp.sum(jnp.where(lo, sq, 0.0), axis=1, keepdims=True)
        s_hi = jnp.sum(jnp.where(lo, 0.0, sq), axis=1, keepdims=True)
        inv = lax.rsqrt(jnp.where(lo, s_lo, s_hi) * (1.0 / hd) + EPS)
        xn = x * inv * g
        up = pltpu.roll(xn, LANES - hd // 8, axis=1)
        dn = pltpu.roll(xn, hd // 8, axis=1)
        return xn * cos + jnp.where(first_half, up, dn) * sin

    for j in range(W // LANES):
        sl = slice(j * LANES, (j + 1) * LANES)
        qn = norm_rope(q_ref[:, sl].astype(F32), qg_ref[:, sl]) * q_scale
        qn_t = qn.T
        qt_ref[0, sl, :] = jnp.where(row_lo, qn_t, 0.0).astype(BF16)
        qt_ref[1, sl, :] = jnp.where(row_lo, 0.0, qn_t).astype(BF16)
        kn_ref[:, sl] = norm_rope(k_ref[:, sl].astype(F32), kg_ref[:, sl]).astype(BF16)
        vt_ref[sl, :] = v_ref[:, sl].astype(F32).T.astype(BF16)


def _qk_prep(z, cos_t, sin_t, qg, kg, *, q_blk, k_blk, v_blk, hd, q_scale):
    S = z.shape[0]
    W = qg.shape[1]
    ts = min(512, S)
    return pl.pallas_call(
        functools.partial(_qkprep_kernel, hd=hd, q_scale=q_scale),
        grid=(S // ts,),
        in_specs=[pl.BlockSpec((ts, W), lambda i: (i, q_blk)),
                  pl.BlockSpec((ts, W), lambda i: (i, k_blk)),
                  pl.BlockSpec((ts, W), lambda i: (i, v_blk)),
                  pl.BlockSpec((ts, LANES), lambda i: (i, 0)),
                  pl.BlockSpec((ts, LANES), lambda i: (i, 0)),
                  pl.BlockSpec((1, W), lambda i: (0, 0)),
                  pl.BlockSpec((1, W), lambda i: (0, 0))],
        out_specs=[pl.BlockSpec((2, W, ts), lambda i: (0, 0, i)),
                   pl.BlockSpec((ts, W), lambda i: (i, 0)),
                   pl.BlockSpec((W, ts), lambda i: (0, i))],
        out_shape=[jax.ShapeDtypeStruct((2, W, S), BF16), jax.ShapeDtypeStruct((S, W), BF16),
                   jax.ShapeDtypeStruct((W, S), BF16)],
        compiler_params=_params("arbitrary"),
        name="attn_qk_prep",
    )(z, z, z, cos_t, sin_t, qg, kg)


ATTN_COL_BLOCK = 256
ATTN_KEY_BLOCK = 1024
ATTN_COL_GROUP = 4


def _flash_kernel(qa_ref, qb_ref, k_ref, vt_ref, lam_ref, sg_ref, o_ref, q_scr, m_scr, l_scr, acc_scr,
                  *, tq, tk, lam_init):
    kj = pl.program_id(2)
    cb_w, kb_w = ATTN_COL_BLOCK, min(ATTN_KEY_BLOCK, tk)

    @pl.when(kj == 0)
    def _():
        q_scr[:, 0:tq] = qa_ref[0]
        q_scr[:, tq:2 * tq] = qb_ref[0]
        m_scr[...] = jnp.full_like(m_scr, NEG)
        l_scr[...] = jnp.zeros_like(l_scr)
        acc_scr[...] = jnp.zeros_like(acc_scr)

    group = ATTN_COL_GROUP
    steps = [(c, kb) for kb in range(tk // kb_w) for c in range(group)]

    def col_group(gi, carry):
        cols = [pl.ds(pl.multiple_of((gi * group + c) * cb_w, cb_w), cb_w) for c in range(group)]
        qts = [q_scr[:, cs] for cs in cols]
        state = [(m_scr[:, cs], l_scr[:, cs], acc_scr[:, cs]) for cs in cols]

        def scores(c, kb):
            return jnp.dot(k_ref[kb * kb_w:(kb + 1) * kb_w, :], qts[c], preferred_element_type=F32)

        s_next = scores(*steps[0])
        for n, (c, kb) in enumerate(steps):
            s = s_next
            if n + 1 < len(steps):
                s_next = scores(*steps[n + 1])
            m, l, acc = state[c]
            m_new = jnp.maximum(m, jnp.max(s, axis=0, keepdims=True))
            alpha = jnp.exp2(m - m_new)
            p = jnp.exp2(s - m_new)
            l = alpha * l + jnp.sum(p, axis=0, keepdims=True)
            acc = alpha * acc + jnp.dot(vt_ref[:, kb * kb_w:(kb + 1) * kb_w], p.astype(BF16),
                                        preferred_element_type=F32)
            state[c] = (m_new, l, acc)
        for cs, (m, l, acc) in zip(cols, state):
            m_scr[:, cs], l_scr[:, cs], acc_scr[:, cs] = m, l, acc
        return carry

    lax.fori_loop(0, 2 * tq // (cb_w * group), col_group, 0)

    @pl.when(kj == pl.num_programs(2) - 1)
    def _():
        lam = lam_ref[...]
        lam_full = (jnp.exp(jnp.sum(lam[0:1] * lam[1:2], axis=1, keepdims=True))
                    - jnp.exp(jnp.sum(lam[2:3] * lam[3:4], axis=1, keepdims=True)) + lam_init)
        a = acc_scr[...] / l_scr[...]
        o_t = a[:, 0:tq] - lam_full * a[:, tq:2 * tq]
        ms = jnp.mean(o_t * o_t, axis=0, keepdims=True)
        y_t = o_t * lax.rsqrt(ms + EPS) * sg_ref[...] * (1.0 - lam_init)
        o_ref[...] = y_t.T.astype(BF16)


def _flash_diff_attention(q_t, kn, v_t, lam, subln_g, *, lam_init):
    _, W, S = q_t.shape
    H = A_HEADS
    dv = W // H
    tq = min(1024, S)
    tk = min(1024, S)
    return pl.pallas_call(
        functools.partial(_flash_kernel, tq=tq, tk=tk, lam_init=lam_init),
        grid=(H, S // tq, S // tk),
        in_specs=[pl.BlockSpec((1, dv, tq), lambda h, i, j: (0, h, i)),
                  pl.BlockSpec((1, dv, tq), lambda h, i, j: (1, h, i)),
                  pl.BlockSpec((tk, dv), lambda h, i, j: (j, h)),
                  pl.BlockSpec((dv, tk), lambda h, i, j: (h, j)),
                  pl.BlockSpec(lam.shape, lambda h, i, j: (0, 0)),
                  pl.BlockSpec((dv, 1), lambda h, i, j: (0, 0))],
        out_specs=pl.BlockSpec((tq, dv), lambda h, i, j: (i, h)),
        out_shape=jax.ShapeDtypeStruct((S, W), BF16),
        scratch_shapes=[pltpu.VMEM((dv, 2 * tq), BF16), pltpu.VMEM((1, 2 * tq), F32),
                        pltpu.VMEM((1, 2 * tq), F32), pltpu.VMEM((dv, 2 * tq), F32)],
        compiler_params=_params("arbitrary", "arbitrary", "arbitrary"),
        name="flash_diff_attn",
    )(q_t, q_t, kn, v_t, lam, subln_g.reshape(dv, 1))


def _merge_kernel(ym_ref, ya_ref, wm_ref, wa_ref, gm_ref, ga_ref, o_ref):
    a = jnp.dot(ym_ref[...], wm_ref[...], preferred_element_type=F32)
    b = jnp.dot(ya_ref[...], wa_ref[...], preferred_element_type=F32)
    o_ref[...] = (_sigmoid(gm_ref[...].astype(F32)) * a + _sigmoid(ga_ref[...].astype(F32)) * b).astype(BF16)


def _merge(y_m, y_a, w_m, w_a, z, *, gm_col, ga_col):
    S, W = y_m.shape
    D = w_m.shape[1]
    tm, tn = min(512, S), 1024
    return pl.pallas_call(
        _merge_kernel,
        grid=(S // tm, D // tn),
        in_specs=[pl.BlockSpec((tm, W), lambda i, j: (i, 0)),
                  pl.BlockSpec((tm, W), lambda i, j: (i, 0)),
                  pl.BlockSpec((W, tn), lambda i, j: (0, j)),
                  pl.BlockSpec((W, tn), lambda i, j: (0, j)),
                  pl.BlockSpec((tm, tn), lambda i, j: (i, gm_col // tn + j)),
                  pl.BlockSpec((tm, tn), lambda i, j: (i, ga_col // tn + j))],
        out_specs=pl.BlockSpec((tm, tn), lambda i, j: (i, j)),
        out_shape=jax.ShapeDtypeStruct((S, D), BF16),
        compiler_params=_params("arbitrary", "arbitrary"),
        name="branch_merge",
    )(y_m, y_a, w_m, w_a, z, z)


def _route(logits):
    G, E = N_GROUPS, EXPERTS_PER_GROUP
    lane = lax.broadcasted_iota(jnp.int32, logits.shape, 1)
    lanef = lane.astype(F32)
    is_g = lane < G
    gl = jnp.where(is_g, logits, NEG)
    ge = jnp.where(is_g, jnp.exp(gl - jnp.max(gl, axis=1, keepdims=True)), 0.0)
    probs = ge / jnp.sum(ge, axis=1, keepdims=True)
    g_val = jnp.max(probs, axis=1, keepdims=True)
    g_idx = jnp.min(jnp.where(is_g & (probs == g_val), lanef, float(LANES)), axis=1, keepdims=True)
    lo = G + E * g_idx
    in_grp = (lanef >= lo) & (lanef < lo + E)
    el = jnp.where(in_grp, logits, NEG)
    v1 = jnp.max(el, axis=1, keepdims=True)
    i1 = jnp.min(jnp.where(in_grp & (el == v1), lanef, float(LANES)), axis=1, keepdims=True)
    rest = in_grp & (lanef != i1)
    el2 = jnp.where(rest, logits, NEG)
    v2 = jnp.max(el2, axis=1, keepdims=True)
    i2 = jnp.min(jnp.where(rest & (el2 == v2), lanef, float(LANES)), axis=1, keepdims=True)
    t = jnp.exp(v2 - v1)
    w1 = (1.0 / (1.0 + t)) * g_val
    w2 = (t / (1.0 + t)) * g_val
    return jnp.where(lane == 0, i1 - G, jnp.where(lane == 1, i2 - G,
                     jnp.where(lane == 2, w1, jnp.where(lane == 3, w2, 0.0))))


def _out_kernel(mg_ref, w_ref, x_ref, g1_ref, n2_ref, sc_ref, sh_ref, wr_ref, br_ref,
                x1_ref, h2_ref, r_ref):
    y = jnp.dot(mg_ref[...], w_ref[...], preferred_element_type=F32)
    x1 = x_ref[...] + g1_ref[...] * y
    x1_ref[...] = x1
    ms = jnp.mean(x1 * x1, axis=-1, keepdims=True)
    h2 = x1 * lax.rsqrt(ms + EPS) * n2_ref[...] * (1.0 + sc_ref[...]) + sh_ref[...]
    h2_ref[...] = h2
    h_hi = h2.astype(BF16)
    h_lo = (h2 - h_hi.astype(F32)).astype(BF16)
    t = jnp.dot(h_hi, wr_ref[...], preferred_element_type=F32)
    logits = (t[:, :LANES] + t[:, LANES:]
              + jnp.dot(h_lo, wr_ref[:, :LANES], preferred_element_type=F32) + br_ref[...])
    r_ref[...] = _route(logits)


def _out_projection(merged, w_out, x, g1, n2g, sc2, sh2, w_router, b_router):
    S, D = x.shape
    tm = min(256, S)
    vec = pl.BlockSpec((1, D), lambda i: (0, 0))
    row = pl.BlockSpec((tm, D), lambda i: (i, 0))
    return pl.pallas_call(
        _out_kernel,
        grid=(S // tm,),
        in_specs=[row, pl.BlockSpec((D, D), lambda i: (0, 0)), row, vec, vec, vec, vec,
                  pl.BlockSpec((D, 2 * LANES), lambda i: (0, 0)), pl.BlockSpec((1, LANES), lambda i: (0, 0))],
        out_specs=[row, row, pl.BlockSpec((tm, LANES), lambda i: (i, 0))],
        out_shape=[jax.ShapeDtypeStruct((S, D), F32), jax.ShapeDtypeStruct((S, D), F32),
                   jax.ShapeDtypeStruct((S, LANES), F32)],
        compiler_params=_params("arbitrary"),
        name="out_proj_router",
    )(merged, w_out, x, g1, n2g, sc2, sh2, w_router, b_router)


def _expert_kernel(te_ref, src_ref, na_ref, h2_hbm, rw_ref, wg_ref, wu_ref, wd_ref, o_ref,
                   xbuf, sem, wg_b, wu_b, wd_b, *, tm):
    i = pl.program_id(0)
    n_active = na_ref[0]

    def issue(tile, slot):
        def body(r, carry):
            tok = src_ref[tile * tm + r]
            pltpu.make_async_copy(h2_hbm.at[pl.ds(tok, 1), :], xbuf.at[slot, pl.ds(r, 1), :],
                                  sem.at[slot]).start()
            return carry
        lax.fori_loop(0, tm, body, 0)

    @pl.when(i == 0)
    def _():
        issue(0, 0)

    @pl.when(i + 1 < n_active)
    def _():
        issue(i + 1, (i + 1) % 2)

    @pl.when(i < n_active)
    def _():
        slot = i % 2
        pltpu.make_async_copy(h2_hbm.at[pl.ds(0, tm), :], xbuf.at[slot], sem.at[slot]).wait()

        @pl.when((i == 0) | (te_ref[i] != te_ref[jnp.maximum(i - 1, 0)]))
        def _():
            wg_b[...] = wg_ref[0, 0].astype(BF16)
            wu_b[...] = wu_ref[0, 0].astype(BF16)
            wd_b[...] = wd_ref[0, 0].astype(BF16)

        x = xbuf[slot].astype(BF16)
        g = jnp.dot(x, wg_b[...], preferred_element_type=F32)
        u = jnp.dot(x, wu_b[...], preferred_element_type=F32)
        a = (g * _sigmoid(g) * u).astype(BF16)
        o_ref[...] = jnp.dot(a, wd_b[...], preferred_element_type=F32) * rw_ref[...]

    @pl.when(i >= n_active)
    def _():
        o_ref[...] = jnp.zeros_like(o_ref)


def _experts(h2, tile_expert, src_tok, n_active, row_w, w_gate, w_up, w_down, *, layer, tm):
    T, D = h2.shape
    Fe = w_gate.shape[-1]
    P = src_tok.shape[0]
    n_tiles = P // tm
    grid_spec = pltpu.PrefetchScalarGridSpec(
        num_scalar_prefetch=3,
        grid=(n_tiles,),
        in_specs=[pl.BlockSpec(memory_space=pl.ANY),
                  pl.BlockSpec((tm, 1), lambda i, te, src, na: (i, 0)),
                  pl.BlockSpec((1, 1, D, Fe), lambda i, te, src, na: (layer, te[i], 0, 0)),
                  pl.BlockSpec((1, 1, D, Fe), lambda i, te, src, na: (layer, te[i], 0, 0)),
                  pl.BlockSpec((1, 1, Fe, D), lambda i, te, src, na: (layer, te[i], 0, 0))],
        out_specs=pl.BlockSpec((tm, D), lambda i, te, src, na: (i, 0)),
        scratch_shapes=[pltpu.VMEM((2, tm, D), F32), pltpu.SemaphoreType.DMA((2,)),
                        pltpu.VMEM((D, Fe), BF16), pltpu.VMEM((D, Fe), BF16), pltpu.VMEM((Fe, D), BF16)],
    )
    return pl.pallas_call(
        functools.partial(_expert_kernel, tm=tm),
        grid_spec=grid_spec,
        out_shape=jax.ShapeDtypeStruct((P, D), F32),
        compiler_params=_params("arbitrary"),
        name="routed_experts",
    )(tile_expert, src_tok, n_active, h2, row_w, w_gate, w_up, w_down)


def _combine_kernel(pos_ref, ys_hbm, x_ref, g2_ref, o_ref, ybuf, sem, *, tm, T):
    i = pl.program_id(0)

    def body(r, carry):
        for slot in range(2):
            p = pos_ref[slot * T + i * tm + r]
            pltpu.make_async_copy(ys_hbm.at[pl.ds(p, 1), :], ybuf.at[slot, pl.ds(r, 1), :], sem.at[0]).start()
        return carry

    lax.fori_loop(0, tm, body, 0)
    for slot in range(2):
        pltpu.make_async_copy(ys_hbm.at[pl.ds(0, tm), :], ybuf.at[slot], sem.at[0]).wait()
    o_ref[...] = x_ref[...] + g2_ref[...] * (ybuf[0] + ybuf[1])


def _combine(y_sorted, pos, x1, g2, *, tm):
    T, D = x1.shape
    grid_spec = pltpu.PrefetchScalarGridSpec(
        num_scalar_prefetch=1,
        grid=(T // tm,),
        in_specs=[pl.BlockSpec(memory_space=pl.ANY),
                  pl.BlockSpec((tm, D), lambda i, pos: (i, 0)),
                  pl.BlockSpec((1, D), lambda i, pos: (0, 0))],
        out_specs=pl.BlockSpec((tm, D), lambda i, pos: (i, 0)),
        scratch_shapes=[pltpu.VMEM((2, tm, D), F32), pltpu.SemaphoreType.DMA((1,))],
    )
    return pl.pallas_call(
        functools.partial(_combine_kernel, tm=tm, T=T),
        grid_spec=grid_spec,
        out_shape=jax.ShapeDtypeStruct((T, D), F32),
        compiler_params=_params("arbitrary"),
        name="moe_combine",
    )(pos, y_sorted, x1, g2)


def _dispatch_plan(route, n_experts, tm):
    T = route.shape[0]
    e_flat = jnp.concatenate([route[:, 0], route[:, 1]]).astype(jnp.int32)
    w_flat = jnp.concatenate([route[:, 2], route[:, 3]])
    order = jnp.argsort(e_flat, stable=True).astype(jnp.int32)
    sorted_e = e_flat[order]
    bounds = jnp.sum(sorted_e[None, :] < jnp.arange(n_experts + 1, dtype=jnp.int32)[:, None],
                     axis=1, dtype=jnp.int32)
    seg_start, counts = bounds[:-1], bounds[1:] - bounds[:-1]
    padded = ((counts + tm - 1) // tm) * tm
    pad_end = jnp.cumsum(padded)
    pad_start = pad_end - padded
    P = 2 * T + n_experts * tm
    n_active = (pad_end[-1] // tm).astype(jnp.int32)
    tile_start = jnp.arange(P // tm, dtype=jnp.int32) * tm
    tile_expert = jnp.sum(pad_end[None, :] <= jnp.minimum(tile_start, pad_end[-1] - 1)[:, None],
                          axis=1, dtype=jnp.int32)
    p_idx = jnp.arange(P, dtype=jnp.int32)
    e_p = jnp.repeat(tile_expert, tm)
    off = p_idx - pad_start[e_p]
    valid = off < counts[e_p]
    pair = order[jnp.clip(seg_start[e_p] + off, 0, 2 * T - 1)]
    src_tok = jnp.where(valid, pair % T, 0)
    row_w = jnp.where(valid, w_flat[pair], 0.0)
    ppos = pad_start[sorted_e] + (jnp.arange(2 * T, dtype=jnp.int32) - seg_start[sorted_e])
    pos = ppos[jnp.argsort(order)]
    return (tile_expert, src_tok, n_active.reshape(1), row_w.reshape(P, 1), pos)


def _rope_tables(S, hd):
    rope_dim = hd // 4
    half = rope_dim // 2
    posn = jnp.arange(S, dtype=F32)
    inv = ROPE_THETA ** (-jnp.arange(0, rope_dim, 2, dtype=F32) / rope_dim)
    ang = posn[:, None] * inv[None, :]
    cos, sin = jnp.cos(ang), jnp.sin(ang)
    rest = hd - rope_dim
    cos_t = jnp.concatenate([cos, cos, jnp.ones((S, rest), F32)], axis=1)
    sin_t = jnp.concatenate([-sin, sin, jnp.zeros((S, rest), F32)], axis=1)
    reps = LANES // hd
    return jnp.tile(cos_t, (1, reps)), jnp.tile(sin_t, (1, reps))


def kernel(x, c, ada_w, ada_b, norm1_g, norm2_g, w_in, m_conv_w, m_gate_b, m_norm_g, a_qnorm_g,
           a_knorm_g, a_lambda, a_subln_g, w_branch_m, w_branch_a, w_out, rg_w, rg_b, re_w, re_b,
           e_w_gate, e_w_up, e_w_down):
    B, S, D = x.shape
    assert B == 1
    depth = ada_w.shape[0]
    Wm = w_branch_m.shape[1]
    Wa = w_branch_a.shape[1]
    qk_w = m_conv_w.shape[2]
    n_gate = 4 * M_HEADS
    hd = a_qnorm_g.shape[1]
    n_experts = re_w.shape[2]
    assert Wa // A_HEADS == LANES and 2 * hd == LANES
    c_mv, c_mo, c_aq = qk_w, qk_w + Wm, qk_w + 2 * Wm
    c_ak, c_av, c_gm = c_aq + Wa, c_aq + 2 * Wa, c_aq + 3 * Wa
    c_ga = c_gm + D
    g0 = qk_w + Wm

    xs = x[0]
    mod = _modulation(c, ada_w, ada_b)
    cos_t, sin_t = _rope_tables(S, hd)
    tm_moe = 256

    for l in range(depth):
        lam_init = 0.8 - 0.6 * math.exp(-0.3 * l)
        sh1, sc1, g1, sh2, sc2, g2 = [mod[l, :, i * D:(i + 1) * D] for i in range(6)]
        w_l = w_in[l]
        w_main = jnp.concatenate([w_l[:, :g0], w_l[:, g0 + n_gate:]], axis=1).astype(BF16)
        w_gate = jnp.pad(w_l[:, g0:g0 + n_gate], ((0, 0), (0, LANES - n_gate))).astype(BF16)
        z, gates = _in_projection(xs, norm1_g[l][None], sc1, sh1, w_main, w_gate)
        qk = _qk_conv(z, m_conv_w[l], float((qk_w // (2 * M_HEADS)) ** -0.5))
        y_m = _mlstm(qk, z, gates, gates[:, :n_gate].T, m_gate_b[l], m_norm_g[l][None],
                     v_blk=c_mv // Wm, mo_blk=c_mo // Wm)
        q_t, kn, v_t = _qk_prep(z, cos_t, sin_t, jnp.tile(a_qnorm_g[l], Wa // hd)[None],
                                jnp.tile(a_knorm_g[l], Wa // hd)[None],
                                q_blk=c_aq // Wa, k_blk=c_ak // Wa, v_blk=c_av // Wa, hd=hd,
                                q_scale=float(hd ** -0.5 * math.log2(math.e)))
        y_a = _flash_diff_attention(q_t, kn, v_t, a_lambda[l], a_subln_g[l], lam_init=lam_init)
        merged = _merge(y_m, y_a, w_branch_m[l].astype(BF16), w_branch_a[l].astype(BF16), z,
                        gm_col=c_gm, ga_col=c_ga)
        w_router = jnp.pad(jnp.concatenate([rg_w[l], re_w[l]], axis=1),
                           ((0, 0), (0, LANES - N_GROUPS - n_experts)))
        w_r_hi = w_router.astype(BF16)
        w_router = jnp.concatenate([w_r_hi, (w_router - w_r_hi.astype(F32)).astype(BF16)], axis=1)
        b_router = jnp.pad(jnp.concatenate([rg_b[l], re_b[l]]), (0, LANES - N_GROUPS - n_experts))[None]
        x1, h2, route = _out_projection(merged, w_out[l].astype(BF16), xs, g1, norm2_g[l][None],
                                        sc2, sh2, w_router, b_router)
        tile_expert, src_tok, n_active, row_w, pos = _dispatch_plan(route, n_experts, tm_moe)
        y_sorted = _experts(h2, tile_expert, src_tok, n_active, row_w, e_w_gate, e_w_up, e_w_down,
                            layer=l, tm=tm_moe)
        xs = _combine(y_sorted, pos, x1, g2, tm=tm_moe)
    return xs[None]
```

```python
import functools
import math

import jax
import jax.numpy as jnp
from jax import lax
from jax.experimental import pallas as pl
from jax.experimental.pallas import tpu as pltpu

F32 = jnp.float32
BF16 = jnp.bfloat16

M_HEADS = 4
A_HEADS = 8
CONV_WIDTH = 5
ROPE_THETA = 500000.0
N_GROUPS = 4
EXPERTS_PER_GROUP = 8
EPS = 1e-6
NEG = -1e30
LANES = 128
SUBLANES = 8
BF16_SUBLANES = 16
VMEM_LIMIT = 56 * 1024 * 1024
MOE_TILE = 256


def _sigmoid(x):
    return 1.0 / (1.0 + jnp.exp(-x))


def _log_sigmoid(x):
    return jnp.minimum(x, 0.0) - jnp.log(1.0 + jnp.exp(-jnp.abs(x)))


def _params(*sem):
    return pltpu.CompilerParams(dimension_semantics=sem, vmem_limit_bytes=VMEM_LIMIT)


def _layer_vec(layer, width, blk=0):
    return pl.BlockSpec((None, 1, width), lambda *_: (layer, 0, blk))


def _mod_kernel(c_ref, w_ref, b_ref, o_ref):
    c = c_ref[...]
    ca = c * _sigmoid(c)
    o_ref[...] = jnp.sum(ca * w_ref[...], axis=0, keepdims=True) + b_ref[...]


def _modulation(c, ada_w, ada_b):
    L, D, N = ada_w.shape
    tn = min(1024, N)
    return pl.pallas_call(
        _mod_kernel,
        grid=(L, N // tn),
        in_specs=[pl.BlockSpec((D, 1), lambda l, j: (0, 0)),
                  pl.BlockSpec((None, D, tn), lambda l, j: (l, 0, j)),
                  pl.BlockSpec((None, 1, tn), lambda l, j: (l, 0, j))],
        out_specs=pl.BlockSpec((None, 1, tn), lambda l, j: (l, 0, j)),
        out_shape=jax.ShapeDtypeStruct((L, 1, N), F32),
        compiler_params=_params("arbitrary", "arbitrary"),
        name="adaln_mod",
    )(c.reshape(D, 1), ada_w, ada_b.reshape(L, 1, N))


def _inproj_kernel(x_ref, g_ref, sc_ref, sh_ref, w_ref, wg_ref, z_ref, gate_ref, gate_t_ref, h_scr):
    @pl.when(pl.program_id(1) == 0)
    def _():
        x = x_ref[...]
        ms = jnp.mean(x * x, axis=-1, keepdims=True)
        h = x * lax.rsqrt(ms + EPS) * g_ref[...] * (1.0 + sc_ref[...]) + sh_ref[...]
        hb = h.astype(BF16)
        h_scr[...] = hb
        g = jnp.dot(hb, wg_ref[...], preferred_element_type=F32)
        gate_ref[...] = g
        gate_t_ref[...] = g.T

    z_ref[...] = jnp.dot(h_scr[...], w_ref[...], preferred_element_type=F32).astype(BF16)


def _in_projection(x, norm_g, mod, w_main, w_gate, *, layer):
    S, D = x.shape
    N = w_main.shape[2]
    tm, tn = min(512, S), 1024
    return pl.pallas_call(
        _inproj_kernel,
        grid=(S // tm, N // tn),
        in_specs=[pl.BlockSpec((tm, D), lambda i, j: (i, 0)),
                  _layer_vec(layer, D), _layer_vec(layer, D, 1), _layer_vec(layer, D, 0),
                  pl.BlockSpec((None, D, tn), lambda i, j: (layer, 0, j)),
                  pl.BlockSpec((None, D, LANES), lambda i, j: (layer, 0, 0))],
        out_specs=[pl.BlockSpec((tm, tn), lambda i, j: (i, j)),
                   pl.BlockSpec((tm, LANES), lambda i, j: (i, 0)),
                   pl.BlockSpec((LANES, tm), lambda i, j: (0, i))],
        out_shape=[jax.ShapeDtypeStruct((S, N), BF16), jax.ShapeDtypeStruct((S, LANES), F32),
                   jax.ShapeDtypeStruct((LANES, S), F32)],
        scratch_shapes=[pltpu.VMEM((tm, D), BF16)],
        compiler_params=_params("arbitrary", "arbitrary"),
        name="norm_inproj",
    )(x, norm_g, mod, mod, w_main, w_gate)


def _conv_kernel(prev_ref, cur_ref, next_ref, w_ref, o_ref, ext_scr, *, ts, q_scale):
    i = pl.program_id(0)
    halo = BF16_SUBLANES
    pad = CONV_WIDTH // 2
    ext_scr[0:halo, :] = jnp.where(i > 0, prev_ref[...].astype(F32), 0.0)
    ext_scr[halo:halo + ts, :] = cur_ref[...].astype(F32)
    ext_scr[halo + ts:2 * halo + ts, :] = jnp.where(i < pl.num_programs(0) - 1,
                                                    next_ref[...].astype(F32), 0.0)
    acc = w_ref[0:1, :] * ext_scr[pl.ds(halo - pad, ts), :]
    for k in range(1, CONV_WIDTH):
        acc = acc + w_ref[k:k + 1, :] * ext_scr[pl.ds(halo - pad + k, ts), :]
    y = acc * _sigmoid(acc)
    C = y.shape[1]
    col = lax.broadcasted_iota(jnp.int32, (1, C), 1)
    o_ref[...] = (y * jnp.where(col < C // 2, q_scale, 1.0)).astype(BF16)


def _qk_conv(z, conv_w, q_scale, *, layer):
    S = z.shape[0]
    C = conv_w.shape[2]
    ts = min(512, S)
    halo = BF16_SUBLANES
    r = ts // halo
    nblk = S // halo
    return pl.pallas_call(
        functools.partial(_conv_kernel, ts=ts, q_scale=q_scale),
        grid=(S // ts,),
        in_specs=[pl.BlockSpec((halo, C), lambda i: (jnp.maximum(i * r - 1, 0), 0)),
                  pl.BlockSpec((ts, C), lambda i: (i, 0)),
                  pl.BlockSpec((halo, C), lambda i: (jnp.minimum((i + 1) * r, nblk - 1), 0)),
                  pl.BlockSpec((None, CONV_WIDTH, C), lambda i: (layer, 0, 0))],
        out_specs=pl.BlockSpec((ts, C), lambda i: (i, 0)),
        out_shape=jax.ShapeDtypeStruct((S, C), BF16),
        scratch_shapes=[pltpu.VMEM((ts + 2 * halo, C), F32)],
        compiler_params=_params("arbitrary"),
        name="qk_conv",
    )(z, z, z, conv_w)


def _mlstm_kernel(qk_ref, v_ref, g_ref, gt_ref, gb_ref, gbt_ref, *rest, reverse, dk, dv, L):
    H = M_HEADS
    if reverse:
        hf_ref, mo_ref, ng_ref, y_ref, C_scr, n_scr, m_scr = rest
    else:
        h_ref, C_scr, n_scr, m_scr = rest

    @pl.when(pl.program_id(0) == 0)
    def _():
        C_scr[...] = jnp.zeros_like(C_scr)
        n_scr[...] = jnp.zeros_like(n_scr)
        m_scr[...] = jnp.zeros_like(m_scr)

    gates = g_ref[...] + gb_ref[...]
    gates_t = gt_ref[...] + gbt_ref[...]
    row = lax.broadcasted_iota(jnp.int32, (L, L), 0)
    col = lax.broadcasted_iota(jnp.int32, (L, L), 1)
    mask = (col >= row) if reverse else (col <= row)
    mask_t = (row >= col) if reverse else (row <= col)
    nt = (((1,), (1,)), ((), ()))
    tn = (((0,), (0,)), ((), ()))

    for h in range(H):
        gi = (2 * H if reverse else 0) + h
        gf = gi + H
        i_col = gates[:, gi:gi + 1]
        f_col = _log_sigmoid(gates[:, gf:gf + 1])
        i_row = gates_t[gi:gi + 1, :]
        f_row = _log_sigmoid(gates_t[gf:gf + 1, :])
        b_col = jnp.sum(jnp.where(mask, f_row, 0.0), axis=1, keepdims=True)
        b_row = jnp.sum(jnp.where(mask_t, f_col, 0.0), axis=0, keepdims=True)
        b_end = jnp.sum(f_row, axis=1, keepdims=True)
        m_prev = m_scr[h][:, 0:1]

        dmat = jnp.where(mask, b_col - b_row + i_row, NEG)
        m_inter = b_col + m_prev
        m_t = jnp.maximum(m_inter, jnp.max(dmat, axis=1, keepdims=True))
        q = qk_ref[:, h * dk:(h + 1) * dk]
        k = qk_ref[:, (H + h) * dk:(H + h + 1) * dk]
        v = v_ref[:, h * dv:(h + 1) * dv]
        s = lax.dot_general(q, k, nt, preferred_element_type=F32) * jnp.exp(dmat - m_t)
        inter = jnp.exp(m_inter - m_t)
        c_prev = C_scr[h]
        n_prev = n_scr[h]
        num = (inter * jnp.dot(q, c_prev.astype(BF16), preferred_element_type=F32)
               + jnp.dot(s.astype(BF16), v, preferred_element_type=F32))
        den = (inter * jnp.sum(q.astype(F32) * n_prev, axis=1, keepdims=True)
               + jnp.sum(s, axis=1, keepdims=True))
        hout = num / jnp.maximum(jnp.abs(den), jnp.exp(-m_t))

        g_col = b_end - b_col + i_col
        m_new = jnp.maximum(b_end + m_prev, jnp.max(g_col, axis=0, keepdims=True))
        w_col = jnp.exp(g_col - m_new)
        decay = jnp.exp(b_end + m_prev - m_new)
        kw = k.astype(F32) * w_col
        C_scr[h] = decay * c_prev + lax.dot_general(kw.astype(BF16), v, tn, preferred_element_type=F32)
        n_scr[h] = decay * n_prev + jnp.sum(kw, axis=0, keepdims=True)
        m_scr[h] = jnp.broadcast_to(m_new, (1, LANES))

        if reverse:
            hs = hout + hf_ref[:, h * dv:(h + 1) * dv]
            ms = jnp.mean(hs * hs, axis=-1, keepdims=True)
            y = hs * lax.rsqrt(ms + EPS) * ng_ref[:, h * dv:(h + 1) * dv]
            y = y * _sigmoid(mo_ref[:, h * dv:(h + 1) * dv].astype(F32))
            y_ref[:, h * dv:(h + 1) * dv] = y.astype(BF16)
        else:
            h_ref[:, h * dv:(h + 1) * dv] = hout


def _mlstm(qk, z, gates, gates_t, gate_b, gate_b_t, norm_g, *, layer, v_blk, mo_blk):
    S = qk.shape[0]
    H = M_HEADS
    dk = qk.shape[1] // (2 * H)
    W = norm_g.shape[2]
    dv = W // H
    L = min(256, S)
    nc = S // L
    scratch = [pltpu.VMEM((H, dk, dv), F32), pltpu.VMEM((H, 1, dk), F32), pltpu.VMEM((H, 1, LANES), F32)]

    def specs(cmap):
        return [pl.BlockSpec((L, 2 * H * dk), lambda c: (cmap(c), 0)),
                pl.BlockSpec((L, W), lambda c: (cmap(c), v_blk)),
                pl.BlockSpec((L, LANES), lambda c: (cmap(c), 0)),
                pl.BlockSpec((4 * H, L), lambda c: (0, cmap(c))),
                _layer_vec(layer, LANES),
                pl.BlockSpec((None, 4 * H, 1), lambda c: (layer, 0, 0))]

    fwd = lambda c: c
    h_fwd = pl.pallas_call(
        functools.partial(_mlstm_kernel, reverse=False, dk=dk, dv=dv, L=L),
        grid=(nc,),
        in_specs=specs(fwd),
        out_specs=pl.BlockSpec((L, W), lambda c: (c, 0)),
        out_shape=jax.ShapeDtypeStruct((S, W), F32),
        scratch_shapes=scratch,
        compiler_params=_params("arbitrary"),
        name="mlstm_fwd",
    )(qk, z, gates, gates_t, gate_b, gate_b_t)

    bwd = lambda c: nc - 1 - c
    return pl.pallas_call(
        functools.partial(_mlstm_kernel, reverse=True, dk=dk, dv=dv, L=L),
        grid=(nc,),
        in_specs=specs(bwd) + [pl.BlockSpec((L, W), lambda c: (bwd(c), 0)),
                               pl.BlockSpec((L, W), lambda c: (bwd(c), mo_blk)),
                               _layer_vec(layer, W)],
        out_specs=pl.BlockSpec((L, W), lambda c: (bwd(c), 0)),
        out_shape=jax.ShapeDtypeStruct((S, W), BF16),
        scratch_shapes=scratch,
        compiler_params=_params("arbitrary"),
        name="mlstm_bwd",
    )(qk, z, gates, gates_t, gate_b, gate_b_t, h_fwd, z, norm_g)


def _qkprep_kernel(q_ref, k_ref, v_ref, cos_ref, sin_ref, qg_ref, kg_ref, qt_ref, kn_ref, vt_ref,
                   *, hd, q_scale):
    ts, W = q_ref.shape
    lane = lax.broadcasted_iota(jnp.int32, (ts, LANES), 1)
    lo = lane < hd
    first_half = (lane % hd) < (hd // 8)
    row_lo = lax.broadcasted_iota(jnp.int32, (LANES, ts), 0) < hd
    cos = cos_ref[...]
    sin = sin_ref[...]

    def norm_rope(x, g):
        sq = x * x
        s_lo = jnp.sum(jnp.where(lo, sq, 0.0), axis=1, keepdims=True)
        s_hi = jnp.sum(jnp.where(lo, 0.0, sq), axis=1, keepdims=True)
        inv = lax.rsqrt(jnp.where(lo, s_lo, s_hi) * (1.0 / hd) + EPS)
        xn = x * inv * g
        up = pltpu.roll(xn, LANES - hd // 8, axis=1)
        dn = pltpu.roll(xn, hd // 8, axis=1)
        return xn * cos + jnp.where(first_half, up, dn) * sin

    for j in range(W // LANES):
        sl = slice(j * LANES, (j + 1) * LANES)
        qn = norm_rope(q_ref[:, sl].astype(F32), qg_ref[:, sl]) * q_scale
        qn_t = qn.T
        qt_ref[0, sl, :] = jnp.where(row_lo, qn_t, 0.0).astype(BF16)
        qt_ref[1, sl, :] = jnp.where(row_lo, 0.0, qn_t).astype(BF16)
        kn_ref[:, sl] = norm_rope(k_ref[:, sl].astype(F32), kg_ref[:, sl]).astype(BF16)
        r0 = j * (LANES + BF16_SUBLANES)
        vt_ref[r0:r0 + LANES, :] = v_ref[:, sl].astype(F32).T.astype(BF16)
        vt_ref[r0 + LANES:r0 + LANES + BF16_SUBLANES, :] = jnp.where(
            lax.broadcasted_iota(jnp.int32, (BF16_SUBLANES, ts), 0) == 0, 1.0, 0.0).astype(BF16)


def _qk_prep(z, cos_t, sin_t, qg, kg, *, layer, q_blk, k_blk, v_blk, hd, q_scale):
    S = z.shape[0]
    W = qg.shape[2]
    ts = min(512, S)
    n_slab = W // LANES
    return pl.pallas_call(
        functools.partial(_qkprep_kernel, hd=hd, q_scale=q_scale),
        grid=(S // ts,),
        in_specs=[pl.BlockSpec((ts, W), lambda i: (i, q_blk)),
                  pl.BlockSpec((ts, W), lambda i: (i, k_blk)),
                  pl.BlockSpec((ts, W), lambda i: (i, v_blk)),
                  pl.BlockSpec((ts, LANES), lambda i: (i, 0)),
                  pl.BlockSpec((ts, LANES), lambda i: (i, 0)),
                  _layer_vec(layer, W), _layer_vec(layer, W)],
        out_specs=[pl.BlockSpec((2, W, ts), lambda i: (0, 0, i)),
                   pl.BlockSpec((ts, W), lambda i: (i, 0)),
                   pl.BlockSpec((W + n_slab * BF16_SUBLANES, ts), lambda i: (0, i))],
        out_shape=[jax.ShapeDtypeStruct((2, W, S), BF16), jax.ShapeDtypeStruct((S, W), BF16),
                   jax.ShapeDtypeStruct((W + n_slab * BF16_SUBLANES, S), BF16)],
        compiler_params=_params("arbitrary"),
        name="attn_qk_prep",
    )(z, z, z, cos_t, sin_t, qg, kg)


ATTN_COL_BLOCK = 256
ATTN_KEY_BLOCK = 1024
ATTN_COL_GROUP = 8


def _flash_kernel(qa_ref, qb_ref, k_ref, vt_ref, lam_ref, sg_ref, o_ref, q_scr, m_scr, acc_scr,
                  *, tq, tk, lam_init):
    kj = pl.program_id(2)
    cb_w, kb_w = ATTN_COL_BLOCK, min(ATTN_KEY_BLOCK, tk)
    dv = o_ref.shape[1]

    @pl.when(kj == 0)
    def _():
        q_scr[:, 0:tq] = qa_ref[0]
        q_scr[:, tq:2 * tq] = qb_ref[0]
        m_scr[...] = jnp.full_like(m_scr, NEG)
        acc_scr[...] = jnp.zeros_like(acc_scr)

    group = min(ATTN_COL_GROUP, 2 * tq // cb_w)
    steps = [(c, kb) for kb in range(tk // kb_w) for c in range(group)]

    def col_group(gi, carry):
        cols = [pl.ds(pl.multiple_of((gi * group + c) * cb_w, cb_w), cb_w) for c in range(group)]
        qts = [q_scr[:, cs] for cs in cols]
        state = [(m_scr[:, cs], acc_scr[:, cs]) for cs in cols]

        def scores(c, kb):
            return jnp.dot(k_ref[kb * kb_w:(kb + 1) * kb_w, :], qts[c], preferred_element_type=F32)

        s_next = scores(*steps[0])
        for n, (c, kb) in enumerate(steps):
            s = s_next
            if n + 1 < len(steps):
                s_next = scores(*steps[n + 1])
            m, acc = state[c]
            m_new = jnp.maximum(m, jnp.max(s, axis=0, keepdims=True))
            alpha = jnp.exp2(m - m_new)
            p = jnp.exp2((s - m_new).astype(BF16))
            acc = alpha * acc + jnp.dot(vt_ref[:, kb * kb_w:(kb + 1) * kb_w], p,
                                        preferred_element_type=F32)
            state[c] = (m_new, acc)
        for cs, (m, acc) in zip(cols, state):
            m_scr[:, cs], acc_scr[:, cs] = m, acc
        return carry

    lax.fori_loop(0, 2 * tq // (cb_w * group), col_group, 0)

    @pl.when(kj == pl.num_programs(2) - 1)
    def _():
        lam = lam_ref[...]
        lam_full = (jnp.exp(jnp.sum(lam[0:1] * lam[1:2], axis=1, keepdims=True))
                    - jnp.exp(jnp.sum(lam[2:3] * lam[3:4], axis=1, keepdims=True)) + lam_init)
        a = acc_scr[0:dv, :] / acc_scr[dv:dv + 1, :]
        o_t = a[:, 0:tq] - lam_full * a[:, tq:2 * tq]
        ms = jnp.mean(o_t * o_t, axis=0, keepdims=True)
        y_t = o_t * lax.rsqrt(ms + EPS) * sg_ref[...] * (1.0 - lam_init)
        o_ref[...] = y_t.T.astype(BF16)


def _flash_diff_attention(q_t, kn, v_t, lam, subln_g, *, layer, lam_init):
    _, W, S = q_t.shape
    H = A_HEADS
    dv = W // H
    tq = min(1024, S)
    tk = min(1024, S)
    return pl.pallas_call(
        functools.partial(_flash_kernel, tq=tq, tk=tk, lam_init=lam_init),
        grid=(H, S // tq, S // tk),
        in_specs=[pl.BlockSpec((1, dv, tq), lambda h, i, j: (0, h, i)),
                  pl.BlockSpec((1, dv, tq), lambda h, i, j: (1, h, i)),
                  pl.BlockSpec((tk, dv), lambda h, i, j: (j, h)),
                  pl.BlockSpec((dv + BF16_SUBLANES, tk), lambda h, i, j: (h, j)),
                  pl.BlockSpec((None,) + lam.shape[1:], lambda h, i, j: (layer, 0, 0)),
                  pl.BlockSpec((None, dv, 1), lambda h, i, j: (layer, 0, 0))],
        out_specs=pl.BlockSpec((tq, dv), lambda h, i, j: (i, h)),
        out_shape=jax.ShapeDtypeStruct((S, W), BF16),
        scratch_shapes=[pltpu.VMEM((dv, 2 * tq), BF16), pltpu.VMEM((1, 2 * tq), F32),
                        pltpu.VMEM((dv + BF16_SUBLANES, 2 * tq), F32)],
        compiler_params=_params("arbitrary", "arbitrary", "arbitrary"),
        name="flash_diff_attn",
    )(q_t, q_t, kn, v_t, lam, subln_g)


def _merge_kernel(ym_ref, ya_ref, wm_ref, wa_ref, gm_ref, ga_ref, o_ref):
    a = jnp.dot(ym_ref[...], wm_ref[...], preferred_element_type=F32)
    b = jnp.dot(ya_ref[...], wa_ref[...], preferred_element_type=F32)
    o_ref[...] = (_sigmoid(gm_ref[...].astype(F32)) * a + _sigmoid(ga_ref[...].astype(F32)) * b).astype(BF16)


def _merge(y_m, y_a, w_m, w_a, z, *, layer, gm_col, ga_col):
    S, W = y_m.shape
    D = w_m.shape[2]
    tm, tn = min(512, S), 1024
    return pl.pallas_call(
        _merge_kernel,
        grid=(S // tm, D // tn),
        in_specs=[pl.BlockSpec((tm, W), lambda i, j: (i, 0)),
                  pl.BlockSpec((tm, W), lambda i, j: (i, 0)),
                  pl.BlockSpec((None, W, tn), lambda i, j: (layer, 0, j)),
                  pl.BlockSpec((None, W, tn), lambda i, j: (layer, 0, j)),
                  pl.BlockSpec((tm, tn), lambda i, j: (i, gm_col // tn + j)),
                  pl.BlockSpec((tm, tn), lambda i, j: (i, ga_col // tn + j))],
        out_specs=pl.BlockSpec((tm, tn), lambda i, j: (i, j)),
        out_shape=jax.ShapeDtypeStruct((S, D), BF16),
        compiler_params=_params("arbitrary", "arbitrary"),
        name="branch_merge",
    )(y_m, y_a, w_m, w_a, z, z)


def _route(logits):
    G, E = N_GROUPS, EXPERTS_PER_GROUP
    lane = lax.broadcasted_iota(jnp.int32, logits.shape, 1)
    lanef = lane.astype(F32)
    is_g = lane < G
    gl = jnp.where(is_g, logits, NEG)
    ge = jnp.where(is_g, jnp.exp(gl - jnp.max(gl, axis=1, keepdims=True)), 0.0)
    probs = ge / jnp.sum(ge, axis=1, keepdims=True)
    g_val = jnp.max(probs, axis=1, keepdims=True)
    g_idx = jnp.min(jnp.where(is_g & (probs == g_val), lanef, float(LANES)), axis=1, keepdims=True)
    lo = G + E * g_idx
    in_grp = (lanef >= lo) & (lanef < lo + E)
    el = jnp.where(in_grp, logits, NEG)
    v1 = jnp.max(el, axis=1, keepdims=True)
    i1 = jnp.min(jnp.where(in_grp & (el == v1), lanef, float(LANES)), axis=1, keepdims=True)
    rest = in_grp & (lanef != i1)
    el2 = jnp.where(rest, logits, NEG)
    v2 = jnp.max(el2, axis=1, keepdims=True)
    i2 = jnp.min(jnp.where(rest & (el2 == v2), lanef, float(LANES)), axis=1, keepdims=True)
    t = jnp.exp(v2 - v1)
    w1 = (1.0 / (1.0 + t)) * g_val
    w2 = (t / (1.0 + t)) * g_val
    return jnp.where(lane == 0, i1 - G, jnp.where(lane == 1, i2 - G,
                     jnp.where(lane == 2, w1, jnp.where(lane == 3, w2, 0.0))))


def _out_kernel(mg_ref, w_ref, x_ref, g1_ref, n2_ref, sc_ref, sh_ref, wr_ref, br_ref,
                x1_ref, h2_ref, r_ref, cnt_ref, cnt_scr):
    tm = x_ref.shape[0]

    @pl.when(pl.program_id(0) == 0)
    def _():
        cnt_scr[...] = jnp.zeros_like(cnt_scr)

    y = jnp.dot(mg_ref[...], w_ref[...], preferred_element_type=F32)
    x1 = x_ref[...] + g1_ref[...] * y
    x1_ref[...] = x1
    ms = jnp.mean(x1 * x1, axis=-1, keepdims=True)
    h2 = x1 * lax.rsqrt(ms + EPS) * n2_ref[...] * (1.0 + sc_ref[...]) + sh_ref[...]
    h2_ref[...] = h2
    h_hi = h2.astype(BF16)
    h_lo = (h2 - h_hi.astype(F32)).astype(BF16)
    t = jnp.dot(h_hi, wr_ref[...], preferred_element_type=F32)
    logits = (t[:, :LANES] + t[:, LANES:]
              + jnp.dot(h_lo, wr_ref[:, :LANES], preferred_element_type=F32) + br_ref[...])
    route = _route(logits)

    lane = lax.broadcasted_iota(jnp.int32, (tm, LANES), 1)
    lanef = lane.astype(F32)
    sel1 = lanef == route[:, 0:1]
    sel2 = lanef == route[:, 1:2]
    onehot = jnp.where(sel1 | sel2, 1.0, 0.0)
    earlier = (lax.broadcasted_iota(jnp.int32, (tm, tm), 0)
               > lax.broadcasted_iota(jnp.int32, (tm, tm), 1))
    before = cnt_scr[...] + jnp.dot(jnp.where(earlier, 1.0, 0.0).astype(BF16), onehot.astype(BF16),
                                    preferred_element_type=F32)
    rank1 = jnp.sum(jnp.where(sel1, before, 0.0), axis=1, keepdims=True)
    rank2 = jnp.sum(jnp.where(sel2, before, 0.0), axis=1, keepdims=True)
    r_ref[...] = jnp.where(lane == 4, rank1, jnp.where(lane == 5, rank2, route))
    cnt = cnt_scr[...] + jnp.sum(onehot, axis=0, keepdims=True)
    cnt_scr[...] = cnt
    cnt_ref[...] = jnp.broadcast_to(cnt, cnt_ref.shape)


def _out_projection(merged, w_out, x, mod, norm2_g, w_router, b_router, *, layer):
    S, D = x.shape
    tm = min(256, S)
    row = pl.BlockSpec((tm, D), lambda i: (i, 0))
    return pl.pallas_call(
        _out_kernel,
        grid=(S // tm,),
        in_specs=[row, pl.BlockSpec((None, D, D), lambda i: (layer, 0, 0)), row,
                  _layer_vec(layer, D, 2), _layer_vec(layer, D), _layer_vec(layer, D, 4),
                  _layer_vec(layer, D, 3),
                  pl.BlockSpec((None, D, 2 * LANES), lambda i: (layer, 0, 0)),
                  _layer_vec(layer, LANES)],
        out_specs=[row, row, pl.BlockSpec((tm, LANES), lambda i: (i, 0)),
                   pl.BlockSpec((SUBLANES, LANES), lambda i: (0, 0))],
        out_shape=[jax.ShapeDtypeStruct((S, D), F32), jax.ShapeDtypeStruct((S, D), F32),
                   jax.ShapeDtypeStruct((S, LANES), F32), jax.ShapeDtypeStruct((SUBLANES, LANES), F32)],
        scratch_shapes=[pltpu.VMEM((1, LANES), F32)],
        compiler_params=_params("arbitrary"),
        name="out_proj_router",
    )(merged, w_out, x, mod, norm2_g, mod, mod, w_router, b_router)


def _dispatch_plan(route, counts_f, n_experts, tm):
    T = route.shape[0]
    counts = counts_f[0, :n_experts].astype(jnp.int32)
    padded = ((counts + tm - 1) // tm) * tm
    pad_end = jnp.cumsum(padded)
    pad_start = pad_end - padded
    n_tiles = (2 * T) // tm + n_experts
    n_active = (pad_end[-1] // tm).astype(jnp.int32)
    tile_start = jnp.arange(n_tiles, dtype=jnp.int32) * tm
    tile_expert = jnp.sum(pad_end[None, :] <= jnp.minimum(tile_start, pad_end[-1] - 1)[:, None],
                          axis=1, dtype=jnp.int32)
    ri = route[:, :6].astype(jnp.int32)
    pos = jnp.concatenate([pad_start[ri[:, 0]] + ri[:, 4], pad_start[ri[:, 1]] + ri[:, 5]])
    tail = ((pad_start + counts) // SUBLANES) * SUBLANES
    fill = jnp.concatenate([tail, (pad_end - tail) // SUBLANES, n_active.reshape(1)])
    return tile_expert, n_active.reshape(1), pos, fill


def _dispatch_kernel(pos_ref, fill_ref, h_ref, xs_hbm, zero_scr, sem, *, tm, T, n_experts, n_tiles):
    i = pl.program_id(0)

    @pl.when(i == 0)
    def _():
        zero_scr[...] = jnp.zeros_like(zero_scr)
        zero8 = zero_scr.at[pl.ds(0, SUBLANES), :]
        n_active = fill_ref[2 * n_experts]

        def per_expert(e, total):
            def piece(k, carry):
                row = pl.multiple_of(fill_ref[e] + k * SUBLANES, SUBLANES)
                pltpu.make_async_copy(zero8, xs_hbm.at[pl.ds(row, SUBLANES), :], sem.at[1]).start()
                return carry
            n = fill_ref[n_experts + e]
            lax.fori_loop(0, n, piece, 0)
            return total + n

        pieces = lax.fori_loop(0, n_experts, per_expert, 0)

        def unused_tile(t, carry):
            row = pl.multiple_of(t * tm, tm)
            pltpu.make_async_copy(zero_scr, xs_hbm.at[pl.ds(row, tm), :], sem.at[2]).start()
            return carry

        lax.fori_loop(n_active, n_tiles, unused_tile, 0)

        def wait_piece(k, carry):
            pltpu.make_async_copy(zero8, xs_hbm.at[pl.ds(0, SUBLANES), :], sem.at[1]).wait()
            return carry

        lax.fori_loop(0, pieces, wait_piece, 0)

        def wait_tile(t, carry):
            pltpu.make_async_copy(zero_scr, xs_hbm.at[pl.ds(0, tm), :], sem.at[2]).wait()
            return carry

        lax.fori_loop(n_active, n_tiles, wait_tile, 0)

    def body(r, carry):
        for slot in range(2):
            p = pos_ref[slot * T + i * tm + r]
            pltpu.make_async_copy(h_ref.at[pl.ds(r, 1), :], xs_hbm.at[pl.ds(p, 1), :], sem.at[0]).start()
        return carry

    lax.fori_loop(0, tm, body, 0, unroll=8)
    for slot in range(2):
        pltpu.make_async_copy(h_ref, xs_hbm.at[pl.ds(0, tm), :], sem.at[0]).wait()


def _dispatch(h2, pos, fill, *, n_experts, tm):
    T, D = h2.shape
    n_tiles = (2 * T) // tm + n_experts
    grid_spec = pltpu.PrefetchScalarGridSpec(
        num_scalar_prefetch=2,
        grid=(T // tm,),
        in_specs=[pl.BlockSpec((tm, D), lambda i, pos, fill: (i, 0))],
        out_specs=pl.BlockSpec(memory_space=pl.ANY),
        scratch_shapes=[pltpu.VMEM((tm, D), F32), pltpu.SemaphoreType.DMA((3,))],
    )
    return pl.pallas_call(
        functools.partial(_dispatch_kernel, tm=tm, T=T, n_experts=n_experts, n_tiles=n_tiles),
        grid_spec=grid_spec,
        out_shape=jax.ShapeDtypeStruct((n_tiles * tm, D), F32),
        compiler_params=_params("arbitrary"),
        name="moe_dispatch",
    )(pos, fill, h2)


def _expert_kernel(te_ref, na_ref, x_ref, wg_ref, wu_ref, wd_ref, o_ref, wg_b, wu_b, wd_b):
    i = pl.program_id(0)

    @pl.when(i >= na_ref[0])
    def _():
        o_ref[...] = jnp.zeros_like(o_ref)

    @pl.when(i < na_ref[0])
    def _():
        @pl.when((i == 0) | (te_ref[i] != te_ref[jnp.maximum(i - 1, 0)]))
        def _():
            wg_b[...] = wg_ref[...].astype(BF16)
            wu_b[...] = wu_ref[...].astype(BF16)
            wd_b[...] = wd_ref[...].astype(BF16)

        x = x_ref[...].astype(BF16)
        g = jnp.dot(x, wg_b[...], preferred_element_type=F32)
        u = jnp.dot(x, wu_b[...], preferred_element_type=F32)
        a = (g * _sigmoid(g) * u).astype(BF16)
        o_ref[...] = jnp.dot(a, wd_b[...], preferred_element_type=F32)


def _experts(xs, tile_expert, n_active, w_gate, w_up, w_down, *, layer, tm):
    D = xs.shape[1]
    Fe = w_gate.shape[-1]
    n_tiles = tile_expert.shape[0]
    tile = lambda i, te, na: (jnp.minimum(i, na[0] - 1), 0)
    grid_spec = pltpu.PrefetchScalarGridSpec(
        num_scalar_prefetch=2,
        grid=(n_tiles,),
        in_specs=[pl.BlockSpec((tm, D), tile),
                  pl.BlockSpec((None, None, D, Fe), lambda i, te, na: (layer, te[i], 0, 0)),
                  pl.BlockSpec((None, None, D, Fe), lambda i, te, na: (layer, te[i], 0, 0)),
                  pl.BlockSpec((None, None, Fe, D), lambda i, te, na: (layer, te[i], 0, 0))],
        out_specs=pl.BlockSpec((tm, D), lambda i, te, na: (i, 0)),
        scratch_shapes=[pltpu.VMEM((D, Fe), BF16), pltpu.VMEM((D, Fe), BF16), pltpu.VMEM((Fe, D), BF16)],
    )
    return pl.pallas_call(
        _expert_kernel,
        grid_spec=grid_spec,
        out_shape=jax.ShapeDtypeStruct((n_tiles * tm, D), F32),
        compiler_params=_params("arbitrary"),
        name="routed_experts",
    )(tile_expert, n_active, xs, w_gate, w_up, w_down)


def _combine_kernel(pos_ref, ys_hbm, x_ref, g2_ref, r_ref, o_ref, ybuf, sem, *, tm, T):
    i = pl.program_id(0)

    def issue(blk, slot):
        def body(r, carry):
            for k in range(2):
                p = pos_ref[k * T + blk * tm + r]
                pltpu.make_async_copy(ys_hbm.at[pl.ds(p, 1), :], ybuf.at[slot, k, pl.ds(r, 1), :],
                                      sem.at[slot]).start()
            return carry
        lax.fori_loop(0, tm, body, 0, unroll=8)

    @pl.when(i == 0)
    def _():
        issue(0, 0)

    @pl.when(i + 1 < pl.num_programs(0))
    def _():
        issue(i + 1, (i + 1) % 2)

    slot = i % 2
    for k in range(2):
        pltpu.make_async_copy(ys_hbm.at[pl.ds(0, tm), :], ybuf.at[slot, k], sem.at[slot]).wait()
    y = r_ref[:, 2:3] * ybuf[slot, 0] + r_ref[:, 3:4] * ybuf[slot, 1]
    o_ref[...] = x_ref[...] + g2_ref[...] * y


def _combine(y_sorted, pos, x1, mod, route, *, layer, tm):
    T, D = x1.shape
    grid_spec = pltpu.PrefetchScalarGridSpec(
        num_scalar_prefetch=1,
        grid=(T // tm,),
        in_specs=[pl.BlockSpec(memory_space=pl.ANY),
                  pl.BlockSpec((tm, D), lambda i, pos: (i, 0)),
                  _layer_vec(layer, D, 5),
                  pl.BlockSpec((tm, LANES), lambda i, pos: (i, 0))],
        out_specs=pl.BlockSpec((tm, D), lambda i, pos: (i, 0)),
        scratch_shapes=[pltpu.VMEM((2, 2, tm, D), F32), pltpu.SemaphoreType.DMA((2,))],
    )
    return pl.pallas_call(
        functools.partial(_combine_kernel, tm=tm, T=T),
        grid_spec=grid_spec,
        out_shape=jax.ShapeDtypeStruct((T, D), F32),
        compiler_params=_params("arbitrary"),
        name="moe_combine",
    )(pos, y_sorted, x1, mod, route)


def _rope_tables(S, hd):
    rope_dim = hd // 4
    posn = jnp.arange(S, dtype=F32)
    inv = ROPE_THETA ** (-jnp.arange(0, rope_dim, 2, dtype=F32) / rope_dim)
    ang = posn[:, None] * inv[None, :]
    cos, sin = jnp.cos(ang), jnp.sin(ang)
    rest = hd - rope_dim
    cos_t = jnp.concatenate([cos, cos, jnp.ones((S, rest), F32)], axis=1)
    sin_t = jnp.concatenate([-sin, sin, jnp.zeros((S, rest), F32)], axis=1)
    reps = LANES // hd
    return jnp.tile(cos_t, (1, reps)), jnp.tile(sin_t, (1, reps))


def kernel(x, c, ada_w, ada_b, norm1_g, norm2_g, w_in, m_conv_w, m_gate_b, m_norm_g, a_qnorm_g,
           a_knorm_g, a_lambda, a_subln_g, w_branch_m, w_branch_a, w_out, rg_w, rg_b, re_w, re_b,
           e_w_gate, e_w_up, e_w_down):
    B, S, D = x.shape
    assert B == 1
    depth = ada_w.shape[0]
    Wm = w_branch_m.shape[1]
    Wa = w_branch_a.shape[1]
    qk_w = m_conv_w.shape[2]
    n_gate = 4 * M_HEADS
    hd = a_qnorm_g.shape[1]
    n_experts = re_w.shape[2]
    n_route = N_GROUPS + n_experts
    assert Wa // A_HEADS == LANES and 2 * hd == LANES
    c_mv, c_mo, c_aq = qk_w, qk_w + Wm, qk_w + 2 * Wm
    c_ak, c_av, c_gm = c_aq + Wa, c_aq + 2 * Wa, c_aq + 3 * Wa
    c_ga = c_gm + D
    g0 = qk_w + Wm

    w_main = jnp.concatenate([w_in[:, :, :g0], w_in[:, :, g0 + n_gate:]], axis=2).astype(BF16)
    w_gate = jnp.pad(w_in[:, :, g0:g0 + n_gate], ((0, 0), (0, 0), (0, LANES - n_gate))).astype(BF16)
    w_bm, w_ba, w_o = w_branch_m.astype(BF16), w_branch_a.astype(BF16), w_out.astype(BF16)
    w_r = jnp.pad(jnp.concatenate([rg_w, re_w], axis=2), ((0, 0), (0, 0), (0, LANES - n_route)))
    w_r_hi = w_r.astype(BF16)
    w_router = jnp.concatenate([w_r_hi, (w_r - w_r_hi.astype(F32)).astype(BF16)], axis=2)
    b_router = jnp.pad(jnp.concatenate([rg_b, re_b], axis=1), ((0, 0), (0, LANES - n_route)))[:, None]
    gate_b = jnp.pad(m_gate_b.reshape(depth, 1, n_gate), ((0, 0), (0, 0), (0, LANES - n_gate)))
    gate_b_t = m_gate_b.reshape(depth, n_gate, 1)
    qg = jnp.tile(a_qnorm_g, (1, Wa // hd))[:, None]
    kg = jnp.tile(a_knorm_g, (1, Wa // hd))[:, None]
    n1, n2, mng = norm1_g[:, None], norm2_g[:, None], m_norm_g[:, None]
    subln = a_subln_g[:, :, None]

    xs = x[0]
    mod = _modulation(c, ada_w, ada_b)
    cos_t, sin_t = _rope_tables(S, hd)
    tm_moe = min(MOE_TILE, S)

    for l in range(depth):
        lam_init = 0.8 - 0.6 * math.exp(-0.3 * l)
        z, gates, gates_t = _in_projection(xs, n1, mod, w_main, w_gate, layer=l)
        qk = _qk_conv(z, m_conv_w, float((qk_w // (2 * M_HEADS)) ** -0.5), layer=l)
        y_m = _mlstm(qk, z, gates, gates_t, gate_b, gate_b_t, mng, layer=l,
                     v_blk=c_mv // Wm, mo_blk=c_mo // Wm)
        q_t, kn, v_t = _qk_prep(z, cos_t, sin_t, qg, kg, layer=l, q_blk=c_aq // Wa, k_blk=c_ak // Wa,
                                v_blk=c_av // Wa, hd=hd, q_scale=float(hd ** -0.5 * math.log2(math.e)))
        y_a = _flash_diff_attention(q_t, kn, v_t, a_lambda, subln, layer=l, lam_init=lam_init)
        merged = _merge(y_m, y_a, w_bm, w_ba, z, layer=l, gm_col=c_gm, ga_col=c_ga)
        x1, h2, route, counts = _out_projection(merged, w_o, xs, mod, n2, w_router, b_router, layer=l)
        tile_expert, n_active, pos, fill = _dispatch_plan(route, counts, n_experts, tm_moe)
        x_sorted = _dispatch(h2, pos, fill, n_experts=n_experts, tm=tm_moe)
        y_sorted = _experts(x_sorted, tile_expert, n_active, e_w_gate, e_w_up, e_w_down,
                            layer=l, tm=tm_moe)
        xs = _combine(y_sorted, pos, x1, mod, route, layer=l, tm=tm_moe)
    return xs[None]
```

```python
import functools
import math

import jax
import jax.numpy as jnp
from jax import lax
from jax.experimental import pallas as pl
from jax.experimental.pallas import tpu as pltpu

F32 = jnp.float32
BF16 = jnp.bfloat16

M_HEADS = 4
A_HEADS = 8
CONV_WIDTH = 5
ROPE_THETA = 500000.0
N_GROUPS = 4
EXPERTS_PER_GROUP = 8
EPS = 1e-6
NEG = -1e30
LANES = 128
SUBLANES = 8
BF16_SUBLANES = 16
VMEM_LIMIT = 56 * 1024 * 1024
MOE_TILE = 256
RANK_RADIX = 65536


def _sigmoid(x):
    return 1.0 / (1.0 + jnp.exp(-x))


def _log_sigmoid(x):
    return jnp.minimum(x, 0.0) - jnp.log(1.0 + jnp.exp(-jnp.abs(x)))


def _params(*sem):
    return pltpu.CompilerParams(dimension_semantics=sem, vmem_limit_bytes=VMEM_LIMIT)


def _layer_vec(layer, width, blk=0):
    return pl.BlockSpec((None, 1, width), lambda *_: (layer, 0, blk))


def _mod_kernel(c_ref, w_ref, b_ref, o_ref):
    c = c_ref[...]
    ca = c * _sigmoid(c)
    o_ref[...] = jnp.sum(ca * w_ref[...], axis=0, keepdims=True) + b_ref[...]


def _modulation(c, ada_w, ada_b):
    L, D, N = ada_w.shape
    tn = min(1024, N)
    return pl.pallas_call(
        _mod_kernel,
        grid=(L, N // tn),
        in_specs=[pl.BlockSpec((D, 1), lambda l, j: (0, 0)),
                  pl.BlockSpec((None, D, tn), lambda l, j: (l, 0, j)),
                  pl.BlockSpec((None, 1, tn), lambda l, j: (l, 0, j))],
        out_specs=pl.BlockSpec((None, 1, tn), lambda l, j: (l, 0, j)),
        out_shape=jax.ShapeDtypeStruct((L, 1, N), F32),
        compiler_params=_params("arbitrary", "arbitrary"),
        name="adaln_mod",
    )(c.reshape(D, 1), ada_w, ada_b.reshape(L, 1, N))


def _inproj_kernel(x_ref, g_ref, sc_ref, sh_ref, w_ref, wg_ref, z_ref, gate_ref, gate_t_ref, h_scr):
    @pl.when(pl.program_id(1) == 0)
    def _():
        x = x_ref[...]
        ms = jnp.mean(x * x, axis=-1, keepdims=True)
        h = x * lax.rsqrt(ms + EPS) * g_ref[...] * (1.0 + sc_ref[...]) + sh_ref[...]
        hb = h.astype(BF16)
        h_scr[...] = hb
        g = jnp.dot(hb, wg_ref[...], preferred_element_type=F32)
        gate_ref[...] = g
        gate_t_ref[...] = g.T

    z_ref[...] = jnp.dot(h_scr[...], w_ref[...], preferred_element_type=F32).astype(BF16)


def _in_projection(x, norm_g, mod, w_main, w_gate, *, layer):
    S, D = x.shape
    N = w_main.shape[2]
    tm, tn = min(1024, S), 1024
    return pl.pallas_call(
        _inproj_kernel,
        grid=(S // tm, N // tn),
        in_specs=[pl.BlockSpec((tm, D), lambda i, j: (i, 0)),
                  _layer_vec(layer, D), _layer_vec(layer, D, 1), _layer_vec(layer, D, 0),
                  pl.BlockSpec((None, D, tn), lambda i, j: (layer, 0, j)),
                  pl.BlockSpec((None, D, LANES), lambda i, j: (layer, 0, 0))],
        out_specs=[pl.BlockSpec((tm, tn), lambda i, j: (i, j)),
                   pl.BlockSpec((tm, LANES), lambda i, j: (i, 0)),
                   pl.BlockSpec((LANES, tm), lambda i, j: (0, i))],
        out_shape=[jax.ShapeDtypeStruct((S, N), BF16), jax.ShapeDtypeStruct((S, LANES), F32),
                   jax.ShapeDtypeStruct((LANES, S), F32)],
        scratch_shapes=[pltpu.VMEM((tm, D), BF16)],
        compiler_params=_params("arbitrary", "arbitrary"),
        name="norm_inproj",
    )(x, norm_g, mod, mod, w_main, w_gate)


def _conv_kernel(prev_ref, cur_ref, next_ref, w_ref, o_ref, ext_scr, *, ts, q_scale):
    i = pl.program_id(0)
    halo = BF16_SUBLANES
    pad = CONV_WIDTH // 2
    ext_scr[0:halo, :] = jnp.where(i > 0, prev_ref[...].astype(F32), 0.0)
    ext_scr[halo:halo + ts, :] = cur_ref[...].astype(F32)
    ext_scr[halo + ts:2 * halo + ts, :] = jnp.where(i < pl.num_programs(0) - 1,
                                                    next_ref[...].astype(F32), 0.0)
    acc = w_ref[0:1, :] * ext_scr[pl.ds(halo - pad, ts), :]
    for k in range(1, CONV_WIDTH):
        acc = acc + w_ref[k:k + 1, :] * ext_scr[pl.ds(halo - pad + k, ts), :]
    y = acc * _sigmoid(acc)
    C = y.shape[1]
    col = lax.broadcasted_iota(jnp.int32, (1, C), 1)
    o_ref[...] = (y * jnp.where(col < C // 2, q_scale, 1.0)).astype(BF16)


def _qk_conv(z, conv_w, q_scale, *, layer):
    S = z.shape[0]
    C = conv_w.shape[2]
    ts = min(512, S)
    halo = BF16_SUBLANES
    r = ts // halo
    nblk = S // halo
    return pl.pallas_call(
        functools.partial(_conv_kernel, ts=ts, q_scale=q_scale),
        grid=(S // ts,),
        in_specs=[pl.BlockSpec((halo, C), lambda i: (jnp.maximum(i * r - 1, 0), 0)),
                  pl.BlockSpec((ts, C), lambda i: (i, 0)),
                  pl.BlockSpec((halo, C), lambda i: (jnp.minimum((i + 1) * r, nblk - 1), 0)),
                  pl.BlockSpec((None, CONV_WIDTH, C), lambda i: (layer, 0, 0))],
        out_specs=pl.BlockSpec((ts, C), lambda i: (i, 0)),
        out_shape=jax.ShapeDtypeStruct((S, C), BF16),
        scratch_shapes=[pltpu.VMEM((ts + 2 * halo, C), F32)],
        compiler_params=_params("arbitrary"),
        name="qk_conv",
    )(z, z, z, conv_w)


def _mlstm_kernel(qk_f, v_f, g_f, gt_f, qk_b, v_b, g_b, gt_b, gb_ref, gbt_ref, hf_ref, hb_ref,
                  C_scr, n_scr, m_scr, *, dk, dv, L):
    @pl.when(pl.program_id(0) == 0)
    def _():
        C_scr[...] = jnp.zeros_like(C_scr)
        n_scr[...] = jnp.zeros_like(n_scr)
        m_scr[...] = jnp.zeros_like(m_scr)

    row = lax.broadcasted_iota(jnp.int32, (L, L), 0)
    col = lax.broadcasted_iota(jnp.int32, (L, L), 1)
    for reverse, refs in ((False, (qk_f, v_f, g_f, gt_f, hf_ref)), (True, (qk_b, v_b, g_b, gt_b, hb_ref))):
        _mlstm_direction(reverse, *refs, gb_ref, gbt_ref, C_scr, n_scr, m_scr, row, col, dk=dk, dv=dv)


def _mlstm_direction(reverse, qk_ref, v_ref, g_ref, gt_ref, h_ref, gb_ref, gbt_ref, C_scr, n_scr, m_scr,
                     row, col, *, dk, dv):
    H = M_HEADS
    gates = g_ref[...] + gb_ref[...]
    gates_t = gt_ref[...] + gbt_ref[...]
    mask = (col >= row) if reverse else (col <= row)
    mask_t = (row >= col) if reverse else (row <= col)
    nt = (((1,), (1,)), ((), ()))
    tn = (((0,), (0,)), ((), ()))

    for h in range(H):
        gi = (2 * H if reverse else 0) + h
        gf = gi + H
        si = (H if reverse else 0) + h
        i_col = gates[:, gi:gi + 1]
        f_col = _log_sigmoid(gates[:, gf:gf + 1])
        i_row = gates_t[gi:gi + 1, :]
        f_row = _log_sigmoid(gates_t[gf:gf + 1, :])
        b_col = jnp.sum(jnp.where(mask, f_row, 0.0), axis=1, keepdims=True)
        b_row = jnp.sum(jnp.where(mask_t, f_col, 0.0), axis=0, keepdims=True)
        b_end = jnp.sum(f_row, axis=1, keepdims=True)
        m_prev = m_scr[si][:, 0:1]

        dmat = jnp.where(mask, b_col - b_row + i_row, NEG)
        m_inter = b_col + m_prev
        m_t = jnp.maximum(m_inter, jnp.max(dmat, axis=1, keepdims=True))
        q = qk_ref[:, h * dk:(h + 1) * dk]
        k = qk_ref[:, (H + h) * dk:(H + h + 1) * dk]
        v = v_ref[:, h * dv:(h + 1) * dv]
        s = lax.dot_general(q, k, nt, preferred_element_type=F32) * jnp.exp(dmat - m_t)
        inter = jnp.exp(m_inter - m_t)
        c_prev = C_scr[si]
        n_prev = n_scr[si]
        num = (inter * jnp.dot(q, c_prev.astype(BF16), preferred_element_type=F32)
               + jnp.dot(s.astype(BF16), v, preferred_element_type=F32))
        den = (inter * jnp.sum(q.astype(F32) * n_prev, axis=1, keepdims=True)
               + jnp.sum(s, axis=1, keepdims=True))
        hout = num / jnp.maximum(jnp.abs(den), jnp.exp(-m_t))

        g_col = b_end - b_col + i_col
        m_new = jnp.maximum(b_end + m_prev, jnp.max(g_col, axis=0, keepdims=True))
        w_col = jnp.exp(g_col - m_new)
        decay = jnp.exp(b_end + m_prev - m_new)
        kw = k.astype(F32) * w_col
        C_scr[si] = decay * c_prev + lax.dot_general(kw.astype(BF16), v, tn, preferred_element_type=F32)
        n_scr[si] = decay * n_prev + jnp.sum(kw, axis=0, keepdims=True)
        m_scr[si] = jnp.broadcast_to(m_new, (1, LANES))
        h_ref[:, h * dv:(h + 1) * dv] = hout


def _mlstm(qk, z, gates, gates_t, gate_b, gate_b_t, *, layer, width, v_blk):
    S = qk.shape[0]
    H = M_HEADS
    dk = qk.shape[1] // (2 * H)
    W = width
    dv = W // H
    L = min(256, S)
    nc = S // L

    def specs(cmap):
        return [pl.BlockSpec((L, 2 * H * dk), lambda c: (cmap(c), 0)),
                pl.BlockSpec((L, W), lambda c: (cmap(c), v_blk)),
                pl.BlockSpec((L, LANES), lambda c: (cmap(c), 0)),
                pl.BlockSpec((4 * H, L), lambda c: (0, cmap(c)))]

    fwd = lambda c: c
    bwd = lambda c: nc - 1 - c
    return pl.pallas_call(
        functools.partial(_mlstm_kernel, dk=dk, dv=dv, L=L),
        grid=(nc,),
        in_specs=specs(fwd) + specs(bwd) + [_layer_vec(layer, LANES),
                                            pl.BlockSpec((None, 4 * H, 1), lambda c: (layer, 0, 0))],
        out_specs=[pl.BlockSpec((L, W), lambda c: (c, 0)), pl.BlockSpec((L, W), lambda c: (bwd(c), 0))],
        out_shape=[jax.ShapeDtypeStruct((S, W), F32), jax.ShapeDtypeStruct((S, W), F32)],
        scratch_shapes=[pltpu.VMEM((2 * H, dk, dv), F32), pltpu.VMEM((2 * H, 1, dk), F32),
                        pltpu.VMEM((2 * H, 1, LANES), F32)],
        compiler_params=_params("arbitrary"),
        name="mlstm_bidir",
    )(qk, z, gates, gates_t, qk, z, gates, gates_t, gate_b, gate_b_t)


def _qkprep_kernel(q_ref, k_ref, v_ref, cos_ref, sin_ref, qg_ref, kg_ref, qt_ref, kn_ref, vt_ref,
                   *, hd, q_scale):
    ts, W = q_ref.shape
    lane = lax.broadcasted_iota(jnp.int32, (ts, LANES), 1)
    lo = lane < hd
    first_half = (lane % hd) < (hd // 8)
    row_lo = lax.broadcasted_iota(jnp.int32, (LANES, ts), 0) < hd
    cos = cos_ref[...]
    sin = sin_ref[...]

    def norm_rope(x, g):
        sq = x * x
        s_lo = jnp.sum(jnp.where(lo, sq, 0.0), axis=1, keepdims=True)
        s_hi = jnp.sum(jnp.where(lo, 0.0, sq), axis=1, keepdims=True)
        inv = lax.rsqrt(jnp.where(lo, s_lo, s_hi) * (1.0 / hd) + EPS)
        xn = x * inv * g
        up = pltpu.roll(xn, LANES - hd // 8, axis=1)
        dn = pltpu.roll(xn, hd // 8, axis=1)
        return xn * cos + jnp.where(first_half, up, dn) * sin

    for j in range(W // LANES):
        sl = slice(j * LANES, (j + 1) * LANES)
        qn = norm_rope(q_ref[:, sl].astype(F32), qg_ref[:, sl]) * q_scale
        qn_t = qn.T
        qt_ref[0, sl, :] = jnp.where(row_lo, qn_t, 0.0).astype(BF16)
        qt_ref[1, sl, :] = jnp.where(row_lo, 0.0, qn_t).astype(BF16)
        kn_ref[:, sl] = norm_rope(k_ref[:, sl].astype(F32), kg_ref[:, sl]).astype(BF16)
        r0 = j * (LANES + BF16_SUBLANES)
        vt_ref[r0:r0 + LANES, :] = v_ref[:, sl].astype(F32).T.astype(BF16)
        vt_ref[r0 + LANES:r0 + LANES + BF16_SUBLANES, :] = jnp.where(
            lax.broadcasted_iota(jnp.int32, (BF16_SUBLANES, ts), 0) == 0, 1.0, 0.0).astype(BF16)


def _qk_prep(z, cos_t, sin_t, qg, kg, *, layer, q_blk, k_blk, v_blk, hd, q_scale):
    S = z.shape[0]
    W = qg.shape[2]
    ts = min(512, S)
    n_slab = W // LANES
    return pl.pallas_call(
        functools.partial(_qkprep_kernel, hd=hd, q_scale=q_scale),
        grid=(S // ts,),
        in_specs=[pl.BlockSpec((ts, W), lambda i: (i, q_blk)),
                  pl.BlockSpec((ts, W), lambda i: (i, k_blk)),
                  pl.BlockSpec((ts, W), lambda i: (i, v_blk)),
                  pl.BlockSpec((ts, LANES), lambda i: (i, 0)),
                  pl.BlockSpec((ts, LANES), lambda i: (i, 0)),
                  _layer_vec(layer, W), _layer_vec(layer, W)],
        out_specs=[pl.BlockSpec((2, W, ts), lambda i: (0, 0, i)),
                   pl.BlockSpec((ts, W), lambda i: (i, 0)),
                   pl.BlockSpec((W + n_slab * BF16_SUBLANES, ts), lambda i: (0, i))],
        out_shape=[jax.ShapeDtypeStruct((2, W, S), BF16), jax.ShapeDtypeStruct((S, W), BF16),
                   jax.ShapeDtypeStruct((W + n_slab * BF16_SUBLANES, S), BF16)],
        compiler_params=_params("arbitrary"),
        name="attn_qk_prep",
    )(z, z, z, cos_t, sin_t, qg, kg)


ATTN_COL_BLOCK = 256
ATTN_KEY_BLOCK = 256
ATTN_COL_GROUP = 8
ATTN_LOOKAHEAD = 8


def _flash_kernel(qa_ref, qb_ref, k_ref, vt_ref, lam_ref, sg_ref, o_ref, q_scr, m_scr, acc_scr,
                  *, tq, tk, lam_init):
    kj = pl.program_id(2)
    cb_w, kb_w = ATTN_COL_BLOCK, min(ATTN_KEY_BLOCK, tk)
    dv = o_ref.shape[1]

    @pl.when(kj == 0)
    def _():
        q_scr[:, 0:tq] = qa_ref[0]
        q_scr[:, tq:2 * tq] = qb_ref[0]
        m_scr[...] = jnp.full_like(m_scr, NEG)
        acc_scr[...] = jnp.zeros_like(acc_scr)

    group = min(ATTN_COL_GROUP, 2 * tq // cb_w)
    steps = [(c, kb) for kb in range(tk // kb_w) for c in range(group)]

    def col_group(gi, carry):
        cols = [pl.ds(pl.multiple_of((gi * group + c) * cb_w, cb_w), cb_w) for c in range(group)]
        qts = [q_scr[:, cs] for cs in cols]
        state = [(m_scr[:, cs], acc_scr[:, cs]) for cs in cols]

        def scores(c, kb):
            return jnp.dot(k_ref[kb * kb_w:(kb + 1) * kb_w, :], qts[c], preferred_element_type=F32)

        pending = [scores(*st) for st in steps[:ATTN_LOOKAHEAD]]
        for n, (c, kb) in enumerate(steps):
            s = pending.pop(0)
            if n + ATTN_LOOKAHEAD < len(steps):
                pending.append(scores(*steps[n + ATTN_LOOKAHEAD]))
            m, acc = state[c]
            m_new = jnp.maximum(m, jnp.max(s, axis=0, keepdims=True))
            alpha = jnp.exp2(m - m_new)
            p = jnp.exp2((s - m_new).astype(BF16))
            acc = alpha * acc + jnp.dot(vt_ref[:, kb * kb_w:(kb + 1) * kb_w], p,
                                        preferred_element_type=F32)
            state[c] = (m_new, acc)
        for cs, (m, acc) in zip(cols, state):
            m_scr[:, cs], acc_scr[:, cs] = m, acc
        return carry

    lax.fori_loop(0, 2 * tq // (cb_w * group), col_group, 0)

    @pl.when(kj == pl.num_programs(2) - 1)
    def _():
        lam = lam_ref[...]
        lam_full = (jnp.exp(jnp.sum(lam[0:1] * lam[1:2], axis=1, keepdims=True))
                    - jnp.exp(jnp.sum(lam[2:3] * lam[3:4], axis=1, keepdims=True)) + lam_init)
        a = acc_scr[0:dv, :] / acc_scr[dv:dv + 1, :]
        o_t = a[:, 0:tq] - lam_full * a[:, tq:2 * tq]
        ms = jnp.mean(o_t * o_t, axis=0, keepdims=True)
        y_t = o_t * lax.rsqrt(ms + EPS) * sg_ref[...] * (1.0 - lam_init)
        o_ref[...] = y_t.T.astype(BF16)


def _flash_diff_attention(q_t, kn, v_t, lam, subln_g, *, layer, lam_init):
    _, W, S = q_t.shape
    H = A_HEADS
    dv = W // H
    tq = min(1024, S)
    tk = min(1024, S)
    return pl.pallas_call(
        functools.partial(_flash_kernel, tq=tq, tk=tk, lam_init=lam_init),
        grid=(H, S // tq, S // tk),
        in_specs=[pl.BlockSpec((1, dv, tq), lambda h, i, j: (0, h, i)),
                  pl.BlockSpec((1, dv, tq), lambda h, i, j: (1, h, i)),
                  pl.BlockSpec((tk, dv), lambda h, i, j: (j, h)),
                  pl.BlockSpec((dv + BF16_SUBLANES, tk), lambda h, i, j: (h, j)),
                  pl.BlockSpec((None,) + lam.shape[1:], lambda h, i, j: (layer, 0, 0)),
                  pl.BlockSpec((None, dv, 1), lambda h, i, j: (layer, 0, 0))],
        out_specs=pl.BlockSpec((tq, dv), lambda h, i, j: (i, h)),
        out_shape=jax.ShapeDtypeStruct((S, W), BF16),
        scratch_shapes=[pltpu.VMEM((dv, 2 * tq), BF16), pltpu.VMEM((1, 2 * tq), F32),
                        pltpu.VMEM((dv + BF16_SUBLANES, 2 * tq), F32)],
        compiler_params=_params("arbitrary", "arbitrary", "arbitrary"),
        name="flash_diff_attn",
    )(q_t, q_t, kn, v_t, lam, subln_g)


def _merge_kernel(hf_ref, hb_ref, mo_ref, ng_ref, ya_ref, wm_ref, wa_ref, gm_ref, ga_ref, o_ref, ym_scr):
    @pl.when(pl.program_id(1) == 0)
    def _():
        dv = hf_ref.shape[1] // M_HEADS
        for h in range(M_HEADS):
            sl = slice(h * dv, (h + 1) * dv)
            hs = hf_ref[:, sl] + hb_ref[:, sl]
            ms = jnp.mean(hs * hs, axis=-1, keepdims=True)
            y = hs * lax.rsqrt(ms + EPS) * ng_ref[:, sl] * _sigmoid(mo_ref[:, sl].astype(F32))
            ym_scr[:, sl] = y.astype(BF16)

    a = jnp.dot(ym_scr[...], wm_ref[...], preferred_element_type=F32)
    b = jnp.dot(ya_ref[...], wa_ref[...], preferred_element_type=F32)
    o_ref[...] = (_sigmoid(gm_ref[...].astype(F32)) * a + _sigmoid(ga_ref[...].astype(F32)) * b).astype(BF16)


def _merge(h_f, h_b, norm_g, y_a, w_m, w_a, z, *, layer, mo_blk, gm_col, ga_col):
    S, W = y_a.shape
    D = w_m.shape[2]
    tm, tn = min(512, S), 1024
    rows = pl.BlockSpec((tm, W), lambda i, j: (i, 0))
    return pl.pallas_call(
        _merge_kernel,
        grid=(S // tm, D // tn),
        in_specs=[rows, rows, pl.BlockSpec((tm, W), lambda i, j: (i, mo_blk)), _layer_vec(layer, W), rows,
                  pl.BlockSpec((None, W, tn), lambda i, j: (layer, 0, j)),
                  pl.BlockSpec((None, W, tn), lambda i, j: (layer, 0, j)),
                  pl.BlockSpec((tm, tn), lambda i, j: (i, gm_col // tn + j)),
                  pl.BlockSpec((tm, tn), lambda i, j: (i, ga_col // tn + j))],
        out_specs=pl.BlockSpec((tm, tn), lambda i, j: (i, j)),
        out_shape=jax.ShapeDtypeStruct((S, D), BF16),
        scratch_shapes=[pltpu.VMEM((tm, W), BF16)],
        compiler_params=_params("arbitrary", "arbitrary"),
        name="branch_merge",
    )(h_f, h_b, z, norm_g, y_a, w_m, w_a, z, z)


def _route(logits):
    G, E = N_GROUPS, EXPERTS_PER_GROUP
    lane = lax.broadcasted_iota(jnp.int32, logits.shape, 1)
    lanef = lane.astype(F32)
    is_g = lane < G
    gl = jnp.where(is_g, logits, NEG)
    ge = jnp.where(is_g, jnp.exp(gl - jnp.max(gl, axis=1, keepdims=True)), 0.0)
    probs = ge / jnp.sum(ge, axis=1, keepdims=True)
    g_val = jnp.max(probs, axis=1, keepdims=True)
    g_idx = jnp.min(jnp.where(is_g & (probs == g_val), lanef, float(LANES)), axis=1, keepdims=True)
    lo = G + E * g_idx
    in_grp = (lanef >= lo) & (lanef < lo + E)
    el = jnp.where(in_grp, logits, NEG)
    v1 = jnp.max(el, axis=1, keepdims=True)
    i1 = jnp.min(jnp.where(in_grp & (el == v1), lanef, float(LANES)), axis=1, keepdims=True)
    rest = in_grp & (lanef != i1)
    el2 = jnp.where(rest, logits, NEG)
    v2 = jnp.max(el2, axis=1, keepdims=True)
    i2 = jnp.min(jnp.where(rest & (el2 == v2), lanef, float(LANES)), axis=1, keepdims=True)
    t = jnp.exp(v2 - v1)
    w1 = (1.0 / (1.0 + t)) * g_val
    w2 = (t / (1.0 + t)) * g_val
    return jnp.where(lane == 0, i1 - G, jnp.where(lane == 1, i2 - G,
                     jnp.where(lane == 2, w1, jnp.where(lane == 3, w2, 0.0))))


def _out_kernel(mg_ref, w_ref, x_ref, g1_ref, n2_ref, sc_ref, sh_ref, wr_ref, br_ref,
                x1_ref, h2_ref, r_ref, cnt_ref, cnt_scr):
    tm = x_ref.shape[0]

    @pl.when(pl.program_id(0) == 0)
    def _():
        cnt_scr[...] = jnp.zeros_like(cnt_scr)

    y = jnp.dot(mg_ref[...], w_ref[...], preferred_element_type=F32)
    x1 = x_ref[...] + g1_ref[...] * y
    x1_ref[...] = x1
    ms = jnp.mean(x1 * x1, axis=-1, keepdims=True)
    h2 = x1 * lax.rsqrt(ms + EPS) * n2_ref[...] * (1.0 + sc_ref[...]) + sh_ref[...]
    h2_ref[...] = h2
    h_hi = h2.astype(BF16)
    h_lo = (h2 - h_hi.astype(F32)).astype(BF16)
    t = jnp.dot(h_hi, wr_ref[...], preferred_element_type=F32)
    logits = (t[:, :LANES] + t[:, LANES:]
              + jnp.dot(h_lo, wr_ref[:, :LANES], preferred_element_type=F32) + br_ref[...])
    route = _route(logits)

    lane = lax.broadcasted_iota(jnp.int32, (tm, LANES), 1)
    lanef = lane.astype(F32)
    sel1 = lanef == route[:, 0:1]
    sel2 = lanef == route[:, 1:2]
    onehot = jnp.where(sel1 | sel2, 1.0, 0.0)
    earlier = (lax.broadcasted_iota(jnp.int32, (tm, tm), 0)
               > lax.broadcasted_iota(jnp.int32, (tm, tm), 1))
    before = cnt_scr[...] + jnp.dot(jnp.where(earlier, 1.0, 0.0).astype(BF16), onehot.astype(BF16),
                                    preferred_element_type=F32)
    rank1 = jnp.sum(jnp.where(sel1, before, 0.0), axis=1, keepdims=True)
    rank2 = jnp.sum(jnp.where(sel2, before, 0.0), axis=1, keepdims=True)
    code1 = route[:, 0:1] * float(RANK_RADIX) + rank1
    code2 = route[:, 1:2] * float(RANK_RADIX) + rank2
    r_ref[...] = jnp.where(lane == 4, code1, jnp.where(lane == 5, code2, route))
    cnt = cnt_scr[...] + jnp.sum(onehot, axis=0, keepdims=True)
    cnt_scr[...] = cnt
    cnt_ref[...] = jnp.broadcast_to(cnt, cnt_ref.shape)


def _out_projection(merged, w_out, x, mod, norm2_g, w_router, b_router, *, layer):
    S, D = x.shape
    tm = min(256, S)
    row = pl.BlockSpec((tm, D), lambda i: (i, 0))
    return pl.pallas_call(
        _out_kernel,
        grid=(S // tm,),
        in_specs=[row, pl.BlockSpec((None, D, D), lambda i: (layer, 0, 0)), row,
                  _layer_vec(layer, D, 2), _layer_vec(layer, D), _layer_vec(layer, D, 4),
                  _layer_vec(layer, D, 3),
                  pl.BlockSpec((None, D, 2 * LANES), lambda i: (layer, 0, 0)),
                  _layer_vec(layer, LANES)],
        out_specs=[row, row, pl.BlockSpec((tm, LANES), lambda i: (i, 0)),
                   pl.BlockSpec((SUBLANES, LANES), lambda i: (0, 0))],
        out_shape=[jax.ShapeDtypeStruct((S, D), F32), jax.ShapeDtypeStruct((S, D), F32),
                   jax.ShapeDtypeStruct((S, LANES), F32), jax.ShapeDtypeStruct((SUBLANES, LANES), F32)],
        scratch_shapes=[pltpu.VMEM((1, LANES), F32)],
        compiler_params=_params("arbitrary"),
        name="out_proj_router",
    )(merged, w_out, x, mod, norm2_g, mod, mod, w_router, b_router)


def _dispatch_plan(route, counts_f, n_experts, tm):
    T = route.shape[0]
    counts = counts_f[0, :n_experts].astype(jnp.int32)
    padded = ((counts + tm - 1) // tm) * tm
    pad_end = jnp.cumsum(padded)
    pad_start = pad_end - padded
    n_tiles = (2 * T) // tm + n_experts
    n_active = (pad_end[-1] // tm).astype(jnp.int32)
    tile_start = jnp.arange(n_tiles, dtype=jnp.int32) * tm
    tile_expert = jnp.sum(pad_end[None, :] <= jnp.minimum(tile_start, pad_end[-1] - 1)[:, None],
                          axis=1, dtype=jnp.int32)
    codes = route[:, 4:6].T.reshape(-1).astype(jnp.int32)
    tail = ((pad_start + counts) // SUBLANES) * SUBLANES
    fill = jnp.concatenate([tail, (pad_end - tail) // SUBLANES, n_active.reshape(1), pad_start])
    return tile_expert, n_active.reshape(1), codes, fill


def _sorted_row(code_ref, fill_ref, idx, n_experts):
    code = code_ref[idx]
    return fill_ref[2 * n_experts + 1 + code // RANK_RADIX] + code % RANK_RADIX


def _dispatch_kernel(code_ref, fill_ref, h_ref, xs_hbm, zero_scr, sem, *, tm, T, n_experts, n_tiles):
    i = pl.program_id(0)

    @pl.when(i == 0)
    def _():
        zero_scr[...] = jnp.zeros_like(zero_scr)
        zero8 = zero_scr.at[pl.ds(0, SUBLANES), :]
        n_active = fill_ref[2 * n_experts]

        def per_expert(e, total):
            def piece(k, carry):
                row = pl.multiple_of(fill_ref[e] + k * SUBLANES, SUBLANES)
                pltpu.make_async_copy(zero8, xs_hbm.at[pl.ds(row, SUBLANES), :], sem.at[1]).start()
                return carry
            n = fill_ref[n_experts + e]
            lax.fori_loop(0, n, piece, 0)
            return total + n

        pieces = lax.fori_loop(0, n_experts, per_expert, 0)

        def unused_tile(t, carry):
            row = pl.multiple_of(t * tm, tm)
            pltpu.make_async_copy(zero_scr, xs_hbm.at[pl.ds(row, tm), :], sem.at[2]).start()
            return carry

        lax.fori_loop(n_active, n_tiles, unused_tile, 0)

        def wait_piece(k, carry):
            pltpu.make_async_copy(zero8, xs_hbm.at[pl.ds(0, SUBLANES), :], sem.at[1]).wait()
            return carry

        lax.fori_loop(0, pieces, wait_piece, 0)

        def wait_tile(t, carry):
            pltpu.make_async_copy(zero_scr, xs_hbm.at[pl.ds(0, tm), :], sem.at[2]).wait()
            return carry

        lax.fori_loop(n_active, n_tiles, wait_tile, 0)

    def body(r, carry):
        for slot in range(2):
            p = _sorted_row(code_ref, fill_ref, slot * T + i * tm + r, n_experts)
            pltpu.make_async_copy(h_ref.at[pl.ds(r, 1), :], xs_hbm.at[pl.ds(p, 1), :], sem.at[0]).start()
        return carry

    lax.fori_loop(0, tm, body, 0, unroll=8)
    for slot in range(2):
        pltpu.make_async_copy(h_ref, xs_hbm.at[pl.ds(0, tm), :], sem.at[0]).wait()


def _dispatch(h2, codes, fill, *, n_experts, tm):
    T, D = h2.shape
    n_tiles = (2 * T) // tm + n_experts
    grid_spec = pltpu.PrefetchScalarGridSpec(
        num_scalar_prefetch=2,
        grid=(T // tm,),
        in_specs=[pl.BlockSpec((tm, D), lambda i, codes, fill: (i, 0))],
        out_specs=pl.BlockSpec(memory_space=pl.ANY),
        scratch_shapes=[pltpu.VMEM((tm, D), F32), pltpu.SemaphoreType.DMA((3,))],
    )
    return pl.pallas_call(
        functools.partial(_dispatch_kernel, tm=tm, T=T, n_experts=n_experts, n_tiles=n_tiles),
        grid_spec=grid_spec,
        out_shape=jax.ShapeDtypeStruct((n_tiles * tm, D), F32),
        compiler_params=_params("arbitrary"),
        name="moe_dispatch",
    )(codes, fill, h2)


def _expert_kernel(te_ref, na_ref, x_ref, wg_ref, wu_ref, wd_ref, o_ref, wg_b, wu_b, wd_b):
    i = pl.program_id(0)

    @pl.when(i >= na_ref[0])
    def _():
        o_ref[...] = jnp.zeros_like(o_ref)

    @pl.when(i < na_ref[0])
    def _():
        @pl.when((i == 0) | (te_ref[i] != te_ref[jnp.maximum(i - 1, 0)]))
        def _():
            wg_b[...] = wg_ref[...].astype(BF16)
            wu_b[...] = wu_ref[...].astype(BF16)
            wd_b[...] = wd_ref[...].astype(BF16)

        x = x_ref[...].astype(BF16)
        g = jnp.dot(x, wg_b[...], preferred_element_type=F32)
        u = jnp.dot(x, wu_b[...], preferred_element_type=F32)
        a = (g * _sigmoid(g) * u).astype(BF16)
        o_ref[...] = jnp.dot(a, wd_b[...], preferred_element_type=F32)


def _experts(xs, tile_expert, n_active, w_gate, w_up, w_down, *, layer, tm):
    D = xs.shape[1]
    Fe = w_gate.shape[-1]
    n_tiles = tile_expert.shape[0]
    tile = lambda i, te, na: (jnp.minimum(i, na[0] - 1), 0)
    grid_spec = pltpu.PrefetchScalarGridSpec(
        num_scalar_prefetch=2,
        grid=(n_tiles,),
        in_specs=[pl.BlockSpec((tm, D), tile),
                  pl.BlockSpec((None, None, D, Fe), lambda i, te, na: (layer, te[i], 0, 0)),
                  pl.BlockSpec((None, None, D, Fe), lambda i, te, na: (layer, te[i], 0, 0)),
                  pl.BlockSpec((None, None, Fe, D), lambda i, te, na: (layer, te[i], 0, 0))],
        out_specs=pl.BlockSpec((tm, D), lambda i, te, na: (i, 0)),
        scratch_shapes=[pltpu.VMEM((D, Fe), BF16), pltpu.VMEM((D, Fe), BF16), pltpu.VMEM((Fe, D), BF16)],
    )
    return pl.pallas_call(
        _expert_kernel,
        grid_spec=grid_spec,
        out_shape=jax.ShapeDtypeStruct((n_tiles * tm, D), F32),
        compiler_params=_params("arbitrary"),
        name="routed_experts",
    )(tile_expert, n_active, xs, w_gate, w_up, w_down)


def _combine_kernel(code_ref, fill_ref, ys_hbm, x_ref, g2_ref, r_ref, o_ref, ybuf, sem,
                    *, tm, T, n_experts):
    i = pl.program_id(0)

    def issue(blk, slot):
        def body(r, carry):
            for k in range(2):
                p = _sorted_row(code_ref, fill_ref, k * T + blk * tm + r, n_experts)
                pltpu.make_async_copy(ys_hbm.at[pl.ds(p, 1), :], ybuf.at[slot, k, pl.ds(r, 1), :],
                                      sem.at[slot]).start()
            return carry
        lax.fori_loop(0, tm, body, 0, unroll=8)

    @pl.when(i == 0)
    def _():
        issue(0, 0)

    @pl.when(i + 1 < pl.num_programs(0))
    def _():
        issue(i + 1, (i + 1) % 2)

    slot = i % 2
    for k in range(2):
        pltpu.make_async_copy(ys_hbm.at[pl.ds(0, tm), :], ybuf.at[slot, k], sem.at[slot]).wait()
    y = r_ref[:, 2:3] * ybuf[slot, 0] + r_ref[:, 3:4] * ybuf[slot, 1]
    o_ref[...] = x_ref[...] + g2_ref[...] * y


def _combine(y_sorted, codes, fill, x1, mod, route, *, layer, n_experts, tm):
    T, D = x1.shape
    grid_spec = pltpu.PrefetchScalarGridSpec(
        num_scalar_prefetch=2,
        grid=(T // tm,),
        in_specs=[pl.BlockSpec(memory_space=pl.ANY),
                  pl.BlockSpec((tm, D), lambda i, codes, fill: (i, 0)),
                  _layer_vec(layer, D, 5),
                  pl.BlockSpec((tm, LANES), lambda i, codes, fill: (i, 0))],
        out_specs=pl.BlockSpec((tm, D), lambda i, codes, fill: (i, 0)),
        scratch_shapes=[pltpu.VMEM((2, 2, tm, D), F32), pltpu.SemaphoreType.DMA((2,))],
    )
    return pl.pallas_call(
        functools.partial(_combine_kernel, tm=tm, T=T, n_experts=n_experts),
        grid_spec=grid_spec,
        out_shape=jax.ShapeDtypeStruct((T, D), F32),
        compiler_params=_params("arbitrary"),
        name="moe_combine",
    )(codes, fill, y_sorted, x1, mod, route)


def _rope_tables(S, hd):
    rope_dim = hd // 4
    posn = jnp.arange(S, dtype=F32)
    inv = ROPE_THETA ** (-jnp.arange(0, rope_dim, 2, dtype=F32) / rope_dim)
    ang = posn[:, None] * inv[None, :]
    cos, sin = jnp.cos(ang), jnp.sin(ang)
    rest = hd - rope_dim
    cos_t = jnp.concatenate([cos, cos, jnp.ones((S, rest), F32)], axis=1)
    sin_t = jnp.concatenate([-sin, sin, jnp.zeros((S, rest), F32)], axis=1)
    reps = LANES // hd
    return jnp.tile(cos_t, (1, reps)), jnp.tile(sin_t, (1, reps))


def kernel(x, c, ada_w, ada_b, norm1_g, norm2_g, w_in, m_conv_w, m_gate_b, m_norm_g, a_qnorm_g,
           a_knorm_g, a_lambda, a_subln_g, w_branch_m, w_branch_a, w_out, rg_w, rg_b, re_w, re_b,
           e_w_gate, e_w_up, e_w_down):
    B, S, D = x.shape
    assert B == 1
    depth = ada_w.shape[0]
    Wm = w_branch_m.shape[1]
    Wa = w_branch_a.shape[1]
    qk_w = m_conv_w.shape[2]
    n_gate = 4 * M_HEADS
    hd = a_qnorm_g.shape[1]
    n_experts = re_w.shape[2]
    n_route = N_GROUPS + n_experts
    assert Wa // A_HEADS == LANES and 2 * hd == LANES
    c_mv, c_mo, c_aq = qk_w, qk_w + Wm, qk_w + 2 * Wm
    c_ak, c_av, c_gm = c_aq + Wa, c_aq + 2 * Wa, c_aq + 3 * Wa
    c_ga = c_gm + D
    g0 = qk_w + Wm

    w_main = jnp.concatenate([w_in[:, :, :g0], w_in[:, :, g0 + n_gate:]], axis=2).astype(BF16)
    w_gate = jnp.pad(w_in[:, :, g0:g0 + n_gate], ((0, 0), (0, 0), (0, LANES - n_gate))).astype(BF16)
    w_bm, w_ba, w_o = w_branch_m.astype(BF16), w_branch_a.astype(BF16), w_out.astype(BF16)
    w_r = jnp.pad(jnp.concatenate([rg_w, re_w], axis=2), ((0, 0), (0, 0), (0, LANES - n_route)))
    w_r_hi = w_r.astype(BF16)
    w_router = jnp.concatenate([w_r_hi, (w_r - w_r_hi.astype(F32)).astype(BF16)], axis=2)
    b_router = jnp.pad(jnp.concatenate([rg_b, re_b], axis=1), ((0, 0), (0, LANES - n_route)))[:, None]
    gate_b = jnp.pad(m_gate_b.reshape(depth, 1, n_gate), ((0, 0), (0, 0), (0, LANES - n_gate)))
    gate_b_t = m_gate_b.reshape(depth, n_gate, 1)
    qg = jnp.tile(a_qnorm_g, (1, Wa // hd))[:, None]
    kg = jnp.tile(a_knorm_g, (1, Wa // hd))[:, None]
    n1, n2, mng = norm1_g[:, None], norm2_g[:, None], m_norm_g[:, None]
    subln = a_subln_g[:, :, None]

    xs = x[0]
    mod = _modulation(c, ada_w, ada_b)
    cos_t, sin_t = _rope_tables(S, hd)
    tm_moe = min(MOE_TILE, S)

    for l in range(depth):
        lam_init = 0.8 - 0.6 * math.exp(-0.3 * l)
        z, gates, gates_t = _in_projection(xs, n1, mod, w_main, w_gate, layer=l)
        qk = _qk_conv(z, m_conv_w, float((qk_w // (2 * M_HEADS)) ** -0.5), layer=l)
        h_f, h_b = _mlstm(qk, z, gates, gates_t, gate_b, gate_b_t, layer=l, width=Wm, v_blk=c_mv // Wm)
        q_t, kn, v_t = _qk_prep(z, cos_t, sin_t, qg, kg, layer=l, q_blk=c_aq // Wa, k_blk=c_ak // Wa,
                                v_blk=c_av // Wa, hd=hd, q_scale=float(hd ** -0.5 * math.log2(math.e)))
        y_a = _flash_diff_attention(q_t, kn, v_t, a_lambda, subln, layer=l, lam_init=lam_init)
        merged = _merge(h_f, h_b, mng, y_a, w_bm, w_ba, z, layer=l, mo_blk=c_mo // Wm,
                        gm_col=c_gm, ga_col=c_ga)
        x1, h2, route, counts = _out_projection(merged, w_o, xs, mod, n2, w_router, b_router, layer=l)
        tile_expert, n_active, codes, fill = _dispatch_plan(route, counts, n_experts, tm_moe)
        x_sorted = _dispatch(h2, codes, fill, n_experts=n_experts, tm=tm_moe)
        y_sorted = _experts(x_sorted, tile_expert, n_active, e_w_gate, e_w_up, e_w_down,
                            layer=l, tm=tm_moe)
        xs = _combine(y_sorted, codes, fill, x1, mod, route, layer=l, n_experts=n_experts, tm=tm_moe)
    return xs[None]
```

```python
import functools
import math

import jax
import jax.numpy as jnp
from jax import lax
from jax.experimental import pallas as pl
from jax.experimental.pallas import tpu as pltpu

F32 = jnp.float32
BF16 = jnp.bfloat16

M_HEADS = 4
A_HEADS = 8
CONV_WIDTH = 5
ROPE_THETA = 500000.0
N_GROUPS = 4
EXPERTS_PER_GROUP = 8
EPS = 1e-6
NEG = -1e30
LANES = 128
SUBLANES = 8
BF16_SUBLANES = 16
VMEM_LIMIT = 56 * 1024 * 1024
MOE_TILE = 256
RANK_RADIX = 65536


def _sigmoid(x):
    return 1.0 / (1.0 + jnp.exp(-x))


def _log_sigmoid(x):
    return jnp.minimum(x, 0.0) - jnp.log(1.0 + jnp.exp(-jnp.abs(x)))


def _params(*sem):
    return pltpu.CompilerParams(dimension_semantics=sem, vmem_limit_bytes=VMEM_LIMIT)


def _layer_vec(layer, width, blk=0):
    return pl.BlockSpec((None, 1, width), lambda *_: (layer, 0, blk))


def _mod_kernel(c_ref, w_ref, b_ref, o_ref):
    c = c_ref[...]
    ca = c * _sigmoid(c)
    o_ref[...] = jnp.sum(ca * w_ref[...], axis=0, keepdims=True) + b_ref[...]


def _modulation(c, ada_w, ada_b):
    L, D, N = ada_w.shape
    tn = min(1024, N)
    return pl.pallas_call(
        _mod_kernel,
        grid=(L, N // tn),
        in_specs=[pl.BlockSpec((D, 1), lambda l, j: (0, 0)),
                  pl.BlockSpec((None, D, tn), lambda l, j: (l, 0, j)),
                  pl.BlockSpec((None, 1, tn), lambda l, j: (l, 0, j))],
        out_specs=pl.BlockSpec((None, 1, tn), lambda l, j: (l, 0, j)),
        out_shape=jax.ShapeDtypeStruct((L, 1, N), F32),
        compiler_params=_params("arbitrary", "arbitrary"),
        name="adaln_mod",
    )(c.reshape(D, 1), ada_w, ada_b.reshape(L, 1, N))


def _inproj_kernel(x_ref, g_ref, sc_ref, sh_ref, w_ref, wg_ref, z_ref, gate_ref, gate_t_ref, h_scr):
    @pl.when(pl.program_id(1) == 0)
    def _():
        x = x_ref[...]
        ms = jnp.mean(x * x, axis=-1, keepdims=True)
        h = x * lax.rsqrt(ms + EPS) * g_ref[...] * (1.0 + sc_ref[...]) + sh_ref[...]
        hb = h.astype(BF16)
        h_scr[...] = hb
        g = jnp.dot(hb, wg_ref[...], preferred_element_type=F32)
        gate_ref[...] = g
        gate_t_ref[...] = g.T

    z_ref[...] = jnp.dot(h_scr[...], w_ref[...], preferred_element_type=F32).astype(BF16)


def _in_projection(x, norm_g, mod, w_main, w_gate, *, layer):
    S, D = x.shape
    N = w_main.shape[2]
    tm, tn = min(1024, S), 1024
    return pl.pallas_call(
        _inproj_kernel,
        grid=(S // tm, N // tn),
        in_specs=[pl.BlockSpec((tm, D), lambda i, j: (i, 0)),
                  _layer_vec(layer, D), _layer_vec(layer, D, 1), _layer_vec(layer, D, 0),
                  pl.BlockSpec((None, D, tn), lambda i, j: (layer, 0, j)),
                  pl.BlockSpec((None, D, LANES), lambda i, j: (layer, 0, 0))],
        out_specs=[pl.BlockSpec((tm, tn), lambda i, j: (i, j)),
                   pl.BlockSpec((tm, LANES), lambda i, j: (i, 0)),
                   pl.BlockSpec((LANES, tm), lambda i, j: (0, i))],
        out_shape=[jax.ShapeDtypeStruct((S, N), BF16), jax.ShapeDtypeStruct((S, LANES), F32),
                   jax.ShapeDtypeStruct((LANES, S), F32)],
        scratch_shapes=[pltpu.VMEM((tm, D), BF16)],
        compiler_params=_params("arbitrary", "arbitrary"),
        name="norm_inproj",
    )(x, norm_g, mod, mod, w_main, w_gate)


def _conv_kernel(prev_ref, cur_ref, next_ref, w_ref, o_ref, ext_scr, *, ts, q_scale):
    i = pl.program_id(0)
    halo = BF16_SUBLANES
    pad = CONV_WIDTH // 2
    ext_scr[0:halo, :] = jnp.where(i > 0, prev_ref[...].astype(F32), 0.0)
    ext_scr[halo:halo + ts, :] = cur_ref[...].astype(F32)
    ext_scr[halo + ts:2 * halo + ts, :] = jnp.where(i < pl.num_programs(0) - 1,
                                                    next_ref[...].astype(F32), 0.0)
    acc = w_ref[0:1, :] * ext_scr[pl.ds(halo - pad, ts), :]
    for k in range(1, CONV_WIDTH):
        acc = acc + w_ref[k:k + 1, :] * ext_scr[pl.ds(halo - pad + k, ts), :]
    y = acc * _sigmoid(acc)
    C = y.shape[1]
    col = lax.broadcasted_iota(jnp.int32, (1, C), 1)
    o_ref[...] = (y * jnp.where(col < C // 2, q_scale, 1.0)).astype(BF16)


def _qk_conv(z, conv_w, q_scale, *, layer):
    S = z.shape[0]
    C = conv_w.shape[2]
    ts = min(512, S)
    halo = BF16_SUBLANES
    r = ts // halo
    nblk = S // halo
    return pl.pallas_call(
        functools.partial(_conv_kernel, ts=ts, q_scale=q_scale),
        grid=(S // ts,),
        in_specs=[pl.BlockSpec((halo, C), lambda i: (jnp.maximum(i * r - 1, 0), 0)),
                  pl.BlockSpec((ts, C), lambda i: (i, 0)),
                  pl.BlockSpec((halo, C), lambda i: (jnp.minimum((i + 1) * r, nblk - 1), 0)),
                  pl.BlockSpec((None, CONV_WIDTH, C), lambda i: (layer, 0, 0))],
        out_specs=pl.BlockSpec((ts, C), lambda i: (i, 0)),
        out_shape=jax.ShapeDtypeStruct((S, C), BF16),
        scratch_shapes=[pltpu.VMEM((ts + 2 * halo, C), F32)],
        compiler_params=_params("arbitrary"),
        name="qk_conv",
    )(z, z, z, conv_w)


def _mlstm_kernel(qk_f, v_f, g_f, gt_f, qk_b, v_b, g_b, gt_b, gb_ref, gbt_ref, hf_ref, hb_ref,
                  C_scr, n_scr, m_scr, *, dk, dv, L):
    @pl.when(pl.program_id(0) == 0)
    def _():
        C_scr[...] = jnp.zeros_like(C_scr)
        n_scr[...] = jnp.zeros_like(n_scr)
        m_scr[...] = jnp.zeros_like(m_scr)

    row = lax.broadcasted_iota(jnp.int32, (L, L), 0)
    col = lax.broadcasted_iota(jnp.int32, (L, L), 1)
    for reverse, refs in ((False, (qk_f, v_f, g_f, gt_f, hf_ref)), (True, (qk_b, v_b, g_b, gt_b, hb_ref))):
        _mlstm_direction(reverse, *refs, gb_ref, gbt_ref, C_scr, n_scr, m_scr, row, col, dk=dk, dv=dv)


def _mlstm_direction(reverse, qk_ref, v_ref, g_ref, gt_ref, h_ref, gb_ref, gbt_ref, C_scr, n_scr, m_scr,
                     row, col, *, dk, dv):
    H = M_HEADS
    gates = g_ref[...] + gb_ref[...]
    gates_t = gt_ref[...] + gbt_ref[...]
    mask = (col >= row) if reverse else (col <= row)
    mask_t = (row >= col) if reverse else (row <= col)
    nt = (((1,), (1,)), ((), ()))
    tn = (((0,), (0,)), ((), ()))

    for h in range(H):
        gi = (2 * H if reverse else 0) + h
        gf = gi + H
        si = (H if reverse else 0) + h
        i_col = gates[:, gi:gi + 1]
        f_col = _log_sigmoid(gates[:, gf:gf + 1])
        i_row = gates_t[gi:gi + 1, :]
        f_row = _log_sigmoid(gates_t[gf:gf + 1, :])
        b_col = jnp.sum(jnp.where(mask, f_row, 0.0), axis=1, keepdims=True)
        b_row = jnp.sum(jnp.where(mask_t, f_col, 0.0), axis=0, keepdims=True)
        b_end = jnp.sum(f_row, axis=1, keepdims=True)
        m_prev = m_scr[si][:, 0:1]

        dmat = jnp.where(mask, b_col - b_row + i_row, NEG)
        m_inter = b_col + m_prev
        m_t = jnp.maximum(m_inter, jnp.max(dmat, axis=1, keepdims=True))
        q = qk_ref[:, h * dk:(h + 1) * dk]
        k = qk_ref[:, (H + h) * dk:(H + h + 1) * dk]
        v = v_ref[:, h * dv:(h + 1) * dv]
        s = lax.dot_general(q, k, nt, preferred_element_type=F32) * jnp.exp(dmat - m_t)
        inter = jnp.exp(m_inter - m_t)
        c_prev = C_scr[si]
        n_prev = n_scr[si]
        num = (inter * jnp.dot(q, c_prev.astype(BF16), preferred_element_type=F32)
               + jnp.dot(s.astype(BF16), v, preferred_element_type=F32))
        den = (inter * jnp.sum(q.astype(F32) * n_prev, axis=1, keepdims=True)
               + jnp.sum(s, axis=1, keepdims=True))
        hout = num / jnp.maximum(jnp.abs(den), jnp.exp(-m_t))

        g_col = b_end - b_col + i_col
        m_new = jnp.maximum(b_end + m_prev, jnp.max(g_col, axis=0, keepdims=True))
        w_col = jnp.exp(g_col - m_new)
        decay = jnp.exp(b_end + m_prev - m_new)
        kw = k.astype(F32) * w_col
        C_scr[si] = decay * c_prev + lax.dot_general(kw.astype(BF16), v, tn, preferred_element_type=F32)
        n_scr[si] = decay * n_prev + jnp.sum(kw, axis=0, keepdims=True)
        m_scr[si] = jnp.broadcast_to(m_new, (1, LANES))
        h_ref[:, h * dv:(h + 1) * dv] = hout


def _mlstm(qk, z, gates, gates_t, gate_b, gate_b_t, *, layer, width, v_blk):
    S = qk.shape[0]
    H = M_HEADS
    dk = qk.shape[1] // (2 * H)
    W = width
    dv = W // H
    L = min(256, S)
    nc = S // L

    def specs(cmap):
        return [pl.BlockSpec((L, 2 * H * dk), lambda c: (cmap(c), 0)),
                pl.BlockSpec((L, W), lambda c: (cmap(c), v_blk)),
                pl.BlockSpec((L, LANES), lambda c: (cmap(c), 0)),
                pl.BlockSpec((4 * H, L), lambda c: (0, cmap(c)))]

    fwd = lambda c: c
    bwd = lambda c: nc - 1 - c
    return pl.pallas_call(
        functools.partial(_mlstm_kernel, dk=dk, dv=dv, L=L),
        grid=(nc,),
        in_specs=specs(fwd) + specs(bwd) + [_layer_vec(layer, LANES),
                                            pl.BlockSpec((None, 4 * H, 1), lambda c: (layer, 0, 0))],
        out_specs=[pl.BlockSpec((L, W), lambda c: (c, 0)), pl.BlockSpec((L, W), lambda c: (bwd(c), 0))],
        out_shape=[jax.ShapeDtypeStruct((S, W), F32), jax.ShapeDtypeStruct((S, W), F32)],
        scratch_shapes=[pltpu.VMEM((2 * H, dk, dv), F32), pltpu.VMEM((2 * H, 1, dk), F32),
                        pltpu.VMEM((2 * H, 1, LANES), F32)],
        compiler_params=_params("arbitrary"),
        name="mlstm_bidir",
    )(qk, z, gates, gates_t, qk, z, gates, gates_t, gate_b, gate_b_t)


def _qkprep_kernel(q_ref, k_ref, v_ref, cos_ref, sin_ref, qg_ref, kg_ref, qt_ref, kn_ref, vt_ref,
                   *, hd, q_scale):
    ts, W = q_ref.shape
    lane = lax.broadcasted_iota(jnp.int32, (ts, LANES), 1)
    lo = lane < hd
    first_half = (lane % hd) < (hd // 8)
    row_lo = lax.broadcasted_iota(jnp.int32, (LANES, ts), 0) < hd
    cos = cos_ref[...]
    sin = sin_ref[...]

    def norm_rope(x, g):
        sq = x * x
        s_lo = jnp.sum(jnp.where(lo, sq, 0.0), axis=1, keepdims=True)
        s_hi = jnp.sum(jnp.where(lo, 0.0, sq), axis=1, keepdims=True)
        inv = lax.rsqrt(jnp.where(lo, s_lo, s_hi) * (1.0 / hd) + EPS)
        xn = x * inv * g
        up = pltpu.roll(xn, LANES - hd // 8, axis=1)
        dn = pltpu.roll(xn, hd // 8, axis=1)
        return xn * cos + jnp.where(first_half, up, dn) * sin

    for j in range(W // LANES):
        sl = slice(j * LANES, (j + 1) * LANES)
        qn = norm_rope(q_ref[:, sl].astype(F32), qg_ref[:, sl]) * q_scale
        qn_t = qn.T
        qt_ref[0, sl, :] = jnp.where(row_lo, qn_t, 0.0).astype(BF16)
        qt_ref[1, sl, :] = jnp.where(row_lo, 0.0, qn_t).astype(BF16)
        kn_ref[:, sl] = norm_rope(k_ref[:, sl].astype(F32), kg_ref[:, sl]).astype(BF16)
        r0 = j * (LANES + BF16_SUBLANES)
        vt_ref[r0:r0 + LANES, :] = v_ref[:, sl].astype(F32).T.astype(BF16)
        vt_ref[r0 + LANES:r0 + LANES + BF16_SUBLANES, :] = jnp.where(
            lax.broadcasted_iota(jnp.int32, (BF16_SUBLANES, ts), 0) == 0, 1.0, 0.0).astype(BF16)


def _qk_prep(z, cos_t, sin_t, qg, kg, *, layer, q_blk, k_blk, v_blk, hd, q_scale):
    S = z.shape[0]
    W = qg.shape[2]
    ts = min(512, S)
    n_slab = W // LANES
    return pl.pallas_call(
        functools.partial(_qkprep_kernel, hd=hd, q_scale=q_scale),
        grid=(S // ts,),
        in_specs=[pl.BlockSpec((ts, W), lambda i: (i, q_blk)),
                  pl.BlockSpec((ts, W), lambda i: (i, k_blk)),
                  pl.BlockSpec((ts, W), lambda i: (i, v_blk)),
                  pl.BlockSpec((ts, LANES), lambda i: (i, 0)),
                  pl.BlockSpec((ts, LANES), lambda i: (i, 0)),
                  _layer_vec(layer, W), _layer_vec(layer, W)],
        out_specs=[pl.BlockSpec((2, W, ts), lambda i: (0, 0, i)),
                   pl.BlockSpec((ts, W), lambda i: (i, 0)),
                   pl.BlockSpec((W + n_slab * BF16_SUBLANES, ts), lambda i: (0, i))],
        out_shape=[jax.ShapeDtypeStruct((2, W, S), BF16), jax.ShapeDtypeStruct((S, W), BF16),
                   jax.ShapeDtypeStruct((W + n_slab * BF16_SUBLANES, S), BF16)],
        compiler_params=_params("arbitrary"),
        name="attn_qk_prep",
    )(z, z, z, cos_t, sin_t, qg, kg)


ATTN_COL_BLOCK = 256
ATTN_KEY_BLOCK = 256
ATTN_COL_GROUP = 8
ATTN_LOOKAHEAD = 8


def _flash_kernel(qa_ref, qb_ref, k_ref, vt_ref, lam_ref, sg_ref, o_ref, q_scr, m_scr, acc_scr,
                  *, tq, tk, lam_init):
    kj = pl.program_id(2)
    cb_w, kb_w = ATTN_COL_BLOCK, min(ATTN_KEY_BLOCK, tk)
    dv = o_ref.shape[1]

    @pl.when(kj == 0)
    def _():
        q_scr[:, 0:tq] = qa_ref[0]
        q_scr[:, tq:2 * tq] = qb_ref[0]
        m_scr[...] = jnp.full_like(m_scr, NEG)
        acc_scr[...] = jnp.zeros_like(acc_scr)

    group = min(ATTN_COL_GROUP, 2 * tq // cb_w)
    steps = [(c, kb) for kb in range(tk // kb_w) for c in range(group)]

    def col_group(gi, carry):
        cols = [pl.ds(pl.multiple_of((gi * group + c) * cb_w, cb_w), cb_w) for c in range(group)]
        qts = [q_scr[:, cs] for cs in cols]
        state = [(m_scr[:, cs], acc_scr[:, cs]) for cs in cols]

        def scores(c, kb):
            return jnp.dot(k_ref[kb * kb_w:(kb + 1) * kb_w, :], qts[c], preferred_element_type=F32)

        pending = [scores(*st) for st in steps[:ATTN_LOOKAHEAD]]
        for n, (c, kb) in enumerate(steps):
            s = pending.pop(0)
            if n + ATTN_LOOKAHEAD < len(steps):
                pending.append(scores(*steps[n + ATTN_LOOKAHEAD]))
            m, acc = state[c]
            m_new = jnp.maximum(m, jnp.max(s, axis=0, keepdims=True))
            alpha = jnp.exp2(m - m_new)
            p = jnp.exp2((s - m_new).astype(BF16))
            acc = alpha * acc + jnp.dot(vt_ref[:, kb * kb_w:(kb + 1) * kb_w], p,
                                        preferred_element_type=F32)
            state[c] = (m_new, acc)
        for cs, (m, acc) in zip(cols, state):
            m_scr[:, cs], acc_scr[:, cs] = m, acc
        return carry

    lax.fori_loop(0, 2 * tq // (cb_w * group), col_group, 0)

    @pl.when(kj == pl.num_programs(2) - 1)
    def _():
        lam = lam_ref[...]
        lam_full = (jnp.exp(jnp.sum(lam[0:1] * lam[1:2], axis=1, keepdims=True))
                    - jnp.exp(jnp.sum(lam[2:3] * lam[3:4], axis=1, keepdims=True)) + lam_init)
        a = acc_scr[0:dv, :] / acc_scr[dv:dv + 1, :]
        o_t = a[:, 0:tq] - lam_full * a[:, tq:2 * tq]
        ms = jnp.mean(o_t * o_t, axis=0, keepdims=True)
        y_t = o_t * lax.rsqrt(ms + EPS) * sg_ref[...] * (1.0 - lam_init)
        o_ref[...] = y_t.T.astype(BF16)


def _flash_diff_attention(q_t, kn, v_t, lam, subln_g, *, layer, lam_init):
    _, W, S = q_t.shape
    H = A_HEADS
    dv = W // H
    tq = min(1024, S)
    tk = min(1024, S)
    return pl.pallas_call(
        functools.partial(_flash_kernel, tq=tq, tk=tk, lam_init=lam_init),
        grid=(H, S // tq, S // tk),
        in_specs=[pl.BlockSpec((1, dv, tq), lambda h, i, j: (0, h, i)),
                  pl.BlockSpec((1, dv, tq), lambda h, i, j: (1, h, i)),
                  pl.BlockSpec((tk, dv), lambda h, i, j: (j, h)),
                  pl.BlockSpec((dv + BF16_SUBLANES, tk), lambda h, i, j: (h, j)),
                  pl.BlockSpec((None,) + lam.shape[1:], lambda h, i, j: (layer, 0, 0)),
                  pl.BlockSpec((None, dv, 1), lambda h, i, j: (layer, 0, 0))],
        out_specs=pl.BlockSpec((tq, dv), lambda h, i, j: (i, h)),
        out_shape=jax.ShapeDtypeStruct((S, W), BF16),
        scratch_shapes=[pltpu.VMEM((dv, 2 * tq), BF16), pltpu.VMEM((1, 2 * tq), F32),
                        pltpu.VMEM((dv + BF16_SUBLANES, 2 * tq), F32)],
        compiler_params=_params("arbitrary", "arbitrary", "arbitrary"),
        name="flash_diff_attn",
    )(q_t, q_t, kn, v_t, lam, subln_g)


def _merge_kernel(hf_ref, hb_ref, mo_ref, ng_ref, ya_ref, wm_ref, wa_ref, gm_ref, ga_ref, o_ref, ym_scr):
    @pl.when(pl.program_id(1) == 0)
    def _():
        dv = hf_ref.shape[1] // M_HEADS
        for h in range(M_HEADS):
            sl = slice(h * dv, (h + 1) * dv)
            hs = hf_ref[:, sl] + hb_ref[:, sl]
            ms = jnp.mean(hs * hs, axis=-1, keepdims=True)
            y = hs * lax.rsqrt(ms + EPS) * ng_ref[:, sl] * _sigmoid(mo_ref[:, sl].astype(F32))
            ym_scr[:, sl] = y.astype(BF16)

    a = jnp.dot(ym_scr[...], wm_ref[...], preferred_element_type=F32)
    b = jnp.dot(ya_ref[...], wa_ref[...], preferred_element_type=F32)
    o_ref[...] = (_sigmoid(gm_ref[...].astype(F32)) * a + _sigmoid(ga_ref[...].astype(F32)) * b).astype(BF16)


def _merge(h_f, h_b, norm_g, y_a, w_m, w_a, z, *, layer, mo_blk, gm_col, ga_col):
    S, W = y_a.shape
    D = w_m.shape[2]
    tm, tn = min(512, S), 1024
    rows = pl.BlockSpec((tm, W), lambda i, j: (i, 0))
    return pl.pallas_call(
        _merge_kernel,
        grid=(S // tm, D // tn),
        in_specs=[rows, rows, pl.BlockSpec((tm, W), lambda i, j: (i, mo_blk)), _layer_vec(layer, W), rows,
                  pl.BlockSpec((None, W, tn), lambda i, j: (layer, 0, j)),
                  pl.BlockSpec((None, W, tn), lambda i, j: (layer, 0, j)),
                  pl.BlockSpec((tm, tn), lambda i, j: (i, gm_col // tn + j)),
                  pl.BlockSpec((tm, tn), lambda i, j: (i, ga_col // tn + j))],
        out_specs=pl.BlockSpec((tm, tn), lambda i, j: (i, j)),
        out_shape=jax.ShapeDtypeStruct((S, D), BF16),
        scratch_shapes=[pltpu.VMEM((tm, W), BF16)],
        compiler_params=_params("arbitrary", "arbitrary"),
        name="branch_merge",
    )(h_f, h_b, z, norm_g, y_a, w_m, w_a, z, z)


def _route(logits):
    G, E = N_GROUPS, EXPERTS_PER_GROUP
    lane = lax.broadcasted_iota(jnp.int32, logits.shape, 1)
    lanef = lane.astype(F32)
    is_g = lane < G
    gl = jnp.where(is_g, logits, NEG)
    ge = jnp.where(is_g, jnp.exp(gl - jnp.max(gl, axis=1, keepdims=True)), 0.0)
    probs = ge / jnp.sum(ge, axis=1, keepdims=True)
    g_val = jnp.max(probs, axis=1, keepdims=True)
    g_idx = jnp.min(jnp.where(is_g & (probs == g_val), lanef, float(LANES)), axis=1, keepdims=True)
    lo = G + E * g_idx
    in_grp = (lanef >= lo) & (lanef < lo + E)
    el = jnp.where(in_grp, logits, NEG)
    v1 = jnp.max(el, axis=1, keepdims=True)
    i1 = jnp.min(jnp.where(in_grp & (el == v1), lanef, float(LANES)), axis=1, keepdims=True)
    rest = in_grp & (lanef != i1)
    el2 = jnp.where(rest, logits, NEG)
    v2 = jnp.max(el2, axis=1, keepdims=True)
    i2 = jnp.min(jnp.where(rest & (el2 == v2), lanef, float(LANES)), axis=1, keepdims=True)
    t = jnp.exp(v2 - v1)
    w1 = (1.0 / (1.0 + t)) * g_val
    w2 = (t / (1.0 + t)) * g_val
    return jnp.where(lane == 0, i1 - G, jnp.where(lane == 1, i2 - G,
                     jnp.where(lane == 2, w1, jnp.where(lane == 3, w2, 0.0))))


def _out_kernel(mg_ref, w_ref, x_ref, g1_ref, n2_ref, sc_ref, sh_ref, wr_ref, br_ref,
                x1_ref, h2_ref, r_ref, cnt_ref, cnt_scr):
    tm = x_ref.shape[0]

    @pl.when(pl.program_id(0) == 0)
    def _():
        cnt_scr[...] = jnp.zeros_like(cnt_scr)

    y = jnp.dot(mg_ref[...], w_ref[...], preferred_element_type=F32)
    x1 = x_ref[...] + g1_ref[...] * y
    x1_ref[...] = x1
    ms = jnp.mean(x1 * x1, axis=-1, keepdims=True)
    h2 = x1 * lax.rsqrt(ms + EPS) * n2_ref[...] * (1.0 + sc_ref[...]) + sh_ref[...]
    h2_ref[...] = h2
    h_hi = h2.astype(BF16)
    h_lo = (h2 - h_hi.astype(F32)).astype(BF16)
    t = jnp.dot(h_hi, wr_ref[...], preferred_element_type=F32)
    logits = (t[:, :LANES] + t[:, LANES:]
              + jnp.dot(h_lo, wr_ref[:, :LANES], preferred_element_type=F32) + br_ref[...])
    route = _route(logits)

    lane = lax.broadcasted_iota(jnp.int32, (tm, LANES), 1)
    lanef = lane.astype(F32)
    sel1 = lanef == route[:, 0:1]
    sel2 = lanef == route[:, 1:2]
    onehot = jnp.where(sel1 | sel2, 1.0, 0.0)
    earlier = (lax.broadcasted_iota(jnp.int32, (tm, tm), 0)
               > lax.broadcasted_iota(jnp.int32, (tm, tm), 1))
    before = cnt_scr[...] + jnp.dot(jnp.where(earlier, 1.0, 0.0).astype(BF16), onehot.astype(BF16),
                                    preferred_element_type=F32)
    rank1 = jnp.sum(jnp.where(sel1, before, 0.0), axis=1, keepdims=True)
    rank2 = jnp.sum(jnp.where(sel2, before, 0.0), axis=1, keepdims=True)
    code1 = route[:, 0:1] * float(RANK_RADIX) + rank1
    code2 = route[:, 1:2] * float(RANK_RADIX) + rank2
    r_ref[...] = jnp.where(lane == 4, code1, jnp.where(lane == 5, code2, route))
    cnt = cnt_scr[...] + jnp.sum(onehot, axis=0, keepdims=True)
    cnt_scr[...] = cnt
    cnt_ref[...] = jnp.broadcast_to(cnt, cnt_ref.shape)


def _out_projection(merged, w_out, x, mod, norm2_g, w_router, b_router, *, layer):
    S, D = x.shape
    tm = min(256, S)
    row = pl.BlockSpec((tm, D), lambda i: (i, 0))
    return pl.pallas_call(
        _out_kernel,
        grid=(S // tm,),
        in_specs=[row, pl.BlockSpec((None, D, D), lambda i: (layer, 0, 0)), row,
                  _layer_vec(layer, D, 2), _layer_vec(layer, D), _layer_vec(layer, D, 4),
                  _layer_vec(layer, D, 3),
                  pl.BlockSpec((None, D, 2 * LANES), lambda i: (layer, 0, 0)),
                  _layer_vec(layer, LANES)],
        out_specs=[row, row, pl.BlockSpec((tm, LANES), lambda i: (i, 0)),
                   pl.BlockSpec((SUBLANES, LANES), lambda i: (0, 0))],
        out_shape=[jax.ShapeDtypeStruct((S, D), F32), jax.ShapeDtypeStruct((S, D), F32),
                   jax.ShapeDtypeStruct((S, LANES), F32), jax.ShapeDtypeStruct((SUBLANES, LANES), F32)],
        scratch_shapes=[pltpu.VMEM((1, LANES), F32)],
        compiler_params=_params("arbitrary"),
        name="out_proj_router",
    )(merged, w_out, x, mod, norm2_g, mod, mod, w_router, b_router)


def _dispatch_plan(route, counts_f, n_experts, tm):
    T = route.shape[0]
    counts = counts_f[0, :n_experts].astype(jnp.int32)
    padded = ((counts + tm - 1) // tm) * tm
    pad_end = jnp.cumsum(padded)
    pad_start = pad_end - padded
    n_tiles = (2 * T) // tm + n_experts
    n_active = (pad_end[-1] // tm).astype(jnp.int32)
    tile_start = jnp.arange(n_tiles, dtype=jnp.int32) * tm
    tile_expert = jnp.sum(pad_end[None, :] <= jnp.minimum(tile_start, pad_end[-1] - 1)[:, None],
                          axis=1, dtype=jnp.int32)
    codes = route[:, 4:6].T.reshape(-1).astype(jnp.int32)
    tail = ((pad_start + counts) // SUBLANES) * SUBLANES
    fill = jnp.concatenate([tail, (pad_end - tail) // SUBLANES, n_active.reshape(1), pad_start])
    ids = jnp.arange(n_experts, dtype=jnp.int32)
    has_rows = counts > 0
    later = (ids[None, :] > ids[:, None]) & has_rows[None, :]
    next_expert = jnp.min(jnp.where(later, ids[None, :], n_experts), axis=1)
    next_expert = jnp.where(next_expert == n_experts, -1, next_expert).astype(jnp.int32)
    expert_ord = jnp.sum((ids[None, :] < ids[:, None]) & has_rows[None, :], axis=1, dtype=jnp.int32)
    return tile_expert, n_active.reshape(1), codes, fill, next_expert, expert_ord


def _sorted_row(code_ref, fill_ref, idx, n_experts):
    code = code_ref[idx]
    expert = lax.shift_right_logical(code, RANK_RADIX.bit_length() - 1)
    return fill_ref[2 * n_experts + 1 + expert] + jnp.bitwise_and(code, RANK_RADIX - 1)


def _dispatch_kernel(code_ref, fill_ref, h_ref, xs_hbm, zero_scr, sem, *, tm, T, n_experts, n_tiles):
    i = pl.program_id(0)

    @pl.when(i == 0)
    def _():
        zero_scr[...] = jnp.zeros_like(zero_scr)
        zero8 = zero_scr.at[pl.ds(0, SUBLANES), :]
        n_active = fill_ref[2 * n_experts]

        def per_expert(e, total):
            def piece(k, carry):
                row = pl.multiple_of(fill_ref[e] + k * SUBLANES, SUBLANES)
                pltpu.make_async_copy(zero8, xs_hbm.at[pl.ds(row, SUBLANES), :], sem.at[1]).start()
                return carry
            n = fill_ref[n_experts + e]
            lax.fori_loop(0, n, piece, 0)
            return total + n

        pieces = lax.fori_loop(0, n_experts, per_expert, 0)

        def unused_tile(t, carry):
            row = pl.multiple_of(t * tm, tm)
            pltpu.make_async_copy(zero_scr, xs_hbm.at[pl.ds(row, tm), :], sem.at[2]).start()
            return carry

        lax.fori_loop(n_active, n_tiles, unused_tile, 0)

        def wait_piece(k, carry):
            pltpu.make_async_copy(zero8, xs_hbm.at[pl.ds(0, SUBLANES), :], sem.at[1]).wait()
            return carry

        lax.fori_loop(0, pieces, wait_piece, 0)

        def wait_tile(t, carry):
            pltpu.make_async_copy(zero_scr, xs_hbm.at[pl.ds(0, tm), :], sem.at[2]).wait()
            return carry

        lax.fori_loop(n_active, n_tiles, wait_tile, 0)

    def body(r, carry):
        for slot in range(2):
            p = _sorted_row(code_ref, fill_ref, slot * T + i * tm + r, n_experts)
            pltpu.make_async_copy(h_ref.at[pl.ds(r, 1), :], xs_hbm.at[pl.ds(p, 1), :], sem.at[0]).start()
        return carry

    lax.fori_loop(0, tm, body, 0, unroll=8)
    for slot in range(2):
        pltpu.make_async_copy(h_ref, xs_hbm.at[pl.ds(0, tm), :], sem.at[0]).wait()


def _dispatch(h2, codes, fill, *, n_experts, tm):
    T, D = h2.shape
    n_tiles = (2 * T) // tm + n_experts
    grid_spec = pltpu.PrefetchScalarGridSpec(
        num_scalar_prefetch=2,
        grid=(T // tm,),
        in_specs=[pl.BlockSpec((tm, D), lambda i, codes, fill: (i, 0))],
        out_specs=pl.BlockSpec(memory_space=pl.ANY),
        scratch_shapes=[pltpu.VMEM((tm, D), F32), pltpu.SemaphoreType.DMA((3,))],
    )
    return pl.pallas_call(
        functools.partial(_dispatch_kernel, tm=tm, T=T, n_experts=n_experts, n_tiles=n_tiles),
        grid_spec=grid_spec,
        out_shape=jax.ShapeDtypeStruct((n_tiles * tm, D), F32),
        compiler_params=_params("arbitrary"),
        name="moe_dispatch",
    )(codes, fill, h2)


def _expert_kernel(te_ref, na_ref, nx_ref, od_ref, x_ref, wg_hbm, wu_hbm, wd_hbm, o_ref,
                   wg_f, wu_f, wd_f, wg_b, wu_b, wd_b, sem, *, layer):
    i = pl.program_id(0)

    def weight_copies(e, slot):
        return [pltpu.make_async_copy(src.at[layer, e], dst.at[slot], sem.at[slot, k])
                for k, (src, dst) in enumerate(((wg_hbm, wg_f), (wu_hbm, wu_f), (wd_hbm, wd_f)))]

    @pl.when(i >= na_ref[0])
    def _():
        o_ref[...] = jnp.zeros_like(o_ref)

    @pl.when(i < na_ref[0])
    def _():
        e = te_ref[i]

        @pl.when((i == 0) | (e != te_ref[jnp.maximum(i - 1, 0)]))
        def _():
            slot = od_ref[e] % 2

            @pl.when(i == 0)
            def _():
                for cp in weight_copies(e, slot):
                    cp.start()

            for cp in weight_copies(e, slot):
                cp.wait()
            nxt = nx_ref[e]

            @pl.when(nxt >= 0)
            def _():
                for cp in weight_copies(nxt, 1 - slot):
                    cp.start()

            wg_b[...] = wg_f[slot].astype(BF16)
            wu_b[...] = wu_f[slot].astype(BF16)
            wd_b[...] = wd_f[slot].astype(BF16)

        x = x_ref[...].astype(BF16)
        g = jnp.dot(x, wg_b[...], preferred_element_type=F32)
        u = jnp.dot(x, wu_b[...], preferred_element_type=F32)
        a = (g * _sigmoid(g) * u).astype(BF16)
        o_ref[...] = jnp.dot(a, wd_b[...], preferred_element_type=F32)


def _experts(xs, tile_expert, n_active, next_expert, expert_ord, w_gate, w_up, w_down, *, layer, tm):
    D = xs.shape[1]
    Fe = w_gate.shape[-1]
    n_tiles = tile_expert.shape[0]
    hbm = pl.BlockSpec(memory_space=pl.ANY)
    grid_spec = pltpu.PrefetchScalarGridSpec(
        num_scalar_prefetch=4,
        grid=(n_tiles,),
        in_specs=[pl.BlockSpec((tm, D), lambda i, te, na, nx, od: (jnp.minimum(i, na[0] - 1), 0)),
                  hbm, hbm, hbm],
        out_specs=pl.BlockSpec((tm, D), lambda i, te, na, nx, od: (i, 0)),
        scratch_shapes=[pltpu.VMEM((2, D, Fe), F32), pltpu.VMEM((2, D, Fe), F32), pltpu.VMEM((2, Fe, D), F32),
                        pltpu.VMEM((D, Fe), BF16), pltpu.VMEM((D, Fe), BF16), pltpu.VMEM((Fe, D), BF16),
                        pltpu.SemaphoreType.DMA((2, 3))],
    )
    return pl.pallas_call(
        functools.partial(_expert_kernel, layer=layer),
        grid_spec=grid_spec,
        out_shape=jax.ShapeDtypeStruct((n_tiles * tm, D), F32),
        compiler_params=_params("arbitrary"),
        name="routed_experts",
    )(tile_expert, n_active, next_expert, expert_ord, xs, w_gate, w_up, w_down)


def _combine_kernel(code_ref, fill_ref, ys_hbm, x_ref, g2_ref, r_ref, o_ref, ybuf, sem,
                    *, tm, T, n_experts):
    i = pl.program_id(0)

    def issue(blk, slot):
        def body(r, carry):
            for k in range(2):
                p = _sorted_row(code_ref, fill_ref, k * T + blk * tm + r, n_experts)
                pltpu.make_async_copy(ys_hbm.at[pl.ds(p, 1), :], ybuf.at[slot, k, pl.ds(r, 1), :],
                                      sem.at[slot]).start()
            return carry
        lax.fori_loop(0, tm, body, 0, unroll=8)

    @pl.when(i == 0)
    def _():
        issue(0, 0)

    @pl.when(i + 1 < pl.num_programs(0))
    def _():
        issue(i + 1, (i + 1) % 2)

    slot = i % 2
    for k in range(2):
        pltpu.make_async_copy(ys_hbm.at[pl.ds(0, tm), :], ybuf.at[slot, k], sem.at[slot]).wait()
    y = r_ref[:, 2:3] * ybuf[slot, 0] + r_ref[:, 3:4] * ybuf[slot, 1]
    o_ref[...] = x_ref[...] + g2_ref[...] * y


def _combine(y_sorted, codes, fill, x1, mod, route, *, layer, n_experts, tm):
    T, D = x1.shape
    grid_spec = pltpu.PrefetchScalarGridSpec(
        num_scalar_prefetch=2,
        grid=(T // tm,),
        in_specs=[pl.BlockSpec(memory_space=pl.ANY),
                  pl.BlockSpec((tm, D), lambda i, codes, fill: (i, 0)),
                  _layer_vec(layer, D, 5),
                  pl.BlockSpec((tm, LANES), lambda i, codes, fill: (i, 0))],
        out_specs=pl.BlockSpec((tm, D), lambda i, codes, fill: (i, 0)),
        scratch_shapes=[pltpu.VMEM((2, 2, tm, D), F32), pltpu.SemaphoreType.DMA((2,))],
    )
    return pl.pallas_call(
        functools.partial(_combine_kernel, tm=tm, T=T, n_experts=n_experts),
        grid_spec=grid_spec,
        out_shape=jax.ShapeDtypeStruct((T, D), F32),
        compiler_params=_params("arbitrary"),
        name="moe_combine",
    )(codes, fill, y_sorted, x1, mod, route)


def _rope_tables(S, hd):
    rope_dim = hd // 4
    posn = jnp.arange(S, dtype=F32)
    inv = ROPE_THETA ** (-jnp.arange(0, rope_dim, 2, dtype=F32) / rope_dim)
    ang = posn[:, None] * inv[None, :]
    cos, sin = jnp.cos(ang), jnp.sin(ang)
    rest = hd - rope_dim
    cos_t = jnp.concatenate([cos, cos, jnp.ones((S, rest), F32)], axis=1)
    sin_t = jnp.concatenate([-sin, sin, jnp.zeros((S, rest), F32)], axis=1)
    reps = LANES // hd
    return jnp.tile(cos_t, (1, reps)), jnp.tile(sin_t, (1, reps))


def kernel(x, c, ada_w, ada_b, norm1_g, norm2_g, w_in, m_conv_w, m_gate_b, m_norm_g, a_qnorm_g,
           a_knorm_g, a_lambda, a_subln_g, w_branch_m, w_branch_a, w_out, rg_w, rg_b, re_w, re_b,
           e_w_gate, e_w_up, e_w_down):
    B, S, D = x.shape
    assert B == 1
    depth = ada_w.shape[0]
    Wm = w_branch_m.shape[1]
    Wa = w_branch_a.shape[1]
    qk_w = m_conv_w.shape[2]
    n_gate = 4 * M_HEADS
    hd = a_qnorm_g.shape[1]
    n_experts = re_w.shape[2]
    n_route = N_GROUPS + n_experts
    assert Wa // A_HEADS == LANES and 2 * hd == LANES
    c_mv, c_mo, c_aq = qk_w, qk_w + Wm, qk_w + 2 * Wm
    c_ak, c_av, c_gm = c_aq + Wa, c_aq + 2 * Wa, c_aq + 3 * Wa
    c_ga = c_gm + D
    g0 = qk_w + Wm

    w_main = jnp.concatenate([w_in[:, :, :g0], w_in[:, :, g0 + n_gate:]], axis=2).astype(BF16)
    w_gate = jnp.pad(w_in[:, :, g0:g0 + n_gate], ((0, 0), (0, 0), (0, LANES - n_gate))).astype(BF16)
    w_bm, w_ba, w_o = w_branch_m.astype(BF16), w_branch_a.astype(BF16), w_out.astype(BF16)
    w_r = jnp.pad(jnp.concatenate([rg_w, re_w], axis=2), ((0, 0), (0, 0), (0, LANES - n_route)))
    w_r_hi = w_r.astype(BF16)
    w_router = jnp.concatenate([w_r_hi, (w_r - w_r_hi.astype(F32)).astype(BF16)], axis=2)
    b_router = jnp.pad(jnp.concatenate([rg_b, re_b], axis=1), ((0, 0), (0, LANES - n_route)))[:, None]
    gate_b = jnp.pad(m_gate_b.reshape(depth, 1, n_gate), ((0, 0), (0, 0), (0, LANES - n_gate)))
    gate_b_t = m_gate_b.reshape(depth, n_gate, 1)
    qg = jnp.tile(a_qnorm_g, (1, Wa // hd))[:, None]
    kg = jnp.tile(a_knorm_g, (1, Wa // hd))[:, None]
    n1, n2, mng = norm1_g[:, None], norm2_g[:, None], m_norm_g[:, None]
    subln = a_subln_g[:, :, None]

    xs = x[0]
    mod = _modulation(c, ada_w, ada_b)
    cos_t, sin_t = _rope_tables(S, hd)
    tm_moe = min(MOE_TILE, S)

    for l in range(depth):
        lam_init = 0.8 - 0.6 * math.exp(-0.3 * l)
        z, gates, gates_t = _in_projection(xs, n1, mod, w_main, w_gate, layer=l)
        qk = _qk_conv(z, m_conv_w, float((qk_w // (2 * M_HEADS)) ** -0.5), layer=l)
        h_f, h_b = _mlstm(qk, z, gates, gates_t, gate_b, gate_b_t, layer=l, width=Wm, v_blk=c_mv // Wm)
        q_t, kn, v_t = _qk_prep(z, cos_t, sin_t, qg, kg, layer=l, q_blk=c_aq // Wa, k_blk=c_ak // Wa,
                                v_blk=c_av // Wa, hd=hd, q_scale=float(hd ** -0.5 * math.log2(math.e)))
        y_a = _flash_diff_attention(q_t, kn, v_t, a_lambda, subln, layer=l, lam_init=lam_init)
        merged = _merge(h_f, h_b, mng, y_a, w_bm, w_ba, z, layer=l, mo_blk=c_mo // Wm,
                        gm_col=c_gm, ga_col=c_ga)
        x1, h2, route, counts = _out_projection(merged, w_o, xs, mod, n2, w_router, b_router, layer=l)
        tile_expert, n_active, codes, fill, next_expert, expert_ord = _dispatch_plan(
            route, counts, n_experts, tm_moe)
        x_sorted = _dispatch(h2, codes, fill, n_experts=n_experts, tm=tm_moe)
        y_sorted = _experts(x_sorted, tile_expert, n_active, next_expert, expert_ord,
                            e_w_gate, e_w_up, e_w_down, layer=l, tm=tm_moe)
        xs = _combine(y_sorted, codes, fill, x1, mod, route, layer=l, n_experts=n_experts, tm=tm_moe)
    return xs[None]
```

```python
import functools
import math

import jax
import jax.numpy as jnp
from jax import lax
from jax.experimental import pallas as pl
from jax.experimental.pallas import tpu as pltpu

F32 = jnp.float32
BF16 = jnp.bfloat16

M_HEADS = 4
A_HEADS = 8
CONV_WIDTH = 5
ROPE_THETA = 500000.0
N_GROUPS = 4
EXPERTS_PER_GROUP = 8
EPS = 1e-6
NEG = -1e30
LANES = 128
SUBLANES = 8
BF16_SUBLANES = 16
VMEM_LIMIT = 56 * 1024 * 1024
MOE_TILE = 256
RANK_RADIX = 65536


def _sigmoid(x):
    return 1.0 / (1.0 + jnp.exp(-x))


def _log_sigmoid(x):
    return jnp.minimum(x, 0.0) - jnp.log(1.0 + jnp.exp(-jnp.abs(x)))


def _params(*sem):
    return pltpu.CompilerParams(dimension_semantics=sem, vmem_limit_bytes=VMEM_LIMIT)


def _layer_vec(layer, width, blk=0):
    return pl.BlockSpec((None, 1, width), lambda *_: (layer, 0, blk))


def _mod_kernel(c_ref, w_ref, b_ref, o_ref):
    c = c_ref[...]
    ca = c * _sigmoid(c)
    o_ref[...] = jnp.sum(ca * w_ref[...], axis=0, keepdims=True) + b_ref[...]


def _modulation(c, ada_w, ada_b):
    L, D, N = ada_w.shape
    tn = min(1024, N)
    return pl.pallas_call(
        _mod_kernel,
        grid=(L, N // tn),
        in_specs=[pl.BlockSpec((D, 1), lambda l, j: (0, 0)),
                  pl.BlockSpec((None, D, tn), lambda l, j: (l, 0, j)),
                  pl.BlockSpec((None, 1, tn), lambda l, j: (l, 0, j))],
        out_specs=pl.BlockSpec((None, 1, tn), lambda l, j: (l, 0, j)),
        out_shape=jax.ShapeDtypeStruct((L, 1, N), F32),
        compiler_params=_params("arbitrary", "arbitrary"),
        name="adaln_mod",
    )(c.reshape(D, 1), ada_w, ada_b.reshape(L, 1, N))


def _inproj_kernel(x_ref, g_ref, sc_ref, sh_ref, wa_ref, wb_ref, wg_ref, z_ref, gate_ref, gate_t_ref,
                   h_scr, *, n_a):
    j = pl.program_id(1)

    @pl.when(j == 0)
    def _():
        x = x_ref[...]
        ms = jnp.mean(x * x, axis=-1, keepdims=True)
        h = x * lax.rsqrt(ms + EPS) * g_ref[...] * (1.0 + sc_ref[...]) + sh_ref[...]
        hb = h.astype(BF16)
        h_scr[...] = hb
        g = jnp.dot(hb, wg_ref[...], preferred_element_type=F32)
        gate_ref[...] = g
        gate_t_ref[...] = g.T

    @pl.when(j < n_a)
    def _():
        z_ref[...] = jnp.dot(h_scr[...], wa_ref[...], preferred_element_type=F32).astype(BF16)

    @pl.when(j >= n_a)
    def _():
        z_ref[...] = jnp.dot(h_scr[...], wb_ref[...], preferred_element_type=F32).astype(BF16)


def _in_projection(x, norm_g, mod, w_a, w_b, w_gate, *, layer):
    S, D = x.shape
    tm, tn = min(1024, S), 1024
    n_a, n_b = w_a.shape[2] // tn, w_b.shape[2] // tn
    N = (n_a + n_b) * tn
    return pl.pallas_call(
        functools.partial(_inproj_kernel, n_a=n_a),
        grid=(S // tm, N // tn),
        in_specs=[pl.BlockSpec((tm, D), lambda i, j: (i, 0)),
                  _layer_vec(layer, D), _layer_vec(layer, D, 1), _layer_vec(layer, D, 0),
                  pl.BlockSpec((None, D, tn), lambda i, j: (layer, 0, jnp.minimum(j, n_a - 1))),
                  pl.BlockSpec((None, D, tn), lambda i, j: (layer, 0, jnp.maximum(j - n_a, 0))),
                  pl.BlockSpec((None, D, LANES), lambda i, j: (layer, 0, 0))],
        out_specs=[pl.BlockSpec((tm, tn), lambda i, j: (i, j)),
                   pl.BlockSpec((tm, LANES), lambda i, j: (i, 0)),
                   pl.BlockSpec((LANES, tm), lambda i, j: (0, i))],
        out_shape=[jax.ShapeDtypeStruct((S, N), BF16), jax.ShapeDtypeStruct((S, LANES), F32),
                   jax.ShapeDtypeStruct((LANES, S), F32)],
        scratch_shapes=[pltpu.VMEM((tm, D), BF16)],
        compiler_params=_params("arbitrary", "arbitrary"),
        name="norm_inproj",
    )(x, norm_g, mod, mod, w_a, w_b, w_gate)


def _conv_kernel(prev_ref, cur_ref, next_ref, w_ref, o_ref, ext_scr, *, ts, q_scale):
    i = pl.program_id(0)
    halo = BF16_SUBLANES
    pad = CONV_WIDTH // 2
    ext_scr[0:halo, :] = jnp.where(i > 0, prev_ref[...].astype(F32), 0.0)
    ext_scr[halo:halo + ts, :] = cur_ref[...].astype(F32)
    ext_scr[halo + ts:2 * halo + ts, :] = jnp.where(i < pl.num_programs(0) - 1,
                                                    next_ref[...].astype(F32), 0.0)
    acc = w_ref[0:1, :] * ext_scr[pl.ds(halo - pad, ts), :]
    for k in range(1, CONV_WIDTH):
        acc = acc + w_ref[k:k + 1, :] * ext_scr[pl.ds(halo - pad + k, ts), :]
    y = acc * _sigmoid(acc)
    C = y.shape[1]
    col = lax.broadcasted_iota(jnp.int32, (1, C), 1)
    o_ref[...] = (y * jnp.where(col < C // 2, q_scale, 1.0)).astype(BF16)


def _qk_conv(z, conv_w, q_scale, *, layer):
    S = z.shape[0]
    C = conv_w.shape[2]
    ts = min(512, S)
    halo = BF16_SUBLANES
    r = ts // halo
    nblk = S // halo
    return pl.pallas_call(
        functools.partial(_conv_kernel, ts=ts, q_scale=q_scale),
        grid=(S // ts,),
        in_specs=[pl.BlockSpec((halo, C), lambda i: (jnp.maximum(i * r - 1, 0), 0)),
                  pl.BlockSpec((ts, C), lambda i: (i, 0)),
                  pl.BlockSpec((halo, C), lambda i: (jnp.minimum((i + 1) * r, nblk - 1), 0)),
                  pl.BlockSpec((None, CONV_WIDTH, C), lambda i: (layer, 0, 0))],
        out_specs=pl.BlockSpec((ts, C), lambda i: (i, 0)),
        out_shape=jax.ShapeDtypeStruct((S, C), BF16),
        scratch_shapes=[pltpu.VMEM((ts + 2 * halo, C), F32)],
        compiler_params=_params("arbitrary"),
        name="qk_conv",
    )(z, z, z, conv_w)


def _mlstm_kernel(qk_f, v_f, g_f, gt_f, qk_b, v_b, g_b, gt_b, gb_ref, gbt_ref, hf_ref, hb_ref,
                  C_scr, n_scr, m_scr, *, dk, dv, L):
    @pl.when(pl.program_id(0) == 0)
    def _():
        C_scr[...] = jnp.zeros_like(C_scr)
        n_scr[...] = jnp.zeros_like(n_scr)
        m_scr[...] = jnp.zeros_like(m_scr)

    row = lax.broadcasted_iota(jnp.int32, (L, L), 0)
    col = lax.broadcasted_iota(jnp.int32, (L, L), 1)
    for reverse, refs in ((False, (qk_f, v_f, g_f, gt_f, hf_ref)), (True, (qk_b, v_b, g_b, gt_b, hb_ref))):
        _mlstm_direction(reverse, *refs, gb_ref, gbt_ref, C_scr, n_scr, m_scr, row, col, dk=dk, dv=dv)


def _mlstm_direction(reverse, qk_ref, v_ref, g_ref, gt_ref, h_ref, gb_ref, gbt_ref, C_scr, n_scr, m_scr,
                     row, col, *, dk, dv):
    H = M_HEADS
    gates = g_ref[...] + gb_ref[...]
    gates_t = gt_ref[...] + gbt_ref[...]
    mask = (col >= row) if reverse else (col <= row)
    mask_t = (row >= col) if reverse else (row <= col)
    nt = (((1,), (1,)), ((), ()))
    tn = (((0,), (0,)), ((), ()))

    for h in range(H):
        gi = (2 * H if reverse else 0) + h
        gf = gi + H
        si = (H if reverse else 0) + h
        i_col = gates[:, gi:gi + 1]
        f_col = _log_sigmoid(gates[:, gf:gf + 1])
        i_row = gates_t[gi:gi + 1, :]
        f_row = _log_sigmoid(gates_t[gf:gf + 1, :])
        b_col = jnp.sum(jnp.where(mask, f_row, 0.0), axis=1, keepdims=True)
        b_row = jnp.sum(jnp.where(mask_t, f_col, 0.0), axis=0, keepdims=True)
        b_end = jnp.sum(f_row, axis=1, keepdims=True)
        m_prev = m_scr[si][:, 0:1]

        dmat = jnp.where(mask, b_col - b_row + i_row, NEG)
        m_inter = b_col + m_prev
        m_t = jnp.maximum(m_inter, jnp.max(dmat, axis=1, keepdims=True))
        q = qk_ref[:, h * dk:(h + 1) * dk]
        k = qk_ref[:, (H + h) * dk:(H + h + 1) * dk]
        v = v_ref[:, h * dv:(h + 1) * dv]
        s = lax.dot_general(q, k, nt, preferred_element_type=F32) * jnp.exp(dmat - m_t)
        inter = jnp.exp(m_inter - m_t)
        c_prev = C_scr[si]
        n_prev = n_scr[si]
        num = (inter * jnp.dot(q, c_prev.astype(BF16), preferred_element_type=F32)
               + jnp.dot(s.astype(BF16), v, preferred_element_type=F32))
        den = (inter * jnp.sum(q.astype(F32) * n_prev, axis=1, keepdims=True)
               + jnp.sum(s, axis=1, keepdims=True))
        hout = num / jnp.maximum(jnp.abs(den), jnp.exp(-m_t))

        g_col = b_end - b_col + i_col
        m_new = jnp.maximum(b_end + m_prev, jnp.max(g_col, axis=0, keepdims=True))
        w_col = jnp.exp(g_col - m_new)
        decay = jnp.exp(b_end + m_prev - m_new)
        kw = k.astype(F32) * w_col
        C_scr[si] = decay * c_prev + lax.dot_general(kw.astype(BF16), v, tn, preferred_element_type=F32)
        n_scr[si] = decay * n_prev + jnp.sum(kw, axis=0, keepdims=True)
        m_scr[si] = jnp.broadcast_to(m_new, (1, LANES))
        h_ref[:, h * dv:(h + 1) * dv] = hout


def _mlstm(qk, z, gates, gates_t, gate_b, gate_b_t, *, layer, width, v_blk):
    S = qk.shape[0]
    H = M_HEADS
    dk = qk.shape[1] // (2 * H)
    W = width
    dv = W // H
    L = min(256, S)
    nc = S // L

    def specs(cmap):
        return [pl.BlockSpec((L, 2 * H * dk), lambda c: (cmap(c), 0)),
                pl.BlockSpec((L, W), lambda c: (cmap(c), v_blk)),
                pl.BlockSpec((L, LANES), lambda c: (cmap(c), 0)),
                pl.BlockSpec((4 * H, L), lambda c: (0, cmap(c)))]

    fwd = lambda c: c
    bwd = lambda c: nc - 1 - c
    return pl.pallas_call(
        functools.partial(_mlstm_kernel, dk=dk, dv=dv, L=L),
        grid=(nc,),
        in_specs=specs(fwd) + specs(bwd) + [_layer_vec(layer, LANES),
                                            pl.BlockSpec((None, 4 * H, 1), lambda c: (layer, 0, 0))],
        out_specs=[pl.BlockSpec((L, W), lambda c: (c, 0)), pl.BlockSpec((L, W), lambda c: (bwd(c), 0))],
        out_shape=[jax.ShapeDtypeStruct((S, W), F32), jax.ShapeDtypeStruct((S, W), F32)],
        scratch_shapes=[pltpu.VMEM((2 * H, dk, dv), F32), pltpu.VMEM((2 * H, 1, dk), F32),
                        pltpu.VMEM((2 * H, 1, LANES), F32)],
        compiler_params=_params("arbitrary"),
        name="mlstm_bidir",
    )(qk, z, gates, gates_t, qk, z, gates, gates_t, gate_b, gate_b_t)


def _qkprep_kernel(q_ref, k_ref, v_ref, cos_ref, sin_ref, qg_ref, kg_ref, qt_ref, kn_ref, vt_ref,
                   *, hd, q_scale):
    ts, W = q_ref.shape
    lane = lax.broadcasted_iota(jnp.int32, (ts, LANES), 1)
    lo = lane < hd
    first_half = (lane % hd) < (hd // 8)
    row_lo = lax.broadcasted_iota(jnp.int32, (LANES, ts), 0) < hd
    cos = cos_ref[...]
    sin = sin_ref[...]

    def norm_rope(x, g):
        sq = x * x
        s_lo = jnp.sum(jnp.where(lo, sq, 0.0), axis=1, keepdims=True)
        s_hi = jnp.sum(jnp.where(lo, 0.0, sq), axis=1, keepdims=True)
        inv = lax.rsqrt(jnp.where(lo, s_lo, s_hi) * (1.0 / hd) + EPS)
        xn = x * inv * g
        up = pltpu.roll(xn, LANES - hd // 8, axis=1)
        dn = pltpu.roll(xn, hd // 8, axis=1)
        return xn * cos + jnp.where(first_half, up, dn) * sin

    for j in range(W // LANES):
        sl = slice(j * LANES, (j + 1) * LANES)
        qn = norm_rope(q_ref[:, sl].astype(F32), qg_ref[:, sl]) * q_scale
        qn_t = qn.T
        qt_ref[0, sl, :] = jnp.where(row_lo, qn_t, 0.0).astype(BF16)
        qt_ref[1, sl, :] = jnp.where(row_lo, 0.0, qn_t).astype(BF16)
        kn_ref[:, sl] = norm_rope(k_ref[:, sl].astype(F32), kg_ref[:, sl]).astype(BF16)
        r0 = j * (LANES + BF16_SUBLANES)
        vt_ref[r0:r0 + LANES, :] = v_ref[:, sl].astype(F32).T.astype(BF16)
        vt_ref[r0 + LANES:r0 + LANES + BF16_SUBLANES, :] = jnp.where(
            lax.broadcasted_iota(jnp.int32, (BF16_SUBLANES, ts), 0) == 0, 1.0, 0.0).astype(BF16)


def _qk_prep(z, cos_t, sin_t, qg, kg, *, layer, q_blk, k_blk, v_blk, hd, q_scale):
    S = z.shape[0]
    W = qg.shape[2]
    ts = min(512, S)
    n_slab = W // LANES
    return pl.pallas_call(
        functools.partial(_qkprep_kernel, hd=hd, q_scale=q_scale),
        grid=(S // ts,),
        in_specs=[pl.BlockSpec((ts, W), lambda i: (i, q_blk)),
                  pl.BlockSpec((ts, W), lambda i: (i, k_blk)),
                  pl.BlockSpec((ts, W), lambda i: (i, v_blk)),
                  pl.BlockSpec((ts, LANES), lambda i: (i, 0)),
                  pl.BlockSpec((ts, LANES), lambda i: (i, 0)),
                  _layer_vec(layer, W), _layer_vec(layer, W)],
        out_specs=[pl.BlockSpec((2, W, ts), lambda i: (0, 0, i)),
                   pl.BlockSpec((ts, W), lambda i: (i, 0)),
                   pl.BlockSpec((W + n_slab * BF16_SUBLANES, ts), lambda i: (0, i))],
        out_shape=[jax.ShapeDtypeStruct((2, W, S), BF16), jax.ShapeDtypeStruct((S, W), BF16),
                   jax.ShapeDtypeStruct((W + n_slab * BF16_SUBLANES, S), BF16)],
        compiler_params=_params("arbitrary"),
        name="attn_qk_prep",
    )(z, z, z, cos_t, sin_t, qg, kg)


ATTN_COL_BLOCK = 256
ATTN_KEY_BLOCK = 256
ATTN_COL_GROUP = 8
ATTN_LOOKAHEAD = 8


def _flash_kernel(qa_ref, qb_ref, k_ref, vt_ref, lam_ref, sg_ref, o_ref, q_scr, m_scr, acc_scr,
                  *, tq, tk, lam_init):
    kj = pl.program_id(2)
    cb_w, kb_w = ATTN_COL_BLOCK, min(ATTN_KEY_BLOCK, tk)
    dv = o_ref.shape[1]

    @pl.when(kj == 0)
    def _():
        q_scr[:, 0:tq] = qa_ref[0]
        q_scr[:, tq:2 * tq] = qb_ref[0]
        m_scr[...] = jnp.full_like(m_scr, NEG)
        acc_scr[...] = jnp.zeros_like(acc_scr)

    group = min(ATTN_COL_GROUP, 2 * tq // cb_w)
    steps = [(c, kb) for kb in range(tk // kb_w) for c in range(group)]

    def col_group(gi, carry):
        cols = [pl.ds(pl.multiple_of((gi * group + c) * cb_w, cb_w), cb_w) for c in range(group)]
        qts = [q_scr[:, cs] for cs in cols]
        state = [(m_scr[:, cs], acc_scr[:, cs]) for cs in cols]

        def scores(c, kb):
            return jnp.dot(k_ref[kb * kb_w:(kb + 1) * kb_w, :], qts[c], preferred_element_type=F32)

        pending = [scores(*st) for st in steps[:ATTN_LOOKAHEAD]]
        for n, (c, kb) in enumerate(steps):
            s = pending.pop(0)
            if n + ATTN_LOOKAHEAD < len(steps):
                pending.append(scores(*steps[n + ATTN_LOOKAHEAD]))
            m, acc = state[c]
            m_new = jnp.maximum(m, jnp.max(s, axis=0, keepdims=True))
            alpha = jnp.exp2(m - m_new)
            p = jnp.exp2((s - m_new).astype(BF16))
            acc = alpha * acc + jnp.dot(vt_ref[:, kb * kb_w:(kb + 1) * kb_w], p,
                                        preferred_element_type=F32)
            state[c] = (m_new, acc)
        for cs, (m, acc) in zip(cols, state):
            m_scr[:, cs], acc_scr[:, cs] = m, acc
        return carry

    lax.fori_loop(0, 2 * tq // (cb_w * group), col_group, 0)

    @pl.when(kj == pl.num_programs(2) - 1)
    def _():
        lam = lam_ref[...]
        lam_full = (jnp.exp(jnp.sum(lam[0:1] * lam[1:2], axis=1, keepdims=True))
                    - jnp.exp(jnp.sum(lam[2:3] * lam[3:4], axis=1, keepdims=True)) + lam_init)
        a = acc_scr[0:dv, :] / acc_scr[dv:dv + 1, :]
        o_t = a[:, 0:tq] - lam_full * a[:, tq:2 * tq]
        ms = jnp.mean(o_t * o_t, axis=0, keepdims=True)
        y_t = o_t * lax.rsqrt(ms + EPS) * sg_ref[...] * (1.0 - lam_init)
        o_ref[...] = y_t.T.astype(BF16)


def _flash_diff_attention(q_t, kn, v_t, lam, subln_g, *, layer, lam_init):
    _, W, S = q_t.shape
    H = A_HEADS
    dv = W // H
    tq = min(2048, S)
    tk = min(1024, S)
    return pl.pallas_call(
        functools.partial(_flash_kernel, tq=tq, tk=tk, lam_init=lam_init),
        grid=(H, S // tq, S // tk),
        in_specs=[pl.BlockSpec((1, dv, tq), lambda h, i, j: (0, h, i)),
                  pl.BlockSpec((1, dv, tq), lambda h, i, j: (1, h, i)),
                  pl.BlockSpec((tk, dv), lambda h, i, j: (j, h)),
                  pl.BlockSpec((dv + BF16_SUBLANES, tk), lambda h, i, j: (h, j)),
                  pl.BlockSpec((None,) + lam.shape[1:], lambda h, i, j: (layer, 0, 0)),
                  pl.BlockSpec((None, dv, 1), lambda h, i, j: (layer, 0, 0))],
        out_specs=pl.BlockSpec((tq, dv), lambda h, i, j: (i, h)),
        out_shape=jax.ShapeDtypeStruct((S, W), BF16),
        scratch_shapes=[pltpu.VMEM((dv, 2 * tq), BF16), pltpu.VMEM((1, 2 * tq), F32),
                        pltpu.VMEM((dv + BF16_SUBLANES, 2 * tq), F32)],
        compiler_params=_params("arbitrary", "arbitrary", "arbitrary"),
        name="flash_diff_attn",
    )(q_t, q_t, kn, v_t, lam, subln_g)


def _merge_kernel(hf_ref, hb_ref, mo_ref, ng_ref, ya_ref, wm_ref, wa_ref, gm_ref, ga_ref, o_ref, ym_scr):
    @pl.when(pl.program_id(1) == 0)
    def _():
        dv = hf_ref.shape[1] // M_HEADS
        for h in range(M_HEADS):
            sl = slice(h * dv, (h + 1) * dv)
            hs = hf_ref[:, sl] + hb_ref[:, sl]
            ms = jnp.mean(hs * hs, axis=-1, keepdims=True)
            y = hs * lax.rsqrt(ms + EPS) * ng_ref[:, sl] * _sigmoid(mo_ref[:, sl].astype(F32))
            ym_scr[:, sl] = y.astype(BF16)

    a = jnp.dot(ym_scr[...], wm_ref[...], preferred_element_type=F32)
    b = jnp.dot(ya_ref[...], wa_ref[...], preferred_element_type=F32)
    o_ref[...] = (_sigmoid(gm_ref[...].astype(F32)) * a + _sigmoid(ga_ref[...].astype(F32)) * b).astype(BF16)


def _merge(h_f, h_b, norm_g, y_a, w_m, w_a, z, *, layer, mo_blk, gm_col, ga_col):
    S, W = y_a.shape
    D = w_m.shape[2]
    tm, tn = min(512, S), 1024
    rows = pl.BlockSpec((tm, W), lambda i, j: (i, 0))
    return pl.pallas_call(
        _merge_kernel,
        grid=(S // tm, D // tn),
        in_specs=[rows, rows, pl.BlockSpec((tm, W), lambda i, j: (i, mo_blk)), _layer_vec(layer, W), rows,
                  pl.BlockSpec((None, W, tn), lambda i, j: (layer, 0, j)),
                  pl.BlockSpec((None, W, tn), lambda i, j: (layer, 0, j)),
                  pl.BlockSpec((tm, tn), lambda i, j: (i, gm_col // tn + j)),
                  pl.BlockSpec((tm, tn), lambda i, j: (i, ga_col // tn + j))],
        out_specs=pl.BlockSpec((tm, tn), lambda i, j: (i, j)),
        out_shape=jax.ShapeDtypeStruct((S, D), BF16),
        scratch_shapes=[pltpu.VMEM((tm, W), BF16)],
        compiler_params=_params("arbitrary", "arbitrary"),
        name="branch_merge",
    )(h_f, h_b, z, norm_g, y_a, w_m, w_a, z, z)


def _route(logits):
    G, E = N_GROUPS, EXPERTS_PER_GROUP
    lane = lax.broadcasted_iota(jnp.int32, logits.shape, 1)
    lanef = lane.astype(F32)
    is_g = lane < G
    gl = jnp.where(is_g, logits, NEG)
    ge = jnp.where(is_g, jnp.exp(gl - jnp.max(gl, axis=1, keepdims=True)), 0.0)
    probs = ge / jnp.sum(ge, axis=1, keepdims=True)
    g_val = jnp.max(probs, axis=1, keepdims=True)
    g_idx = jnp.min(jnp.where(is_g & (probs == g_val), lanef, float(LANES)), axis=1, keepdims=True)
    lo = G + E * g_idx
    in_grp = (lanef >= lo) & (lanef < lo + E)
    el = jnp.where(in_grp, logits, NEG)
    v1 = jnp.max(el, axis=1, keepdims=True)
    i1 = jnp.min(jnp.where(in_grp & (el == v1), lanef, float(LANES)), axis=1, keepdims=True)
    rest = in_grp & (lanef != i1)
    el2 = jnp.where(rest, logits, NEG)
    v2 = jnp.max(el2, axis=1, keepdims=True)
    i2 = jnp.min(jnp.where(rest & (el2 == v2), lanef, float(LANES)), axis=1, keepdims=True)
    t = jnp.exp(v2 - v1)
    w1 = (1.0 / (1.0 + t)) * g_val
    w2 = (t / (1.0 + t)) * g_val
    return jnp.where(lane == 0, i1 - G, jnp.where(lane == 1, i2 - G,
                     jnp.where(lane == 2, w1, jnp.where(lane == 3, w2, 0.0))))


def _out_kernel(mg_ref, w_ref, x_ref, g1_ref, n2_ref, sc_ref, sh_ref, wr_ref, br_ref,
                x1_ref, h2_ref, r_ref, cnt_ref, cnt_scr):
    tm = x_ref.shape[0]

    @pl.when(pl.program_id(0) == 0)
    def _():
        cnt_scr[...] = jnp.zeros_like(cnt_scr)

    y = jnp.dot(mg_ref[...], w_ref[...], preferred_element_type=F32)
    x1 = x_ref[...] + g1_ref[...] * y
    x1_ref[...] = x1
    ms = jnp.mean(x1 * x1, axis=-1, keepdims=True)
    h2 = x1 * lax.rsqrt(ms + EPS) * n2_ref[...] * (1.0 + sc_ref[...]) + sh_ref[...]
    h2_ref[...] = h2
    h_hi = h2.astype(BF16)
    h_lo = (h2 - h_hi.astype(F32)).astype(BF16)
    t = jnp.dot(h_hi, wr_ref[...], preferred_element_type=F32)
    logits = (t[:, :LANES] + t[:, LANES:]
              + jnp.dot(h_lo, wr_ref[:, :LANES], preferred_element_type=F32) + br_ref[...])
    route = _route(logits)

    lane = lax.broadcasted_iota(jnp.int32, (tm, LANES), 1)
    lanef = lane.astype(F32)
    sel1 = lanef == route[:, 0:1]
    sel2 = lanef == route[:, 1:2]
    onehot = jnp.where(sel1 | sel2, 1.0, 0.0)
    earlier = (lax.broadcasted_iota(jnp.int32, (tm, tm), 0)
               > lax.broadcasted_iota(jnp.int32, (tm, tm), 1))
    before = cnt_scr[...] + jnp.dot(jnp.where(earlier, 1.0, 0.0).astype(BF16), onehot.astype(BF16),
                                    preferred_element_type=F32)
    rank1 = jnp.sum(jnp.where(sel1, before, 0.0), axis=1, keepdims=True)
    rank2 = jnp.sum(jnp.where(sel2, before, 0.0), axis=1, keepdims=True)
    code1 = route[:, 0:1] * float(RANK_RADIX) + rank1
    code2 = route[:, 1:2] * float(RANK_RADIX) + rank2
    r_ref[...] = jnp.where(lane == 4, code1, jnp.where(lane == 5, code2, route))
    cnt = cnt_scr[...] + jnp.sum(onehot, axis=0, keepdims=True)
    cnt_scr[...] = cnt
    cnt_ref[...] = jnp.broadcast_to(cnt, cnt_ref.shape)


def _out_projection(merged, w_out, x, mod, norm2_g, w_router, b_router, *, layer):
    S, D = x.shape
    tm = min(512, S)
    row = pl.BlockSpec((tm, D), lambda i: (i, 0))
    once = pl.Buffered(1)
    return pl.pallas_call(
        _out_kernel,
        grid=(S // tm,),
        in_specs=[row, pl.BlockSpec((None, D, D), lambda i: (layer, 0, 0), pipeline_mode=once), row,
                  _layer_vec(layer, D, 2), _layer_vec(layer, D), _layer_vec(layer, D, 4),
                  _layer_vec(layer, D, 3),
                  pl.BlockSpec((None, D, 2 * LANES), lambda i: (layer, 0, 0), pipeline_mode=once),
                  _layer_vec(layer, LANES)],
        out_specs=[row, row, pl.BlockSpec((tm, LANES), lambda i: (i, 0)),
                   pl.BlockSpec((SUBLANES, LANES), lambda i: (0, 0))],
        out_shape=[jax.ShapeDtypeStruct((S, D), F32), jax.ShapeDtypeStruct((S, D), F32),
                   jax.ShapeDtypeStruct((S, LANES), F32), jax.ShapeDtypeStruct((SUBLANES, LANES), F32)],
        scratch_shapes=[pltpu.VMEM((1, LANES), F32)],
        compiler_params=_params("arbitrary"),
        name="out_proj_router",
    )(merged, w_out, x, mod, norm2_g, mod, mod, w_router, b_router)


def _dispatch_plan(route, counts_f, n_experts, tm):
    T = route.shape[0]
    counts = counts_f[0, :n_experts].astype(jnp.int32)
    padded = ((counts + tm - 1) // tm) * tm
    pad_end = jnp.cumsum(padded)
    pad_start = pad_end - padded
    n_tiles = (2 * T) // tm + n_experts
    n_active = (pad_end[-1] // tm).astype(jnp.int32)
    tile_start = jnp.arange(n_tiles, dtype=jnp.int32) * tm
    tile_expert = jnp.sum(pad_end[None, :] <= jnp.minimum(tile_start, pad_end[-1] - 1)[:, None],
                          axis=1, dtype=jnp.int32)
    codes = route[:, 4:6].T.reshape(-1).astype(jnp.int32)
    tail = ((pad_start + counts) // SUBLANES) * SUBLANES
    fill = jnp.concatenate([tail, (pad_end - tail) // SUBLANES, n_active.reshape(1), pad_start])
    ids = jnp.arange(n_experts, dtype=jnp.int32)
    has_rows = counts > 0
    later = (ids[None, :] > ids[:, None]) & has_rows[None, :]
    next_expert = jnp.min(jnp.where(later, ids[None, :], n_experts), axis=1)
    next_expert = jnp.where(next_expert == n_experts, -1, next_expert).astype(jnp.int32)
    expert_ord = jnp.sum((ids[None, :] < ids[:, None]) & has_rows[None, :], axis=1, dtype=jnp.int32)
    return tile_expert, n_active.reshape(1), codes, fill, next_expert, expert_ord


def _sorted_row(code_ref, fill_ref, idx, n_experts):
    code = code_ref[idx]
    expert = lax.shift_right_logical(code, RANK_RADIX.bit_length() - 1)
    return fill_ref[2 * n_experts + 1 + expert] + jnp.bitwise_and(code, RANK_RADIX - 1)


def _dispatch_kernel(code_ref, fill_ref, h_ref, xs_hbm, zero_scr, sem, *, tm, T, n_experts, n_tiles):
    i = pl.program_id(0)

    @pl.when(i == 0)
    def _():
        zero_scr[...] = jnp.zeros_like(zero_scr)
        zero8 = zero_scr.at[pl.ds(0, SUBLANES), :]
        n_active = fill_ref[2 * n_experts]

        def per_expert(e, total):
            def piece(k, carry):
                row = pl.multiple_of(fill_ref[e] + k * SUBLANES, SUBLANES)
                pltpu.make_async_copy(zero8, xs_hbm.at[pl.ds(row, SUBLANES), :], sem.at[1]).start()
                return carry
            n = fill_ref[n_experts + e]
            lax.fori_loop(0, n, piece, 0)
            return total + n

        pieces = lax.fori_loop(0, n_experts, per_expert, 0)

        def unused_tile(t, carry):
            row = pl.multiple_of(t * tm, tm)
            pltpu.make_async_copy(zero_scr, xs_hbm.at[pl.ds(row, tm), :], sem.at[2]).start()
            return carry

        lax.fori_loop(n_active, n_tiles, unused_tile, 0)

        def wait_piece(k, carry):
            pltpu.make_async_copy(zero8, xs_hbm.at[pl.ds(0, SUBLANES), :], sem.at[1]).wait()
            return carry

        lax.fori_loop(0, pieces, wait_piece, 0)

        def wait_tile(t, carry):
            pltpu.make_async_copy(zero_scr, xs_hbm.at[pl.ds(0, tm), :], sem.at[2]).wait()
            return carry

        lax.fori_loop(n_active, n_tiles, wait_tile, 0)

    def body(r, carry):
        for slot in range(2):
            p = _sorted_row(code_ref, fill_ref, slot * T + i * tm + r, n_experts)
            pltpu.make_async_copy(h_ref.at[pl.ds(r, 1), :], xs_hbm.at[pl.ds(p, 1), :], sem.at[0]).start()
        return carry

    lax.fori_loop(0, tm, body, 0, unroll=8)
    for slot in range(2):
        pltpu.make_async_copy(h_ref, xs_hbm.at[pl.ds(0, tm), :], sem.at[0]).wait()


def _dispatch(h2, codes, fill, *, n_experts, tm):
    T, D = h2.shape
    n_tiles = (2 * T) // tm + n_experts
    grid_spec = pltpu.PrefetchScalarGridSpec(
        num_scalar_prefetch=2,
        grid=(T // tm,),
        in_specs=[pl.BlockSpec((tm, D), lambda i, codes, fill: (i, 0))],
        out_specs=pl.BlockSpec(memory_space=pl.ANY),
        scratch_shapes=[pltpu.VMEM((tm, D), F32), pltpu.SemaphoreType.DMA((3,))],
    )
    return pl.pallas_call(
        functools.partial(_dispatch_kernel, tm=tm, T=T, n_experts=n_experts, n_tiles=n_tiles),
        grid_spec=grid_spec,
        out_shape=jax.ShapeDtypeStruct((n_tiles * tm, D), F32),
        compiler_params=_params("arbitrary"),
        name="moe_dispatch",
    )(codes, fill, h2)


def _expert_kernel(te_ref, na_ref, nx_ref, od_ref, x_ref, wg_hbm, wu_hbm, wd_hbm, o_ref,
                   wg_f, wu_f, wd_f, wg_b, wu_b, wd_b, sem, *, layer):
    i = pl.program_id(0)

    def weight_copies(e, slot):
        return [pltpu.make_async_copy(src.at[layer, e], dst.at[slot], sem.at[slot, k])
                for k, (src, dst) in enumerate(((wg_hbm, wg_f), (wu_hbm, wu_f), (wd_hbm, wd_f)))]

    @pl.when(i >= na_ref[0])
    def _():
        o_ref[...] = jnp.zeros_like(o_ref)

    @pl.when(i < na_ref[0])
    def _():
        e = te_ref[i]

        @pl.when((i == 0) | (e != te_ref[jnp.maximum(i - 1, 0)]))
        def _():
            slot = od_ref[e] % 2

            @pl.when(i == 0)
            def _():
                for cp in weight_copies(e, slot):
                    cp.start()

            for cp in weight_copies(e, slot):
                cp.wait()
            nxt = nx_ref[e]

            @pl.when(nxt >= 0)
            def _():
                for cp in weight_copies(nxt, 1 - slot):
                    cp.start()

            wg_b[...] = wg_f[slot].astype(BF16)
            wu_b[...] = wu_f[slot].astype(BF16)
            wd_b[...] = wd_f[slot].astype(BF16)

        x = x_ref[...].astype(BF16)
        g = jnp.dot(x, wg_b[...], preferred_element_type=F32)
        u = jnp.dot(x, wu_b[...], preferred_element_type=F32)
        a = (g * _sigmoid(g) * u).astype(BF16)
        o_ref[...] = jnp.dot(a, wd_b[...], preferred_element_type=F32)


def _experts(xs, tile_expert, n_active, next_expert, expert_ord, w_gate, w_up, w_down, *, layer, tm):
    D = xs.shape[1]
    Fe = w_gate.shape[-1]
    n_tiles = tile_expert.shape[0]
    hbm = pl.BlockSpec(memory_space=pl.ANY)
    grid_spec = pltpu.PrefetchScalarGridSpec(
        num_scalar_prefetch=4,
        grid=(n_tiles,),
        in_specs=[pl.BlockSpec((tm, D), lambda i, te, na, nx, od: (jnp.minimum(i, na[0] - 1), 0)),
                  hbm, hbm, hbm],
        out_specs=pl.BlockSpec((tm, D), lambda i, te, na, nx, od: (i, 0)),
        scratch_shapes=[pltpu.VMEM((2, D, Fe), F32), pltpu.VMEM((2, D, Fe), F32), pltpu.VMEM((2, Fe, D), F32),
                        pltpu.VMEM((D, Fe), BF16), pltpu.VMEM((D, Fe), BF16), pltpu.VMEM((Fe, D), BF16),
                        pltpu.SemaphoreType.DMA((2, 3))],
    )
    return pl.pallas_call(
        functools.partial(_expert_kernel, layer=layer),
        grid_spec=grid_spec,
        out_shape=jax.ShapeDtypeStruct((n_tiles * tm, D), F32),
        compiler_params=_params("arbitrary"),
        name="routed_experts",
    )(tile_expert, n_active, next_expert, expert_ord, xs, w_gate, w_up, w_down)


def _combine_kernel(code_ref, fill_ref, ys_hbm, x_ref, g2_ref, r_ref, o_ref, ybuf, sem,
                    *, tm, T, n_experts):
    i = pl.program_id(0)

    def issue(blk, slot):
        def body(r, carry):
            for k in range(2):
                p = _sorted_row(code_ref, fill_ref, k * T + blk * tm + r, n_experts)
                pltpu.make_async_copy(ys_hbm.at[pl.ds(p, 1), :], ybuf.at[slot, k, pl.ds(r, 1), :],
                                      sem.at[slot]).start()
            return carry
        lax.fori_loop(0, tm, body, 0, unroll=8)

    @pl.when(i == 0)
    def _():
        issue(0, 0)

    @pl.when(i + 1 < pl.num_programs(0))
    def _():
        issue(i + 1, (i + 1) % 2)

    slot = i % 2
    for k in range(2):
        pltpu.make_async_copy(ys_hbm.at[pl.ds(0, tm), :], ybuf.at[slot, k], sem.at[slot]).wait()
    y = r_ref[:, 2:3] * ybuf[slot, 0] + r_ref[:, 3:4] * ybuf[slot, 1]
    o_ref[...] = x_ref[...] + g2_ref[...] * y


def _combine(y_sorted, codes, fill, x1, mod, route, *, layer, n_experts, tm):
    T, D = x1.shape
    grid_spec = pltpu.PrefetchScalarGridSpec(
        num_scalar_prefetch=2,
        grid=(T // tm,),
        in_specs=[pl.BlockSpec(memory_space=pl.ANY),
                  pl.BlockSpec((tm, D), lambda i, codes, fill: (i, 0)),
                  _layer_vec(layer, D, 5),
                  pl.BlockSpec((tm, LANES), lambda i, codes, fill: (i, 0))],
        out_specs=pl.BlockSpec((tm, D), lambda i, codes, fill: (i, 0)),
        scratch_shapes=[pltpu.VMEM((2, 2, tm, D), F32), pltpu.SemaphoreType.DMA((2,))],
    )
    return pl.pallas_call(
        functools.partial(_combine_kernel, tm=tm, T=T, n_experts=n_experts),
        grid_spec=grid_spec,
        out_shape=jax.ShapeDtypeStruct((T, D), F32),
        compiler_params=_params("arbitrary"),
        name="moe_combine",
    )(codes, fill, y_sorted, x1, mod, route)


def _rope_tables(S, hd):
    rope_dim = hd // 4
    posn = jnp.arange(S, dtype=F32)
    inv = ROPE_THETA ** (-jnp.arange(0, rope_dim, 2, dtype=F32) / rope_dim)
    ang = posn[:, None] * inv[None, :]
    cos, sin = jnp.cos(ang), jnp.sin(ang)
    rest = hd - rope_dim
    cos_t = jnp.concatenate([cos, cos, jnp.ones((S, rest), F32)], axis=1)
    sin_t = jnp.concatenate([-sin, sin, jnp.zeros((S, rest), F32)], axis=1)
    reps = LANES // hd
    return jnp.tile(cos_t, (1, reps)), jnp.tile(sin_t, (1, reps))


def kernel(x, c, ada_w, ada_b, norm1_g, norm2_g, w_in, m_conv_w, m_gate_b, m_norm_g, a_qnorm_g,
           a_knorm_g, a_lambda, a_subln_g, w_branch_m, w_branch_a, w_out, rg_w, rg_b, re_w, re_b,
           e_w_gate, e_w_up, e_w_down):
    B, S, D = x.shape
    assert B == 1
    depth = ada_w.shape[0]
    Wm = w_branch_m.shape[1]
    Wa = w_branch_a.shape[1]
    qk_w = m_conv_w.shape[2]
    n_gate = 4 * M_HEADS
    hd = a_qnorm_g.shape[1]
    n_experts = re_w.shape[2]
    n_route = N_GROUPS + n_experts
    assert Wa // A_HEADS == LANES and 2 * hd == LANES
    c_mv, c_mo, c_aq = qk_w, qk_w + Wm, qk_w + 2 * Wm
    c_ak, c_av, c_gm = c_aq + Wa, c_aq + 2 * Wa, c_aq + 3 * Wa
    c_ga = c_gm + D
    g0 = qk_w + Wm

    w_a, w_b = w_in[:, :, :g0].astype(BF16), w_in[:, :, g0 + n_gate:].astype(BF16)
    w_gate = jnp.pad(w_in[:, :, g0:g0 + n_gate], ((0, 0), (0, 0), (0, LANES - n_gate))).astype(BF16)
    w_bm, w_ba, w_o = w_branch_m.astype(BF16), w_branch_a.astype(BF16), w_out.astype(BF16)
    w_r = jnp.pad(jnp.concatenate([rg_w, re_w], axis=2), ((0, 0), (0, 0), (0, LANES - n_route)))
    w_r_hi = w_r.astype(BF16)
    w_router = jnp.concatenate([w_r_hi, (w_r - w_r_hi.astype(F32)).astype(BF16)], axis=2)
    b_router = jnp.pad(jnp.concatenate([rg_b, re_b], axis=1), ((0, 0), (0, LANES - n_route)))[:, None]
    gate_b = jnp.pad(m_gate_b.reshape(depth, 1, n_gate), ((0, 0), (0, 0), (0, LANES - n_gate)))
    gate_b_t = m_gate_b.reshape(depth, n_gate, 1)
    qg = jnp.tile(a_qnorm_g, (1, Wa // hd))[:, None]
    kg = jnp.tile(a_knorm_g, (1, Wa // hd))[:, None]
    n1, n2, mng = norm1_g[:, None], norm2_g[:, None], m_norm_g[:, None]
    subln = a_subln_g[:, :, None]

    xs = x[0]
    mod = _modulation(c, ada_w, ada_b)
    cos_t, sin_t = _rope_tables(S, hd)
    tm_moe = min(MOE_TILE, S)

    for l in range(depth):
        lam_init = 0.8 - 0.6 * math.exp(-0.3 * l)
        z, gates, gates_t = _in_projection(xs, n1, mod, w_a, w_b, w_gate, layer=l)
        qk = _qk_conv(z, m_conv_w, float((qk_w // (2 * M_HEADS)) ** -0.5), layer=l)
        h_f, h_b = _mlstm(qk, z, gates, gates_t, gate_b, gate_b_t, layer=l, width=Wm, v_blk=c_mv // Wm)
        q_t, kn, v_t = _qk_prep(z, cos_t, sin_t, qg, kg, layer=l, q_blk=c_aq // Wa, k_blk=c_ak // Wa,
                                v_blk=c_av // Wa, hd=hd, q_scale=float(hd ** -0.5 * math.log2(math.e)))
        y_a = _flash_diff_attention(q_t, kn, v_t, a_lambda, subln, layer=l, lam_init=lam_init)
        merged = _merge(h_f, h_b, mng, y_a, w_bm, w_ba, z, layer=l, mo_blk=c_mo // Wm,
                        gm_col=c_gm, ga_col=c_ga)
        x1, h2, route, counts = _out_projection(merged, w_o, xs, mod, n2, w_router, b_router, layer=l)
        tile_expert, n_active, codes, fill, next_expert, expert_ord = _dispatch_plan(
            route, counts, n_experts, tm_moe)
        x_sorted = _dispatch(h2, codes, fill, n_experts=n_experts, tm=tm_moe)
        y_sorted = _experts(x_sorted, tile_expert, n_active, next_expert, expert_ord,
                            e_w_gate, e_w_up, e_w_down, layer=l, tm=tm_moe)
        xs = _combine(y_sorted, codes, fill, x1, mod, route, layer=l, n_experts=n_experts, tm=tm_moe)
    return xs[None]
```

```python
import functools
import math

import jax
import jax.numpy as jnp
from jax import lax
from jax.experimental import pallas as pl
from jax.experimental.pallas import tpu as pltpu

F32 = jnp.float32
BF16 = jnp.bfloat16

M_HEADS = 4
A_HEADS = 8
CONV_WIDTH = 5
ROPE_THETA = 500000.0
N_GROUPS = 4
EXPERTS_PER_GROUP = 8
EPS = 1e-6
NEG = -1e30
LANES = 128
SUBLANES = 8
BF16_SUBLANES = 16
VMEM_LIMIT = 56 * 1024 * 1024
MOE_TILE = 256
RANK_RADIX = 65536


def _sigmoid(x):
    return 1.0 / (1.0 + jnp.exp(-x))


def _log_sigmoid(x):
    return jnp.minimum(x, 0.0) - jnp.log(1.0 + jnp.exp(-jnp.abs(x)))


def _params(*sem):
    return pltpu.CompilerParams(dimension_semantics=sem, vmem_limit_bytes=VMEM_LIMIT)


def _layer_vec(layer, width, blk=0):
    return pl.BlockSpec((None, 1, width), lambda *_: (layer, 0, blk))


def _mod_kernel(c_ref, w_ref, b_ref, o_ref):
    c = c_ref[...]
    ca = c * _sigmoid(c)
    o_ref[...] = jnp.sum(ca * w_ref[...], axis=0, keepdims=True) + b_ref[...]


def _modulation(c, ada_w, ada_b):
    L, D, N = ada_w.shape
    tn = min(1024, N)
    return pl.pallas_call(
        _mod_kernel,
        grid=(L, N // tn),
        in_specs=[pl.BlockSpec((D, 1), lambda l, j: (0, 0)),
                  pl.BlockSpec((None, D, tn), lambda l, j: (l, 0, j)),
                  pl.BlockSpec((None, 1, tn), lambda l, j: (l, 0, j))],
        out_specs=pl.BlockSpec((None, 1, tn), lambda l, j: (l, 0, j)),
        out_shape=jax.ShapeDtypeStruct((L, 1, N), F32),
        compiler_params=_params("arbitrary", "arbitrary"),
        name="adaln_mod",
    )(c.reshape(D, 1), ada_w, ada_b.reshape(L, 1, N))


def _wcast_kernel(main_ref, extra_ref, o_ref, *, n_plain, shift):
    tn = o_ref.shape[1]

    @pl.when(pl.program_id(1) < n_plain)
    def _():
        o_ref[...] = main_ref[...].astype(BF16)

    @pl.when(pl.program_id(1) >= n_plain)
    def _():
        wide = jnp.concatenate([main_ref[...], extra_ref[...]], axis=1)
        o_ref[...] = wide[:, shift:shift + tn].astype(BF16)


def _cast_input_weights(w_in, *, n_plain_cols, shift):
    L, D, n_in = w_in.shape
    tn = 1024
    n_out = (n_in - shift) // tn
    return pl.pallas_call(
        functools.partial(_wcast_kernel, n_plain=n_plain_cols // tn, shift=shift),
        grid=(L, n_out),
        in_specs=[pl.BlockSpec((None, D, tn), lambda l, k: (l, 0, k)),
                  pl.BlockSpec((None, D, LANES), lambda l, k: (l, 0, (k + 1) * (tn // LANES)))],
        out_specs=pl.BlockSpec((None, D, tn), lambda l, k: (l, 0, k)),
        out_shape=jax.ShapeDtypeStruct((L, D, n_out * tn), BF16),
        compiler_params=_params("arbitrary", "arbitrary"),
        name="inproj_weight_cast",
    )(w_in, w_in)


def _inproj_kernel(x_ref, g_ref, sc_ref, sh_ref, w_ref, wg_ref, z_ref, gate_ref, gate_t_ref, h_scr):
    @pl.when(pl.program_id(1) == 0)
    def _():
        x = x_ref[...]
        ms = jnp.mean(x * x, axis=-1, keepdims=True)
        h = x * lax.rsqrt(ms + EPS) * g_ref[...] * (1.0 + sc_ref[...]) + sh_ref[...]
        hb = h.astype(BF16)
        h_scr[...] = hb
        g = jnp.dot(hb, wg_ref[...], preferred_element_type=F32)
        gate_ref[...] = g
        gate_t_ref[...] = g.T

    z_ref[...] = jnp.dot(h_scr[...], w_ref[...], preferred_element_type=F32).astype(BF16)


def _in_projection(x, norm_g, mod, w_main, w_gate, *, layer):
    S, D = x.shape
    N = w_main.shape[2]
    tm, tn = min(1024, S), 1024
    return pl.pallas_call(
        _inproj_kernel,
        grid=(S // tm, N // tn),
        in_specs=[pl.BlockSpec((tm, D), lambda i, j: (i, 0)),
                  _layer_vec(layer, D), _layer_vec(layer, D, 1), _layer_vec(layer, D, 0),
                  pl.BlockSpec((None, D, tn), lambda i, j: (layer, 0, j)),
                  pl.BlockSpec((None, D, LANES), lambda i, j: (layer, 0, 0))],
        out_specs=[pl.BlockSpec((tm, tn), lambda i, j: (i, j)),
                   pl.BlockSpec((tm, LANES), lambda i, j: (i, 0)),
                   pl.BlockSpec((LANES, tm), lambda i, j: (0, i))],
        out_shape=[jax.ShapeDtypeStruct((S, N), BF16), jax.ShapeDtypeStruct((S, LANES), F32),
                   jax.ShapeDtypeStruct((LANES, S), F32)],
        scratch_shapes=[pltpu.VMEM((tm, D), BF16)],
        compiler_params=_params("arbitrary", "arbitrary"),
        name="norm_inproj",
    )(x, norm_g, mod, mod, w_main, w_gate)


def _conv_kernel(prev_ref, cur_ref, next_ref, w_ref, o_ref, ext_scr, *, ts, q_scale):
    i = pl.program_id(0)
    halo = BF16_SUBLANES
    pad = CONV_WIDTH // 2
    ext_scr[0:halo, :] = jnp.where(i > 0, prev_ref[...].astype(F32), 0.0)
    ext_scr[halo:halo + ts, :] = cur_ref[...].astype(F32)
    ext_scr[halo + ts:2 * halo + ts, :] = jnp.where(i < pl.num_programs(0) - 1,
                                                    next_ref[...].astype(F32), 0.0)
    acc = w_ref[0:1, :] * ext_scr[pl.ds(halo - pad, ts), :]
    for k in range(1, CONV_WIDTH):
        acc = acc + w_ref[k:k + 1, :] * ext_scr[pl.ds(halo - pad + k, ts), :]
    y = acc * _sigmoid(acc)
    C = y.shape[1]
    col = lax.broadcasted_iota(jnp.int32, (1, C), 1)
    o_ref[...] = (y * jnp.where(col < C // 2, q_scale, 1.0)).astype(BF16)


def _qk_conv(z, conv_w, q_scale, *, layer):
    S = z.shape[0]
    C = conv_w.shape[2]
    ts = min(512, S)
    halo = BF16_SUBLANES
    r = ts // halo
    nblk = S // halo
    return pl.pallas_call(
        functools.partial(_conv_kernel, ts=ts, q_scale=q_scale),
        grid=(S // ts,),
        in_specs=[pl.BlockSpec((halo, C), lambda i: (jnp.maximum(i * r - 1, 0), 0)),
                  pl.BlockSpec((ts, C), lambda i: (i, 0)),
                  pl.BlockSpec((halo, C), lambda i: (jnp.minimum((i + 1) * r, nblk - 1), 0)),
                  pl.BlockSpec((None, CONV_WIDTH, C), lambda i: (layer, 0, 0))],
        out_specs=pl.BlockSpec((ts, C), lambda i: (i, 0)),
        out_shape=jax.ShapeDtypeStruct((S, C), BF16),
        scratch_shapes=[pltpu.VMEM((ts + 2 * halo, C), F32)],
        compiler_params=_params("arbitrary"),
        name="qk_conv",
    )(z, z, z, conv_w)


def _mlstm_kernel(qk_f, v_f, g_f, gt_f, qk_b, v_b, g_b, gt_b, gb_ref, gbt_ref, hf_ref, hb_ref,
                  C_scr, n_scr, m_scr, *, dk, dv, L):
    @pl.when(pl.program_id(0) == 0)
    def _():
        C_scr[...] = jnp.zeros_like(C_scr)
        n_scr[...] = jnp.zeros_like(n_scr)
        m_scr[...] = jnp.zeros_like(m_scr)

    row = lax.broadcasted_iota(jnp.int32, (L, L), 0)
    col = lax.broadcasted_iota(jnp.int32, (L, L), 1)
    for reverse, refs in ((False, (qk_f, v_f, g_f, gt_f, hf_ref)), (True, (qk_b, v_b, g_b, gt_b, hb_ref))):
        _mlstm_direction(reverse, *refs, gb_ref, gbt_ref, C_scr, n_scr, m_scr, row, col, dk=dk, dv=dv)


def _mlstm_direction(reverse, qk_ref, v_ref, g_ref, gt_ref, h_ref, gb_ref, gbt_ref, C_scr, n_scr, m_scr,
                     row, col, *, dk, dv):
    H = M_HEADS
    gates = g_ref[...] + gb_ref[...]
    gates_t = gt_ref[...] + gbt_ref[...]
    mask = (col >= row) if reverse else (col <= row)
    mask_t = (row >= col) if reverse else (row <= col)
    nt = (((1,), (1,)), ((), ()))
    tn = (((0,), (0,)), ((), ()))

    for h in range(H):
        gi = (2 * H if reverse else 0) + h
        gf = gi + H
        si = (H if reverse else 0) + h
        i_col = gates[:, gi:gi + 1]
        f_col = _log_sigmoid(gates[:, gf:gf + 1])
        i_row = gates_t[gi:gi + 1, :]
        f_row = _log_sigmoid(gates_t[gf:gf + 1, :])
        b_col = jnp.sum(jnp.where(mask, f_row, 0.0), axis=1, keepdims=True)
        b_row = jnp.sum(jnp.where(mask_t, f_col, 0.0), axis=0, keepdims=True)
        b_end = jnp.sum(f_row, axis=1, keepdims=True)
        m_prev = m_scr[si][:, 0:1]

        dmat = jnp.where(mask, b_col - b_row + i_row, NEG)
        m_inter = b_col + m_prev
        m_t = jnp.maximum(m_inter, jnp.max(dmat, axis=1, keepdims=True))
        q = qk_ref[:, h * dk:(h + 1) * dk]
        k = qk_ref[:, (H + h) * dk:(H + h + 1) * dk]
        v = v_ref[:, h * dv:(h + 1) * dv]
        s = lax.dot_general(q, k, nt, preferred_element_type=F32) * jnp.exp(dmat - m_t)
        inter = jnp.exp(m_inter - m_t)
        c_prev = C_scr[si]
        n_prev = n_scr[si]
        num = (inter * jnp.dot(q, c_prev.astype(BF16), preferred_element_type=F32)
               + jnp.dot(s.astype(BF16), v, preferred_element_type=F32))
        den = (inter * jnp.sum(q.astype(F32) * n_prev, axis=1, keepdims=True)
               + jnp.sum(s, axis=1, keepdims=True))
        hout = num / jnp.maximum(jnp.abs(den), jnp.exp(-m_t))

        g_col = b_end - b_col + i_col
        m_new = jnp.maximum(b_end + m_prev, jnp.max(g_col, axis=0, keepdims=True))
        w_col = jnp.exp(g_col - m_new)
        decay = jnp.exp(b_end + m_prev - m_new)
        kw = k.astype(F32) * w_col
        C_scr[si] = decay * c_prev + lax.dot_general(kw.astype(BF16), v, tn, preferred_element_type=F32)
        n_scr[si] = decay * n_prev + jnp.sum(kw, axis=0, keepdims=True)
        m_scr[si] = jnp.broadcast_to(m_new, (1, LANES))
        h_ref[:, h * dv:(h + 1) * dv] = hout


def _mlstm(qk, z, gates, gates_t, gate_b, gate_b_t, *, layer, width, v_blk):
    S = qk.shape[0]
    H = M_HEADS
    dk = qk.shape[1] // (2 * H)
    W = width
    dv = W // H
    L = min(256, S)
    nc = S // L

    def specs(cmap):
        return [pl.BlockSpec((L, 2 * H * dk), lambda c: (cmap(c), 0)),
                pl.BlockSpec((L, W), lambda c: (cmap(c), v_blk)),
                pl.BlockSpec((L, LANES), lambda c: (cmap(c), 0)),
                pl.BlockSpec((4 * H, L), lambda c: (0, cmap(c)))]

    fwd = lambda c: c
    bwd = lambda c: nc - 1 - c
    return pl.pallas_call(
        functools.partial(_mlstm_kernel, dk=dk, dv=dv, L=L),
        grid=(nc,),
        in_specs=specs(fwd) + specs(bwd) + [_layer_vec(layer, LANES),
                                            pl.BlockSpec((None, 4 * H, 1), lambda c: (layer, 0, 0))],
        out_specs=[pl.BlockSpec((L, W), lambda c: (c, 0)), pl.BlockSpec((L, W), lambda c: (bwd(c), 0))],
        out_shape=[jax.ShapeDtypeStruct((S, W), F32), jax.ShapeDtypeStruct((S, W), F32)],
        scratch_shapes=[pltpu.VMEM((2 * H, dk, dv), F32), pltpu.VMEM((2 * H, 1, dk), F32),
                        pltpu.VMEM((2 * H, 1, LANES), F32)],
        compiler_params=_params("arbitrary"),
        name="mlstm_bidir",
    )(qk, z, gates, gates_t, qk, z, gates, gates_t, gate_b, gate_b_t)


def _qkprep_kernel(q_ref, k_ref, v_ref, cos_ref, sin_ref, qg_ref, kg_ref, qt_ref, kn_ref, vt_ref,
                   *, hd, q_scale):
    ts, W = q_ref.shape
    lane = lax.broadcasted_iota(jnp.int32, (ts, LANES), 1)
    lo = lane < hd
    first_half = (lane % hd) < (hd // 8)
    row_lo = lax.broadcasted_iota(jnp.int32, (LANES, ts), 0) < hd
    cos = cos_ref[...]
    sin = sin_ref[...]

    def norm_rope(x, g):
        sq = x * x
        s_lo = jnp.sum(jnp.where(lo, sq, 0.0), axis=1, keepdims=True)
        s_hi = jnp.sum(jnp.where(lo, 0.0, sq), axis=1, keepdims=True)
        inv = lax.rsqrt(jnp.where(lo, s_lo, s_hi) * (1.0 / hd) + EPS)
        xn = x * inv * g
        up = pltpu.roll(xn, LANES - hd // 8, axis=1)
        dn = pltpu.roll(xn, hd // 8, axis=1)
        return xn * cos + jnp.where(first_half, up, dn) * sin

    for j in range(W // LANES):
        sl = slice(j * LANES, (j + 1) * LANES)
        qn = norm_rope(q_ref[:, sl].astype(F32), qg_ref[:, sl]) * q_scale
        qn_t = qn.T
        qt_ref[0, sl, :] = jnp.where(row_lo, qn_t, 0.0).astype(BF16)
        qt_ref[1, sl, :] = jnp.where(row_lo, 0.0, qn_t).astype(BF16)
        kn_ref[:, sl] = norm_rope(k_ref[:, sl].astype(F32), kg_ref[:, sl]).astype(BF16)
        r0 = j * (LANES + BF16_SUBLANES)
        vt_ref[r0:r0 + LANES, :] = v_ref[:, sl].astype(F32).T.astype(BF16)
        vt_ref[r0 + LANES:r0 + LANES + BF16_SUBLANES, :] = jnp.where(
            lax.broadcasted_iota(jnp.int32, (BF16_SUBLANES, ts), 0) == 0, 1.0, 0.0).astype(BF16)


def _qk_prep(z, cos_t, sin_t, qg, kg, *, layer, q_blk, k_blk, v_blk, hd, q_scale):
    S = z.shape[0]
    W = qg.shape[2]
    ts = min(512, S)
    n_slab = W // LANES
    return pl.pallas_call(
        functools.partial(_qkprep_kernel, hd=hd, q_scale=q_scale),
        grid=(S // ts,),
        in_specs=[pl.BlockSpec((ts, W), lambda i: (i, q_blk)),
                  pl.BlockSpec((ts, W), lambda i: (i, k_blk)),
                  pl.BlockSpec((ts, W), lambda i: (i, v_blk)),
                  pl.BlockSpec((ts, LANES), lambda i: (i, 0)),
                  pl.BlockSpec((ts, LANES), lambda i: (i, 0)),
                  _layer_vec(layer, W), _layer_vec(layer, W)],
        out_specs=[pl.BlockSpec((2, W, ts), lambda i: (0, 0, i)),
                   pl.BlockSpec((ts, W), lambda i: (i, 0)),
                   pl.BlockSpec((W + n_slab * BF16_SUBLANES, ts), lambda i: (0, i))],
        out_shape=[jax.ShapeDtypeStruct((2, W, S), BF16), jax.ShapeDtypeStruct((S, W), BF16),
                   jax.ShapeDtypeStruct((W + n_slab * BF16_SUBLANES, S), BF16)],
        compiler_params=_params("arbitrary"),
        name="attn_qk_prep",
    )(z, z, z, cos_t, sin_t, qg, kg)


ATTN_COL_BLOCK = 256
ATTN_KEY_BLOCK = 256
ATTN_COL_GROUP = 8
ATTN_LOOKAHEAD = 8


def _flash_kernel(qa_ref, qb_ref, k_ref, vt_ref, lam_ref, sg_ref, o_ref, q_scr, m_scr, acc_scr,
                  *, tq, tk, lam_init):
    kj = pl.program_id(2)
    cb_w, kb_w = ATTN_COL_BLOCK, min(ATTN_KEY_BLOCK, tk)
    dv = o_ref.shape[1]

    @pl.when(kj == 0)
    def _():
        q_scr[:, 0:tq] = qa_ref[0]
        q_scr[:, tq:2 * tq] = qb_ref[0]
        m_scr[...] = jnp.full_like(m_scr, NEG)
        acc_scr[...] = jnp.zeros_like(acc_scr)

    group = min(ATTN_COL_GROUP, 2 * tq // cb_w)
    steps = [(c, kb) for kb in range(tk // kb_w) for c in range(group)]

    def col_group(gi, carry):
        cols = [pl.ds(pl.multiple_of((gi * group + c) * cb_w, cb_w), cb_w) for c in range(group)]
        qts = [q_scr[:, cs] for cs in cols]
        state = [(m_scr[:, cs], acc_scr[:, cs]) for cs in cols]

        def scores(c, kb):
            return jnp.dot(k_ref[kb * kb_w:(kb + 1) * kb_w, :], qts[c], preferred_element_type=F32)

        pending = [scores(*st) for st in steps[:ATTN_LOOKAHEAD]]
        for n, (c, kb) in enumerate(steps):
            s = pending.pop(0)
            if n + ATTN_LOOKAHEAD < len(steps):
                pending.append(scores(*steps[n + ATTN_LOOKAHEAD]))
            m, acc = state[c]
            m_new = jnp.maximum(m, jnp.max(s, axis=0, keepdims=True))
            alpha = jnp.exp2(m - m_new)
            p = jnp.exp2((s - m_new).astype(BF16))
            acc = alpha * acc + jnp.dot(vt_ref[:, kb * kb_w:(kb + 1) * kb_w], p,
                                        preferred_element_type=F32)
            state[c] = (m_new, acc)
        for cs, (m, acc) in zip(cols, state):
            m_scr[:, cs], acc_scr[:, cs] = m, acc
        return carry

    lax.fori_loop(0, 2 * tq // (cb_w * group), col_group, 0)

    @pl.when(kj == pl.num_programs(2) - 1)
    def _():
        lam = lam_ref[...]
        lam_full = (jnp.exp(jnp.sum(lam[0:1] * lam[1:2], axis=1, keepdims=True))
                    - jnp.exp(jnp.sum(lam[2:3] * lam[3:4], axis=1, keepdims=True)) + lam_init)
        a = acc_scr[0:dv, :] / acc_scr[dv:dv + 1, :]
        o_t = a[:, 0:tq] - lam_full * a[:, tq:2 * tq]
        ms = jnp.mean(o_t * o_t, axis=0, keepdims=True)
        y_t = o_t * lax.rsqrt(ms + EPS) * sg_ref[...] * (1.0 - lam_init)
        o_ref[...] = y_t.T.astype(BF16)


def _flash_diff_attention(q_t, kn, v_t, lam, subln_g, *, layer, lam_init):
    _, W, S = q_t.shape
    H = A_HEADS
    dv = W // H
    tq = min(2048, S)
    tk = min(1024, S)
    return pl.pallas_call(
        functools.partial(_flash_kernel, tq=tq, tk=tk, lam_init=lam_init),
        grid=(H, S // tq, S // tk),
        in_specs=[pl.BlockSpec((1, dv, tq), lambda h, i, j: (0, h, i)),
                  pl.BlockSpec((1, dv, tq), lambda h, i, j: (1, h, i)),
                  pl.BlockSpec((tk, dv), lambda h, i, j: (j, h)),
                  pl.BlockSpec((dv + BF16_SUBLANES, tk), lambda h, i, j: (h, j)),
                  pl.BlockSpec((None,) + lam.shape[1:], lambda h, i, j: (layer, 0, 0)),
                  pl.BlockSpec((None, dv, 1), lambda h, i, j: (layer, 0, 0))],
        out_specs=pl.BlockSpec((tq, dv), lambda h, i, j: (i, h)),
        out_shape=jax.ShapeDtypeStruct((S, W), BF16),
        scratch_shapes=[pltpu.VMEM((dv, 2 * tq), BF16), pltpu.VMEM((1, 2 * tq), F32),
                        pltpu.VMEM((dv + BF16_SUBLANES, 2 * tq), F32)],
        compiler_params=_params("arbitrary", "arbitrary", "arbitrary"),
        name="flash_diff_attn",
    )(q_t, q_t, kn, v_t, lam, subln_g)


def _merge_kernel(hf_ref, hb_ref, mo_ref, ng_ref, ya_ref, wm_ref, wa_ref, gm_ref, ga_ref, o_ref, ym_scr):
    @pl.when(pl.program_id(1) == 0)
    def _():
        dv = hf_ref.shape[1] // M_HEADS
        for h in range(M_HEADS):
            sl = slice(h * dv, (h + 1) * dv)
            hs = hf_ref[:, sl] + hb_ref[:, sl]
            ms = jnp.mean(hs * hs, axis=-1, keepdims=True)
            y = hs * lax.rsqrt(ms + EPS) * ng_ref[:, sl] * _sigmoid(mo_ref[:, sl].astype(F32))
            ym_scr[:, sl] = y.astype(BF16)

    a = jnp.dot(ym_scr[...], wm_ref[...], preferred_element_type=F32)
    b = jnp.dot(ya_ref[...], wa_ref[...], preferred_element_type=F32)
    o_ref[...] = (_sigmoid(gm_ref[...].astype(F32)) * a + _sigmoid(ga_ref[...].astype(F32)) * b).astype(BF16)


def _merge(h_f, h_b, norm_g, y_a, w_m, w_a, z, *, layer, mo_blk, gm_col, ga_col):
    S, W = y_a.shape
    D = w_m.shape[2]
    tm, tn = min(512, S), 1024
    rows = pl.BlockSpec((tm, W), lambda i, j: (i, 0))
    return pl.pallas_call(
        _merge_kernel,
        grid=(S // tm, D // tn),
        in_specs=[rows, rows, pl.BlockSpec((tm, W), lambda i, j: (i, mo_blk)), _layer_vec(layer, W), rows,
                  pl.BlockSpec((None, W, tn), lambda i, j: (layer, 0, j)),
                  pl.BlockSpec((None, W, tn), lambda i, j: (layer, 0, j)),
                  pl.BlockSpec((tm, tn), lambda i, j: (i, gm_col // tn + j)),
                  pl.BlockSpec((tm, tn), lambda i, j: (i, ga_col // tn + j))],
        out_specs=pl.BlockSpec((tm, tn), lambda i, j: (i, j)),
        out_shape=jax.ShapeDtypeStruct((S, D), BF16),
        scratch_shapes=[pltpu.VMEM((tm, W), BF16)],
        compiler_params=_params("arbitrary", "arbitrary"),
        name="branch_merge",
    )(h_f, h_b, z, norm_g, y_a, w_m, w_a, z, z)


def _route(logits):
    G, E = N_GROUPS, EXPERTS_PER_GROUP
    lane = lax.broadcasted_iota(jnp.int32, logits.shape, 1)
    lanef = lane.astype(F32)
    is_g = lane < G
    gl = jnp.where(is_g, logits, NEG)
    ge = jnp.where(is_g, jnp.exp(gl - jnp.max(gl, axis=1, keepdims=True)), 0.0)
    probs = ge / jnp.sum(ge, axis=1, keepdims=True)
    g_val = jnp.max(probs, axis=1, keepdims=True)
    g_idx = jnp.min(jnp.where(is_g & (probs == g_val), lanef, float(LANES)), axis=1, keepdims=True)
    lo = G + E * g_idx
    in_grp = (lanef >= lo) & (lanef < lo + E)
    el = jnp.where(in_grp, logits, NEG)
    v1 = jnp.max(el, axis=1, keepdims=True)
    i1 = jnp.min(jnp.where(in_grp & (el == v1), lanef, float(LANES)), axis=1, keepdims=True)
    rest = in_grp & (lanef != i1)
    el2 = jnp.where(rest, logits, NEG)
    v2 = jnp.max(el2, axis=1, keepdims=True)
    i2 = jnp.min(jnp.where(rest & (el2 == v2), lanef, float(LANES)), axis=1, keepdims=True)
    t = jnp.exp(v2 - v1)
    w1 = (1.0 / (1.0 + t)) * g_val
    w2 = (t / (1.0 + t)) * g_val
    return jnp.where(lane == 0, i1 - G, jnp.where(lane == 1, i2 - G,
                     jnp.where(lane == 2, w1, jnp.where(lane == 3, w2, 0.0))))


def _out_kernel(mg_ref, w_ref, x_ref, g1_ref, n2_ref, sc_ref, sh_ref, wr_ref, br_ref,
                x1_ref, h2_ref, r_ref, cnt_ref, cnt_scr):
    tm = x_ref.shape[0]

    @pl.when(pl.program_id(0) == 0)
    def _():
        cnt_scr[...] = jnp.zeros_like(cnt_scr)

    y = jnp.dot(mg_ref[...], w_ref[...], preferred_element_type=F32)
    x1 = x_ref[...] + g1_ref[...] * y
    x1_ref[...] = x1
    ms = jnp.mean(x1 * x1, axis=-1, keepdims=True)
    h2 = x1 * lax.rsqrt(ms + EPS) * n2_ref[...] * (1.0 + sc_ref[...]) + sh_ref[...]
    h2_ref[...] = h2
    h_hi = h2.astype(BF16)
    h_lo = (h2 - h_hi.astype(F32)).astype(BF16)
    t = jnp.dot(h_hi, wr_ref[...], preferred_element_type=F32)
    logits = (t[:, :LANES] + t[:, LANES:]
              + jnp.dot(h_lo, wr_ref[:, :LANES], preferred_element_type=F32) + br_ref[...])
    route = _route(logits)

    lane = lax.broadcasted_iota(jnp.int32, (tm, LANES), 1)
    lanef = lane.astype(F32)
    sel1 = lanef == route[:, 0:1]
    sel2 = lanef == route[:, 1:2]
    onehot = jnp.where(sel1 | sel2, 1.0, 0.0)
    earlier = (lax.broadcasted_iota(jnp.int32, (tm, tm), 0)
               > lax.broadcasted_iota(jnp.int32, (tm, tm), 1))
    before = cnt_scr[...] + jnp.dot(jnp.where(earlier, 1.0, 0.0).astype(BF16), onehot.astype(BF16),
                                    preferred_element_type=F32)
    rank1 = jnp.sum(jnp.where(sel1, before, 0.0), axis=1, keepdims=True)
    rank2 = jnp.sum(jnp.where(sel2, before, 0.0), axis=1, keepdims=True)
    code1 = route[:, 0:1] * float(RANK_RADIX) + rank1
    code2 = route[:, 1:2] * float(RANK_RADIX) + rank2
    r_ref[...] = jnp.where(lane == 4, code1, jnp.where(lane == 5, code2, route))
    cnt = cnt_scr[...] + jnp.sum(onehot, axis=0, keepdims=True)
    cnt_scr[...] = cnt
    cnt_ref[...] = jnp.broadcast_to(cnt, cnt_ref.shape)


def _out_projection(merged, w_out, x, mod, norm2_g, w_router, b_router, *, layer):
    S, D = x.shape
    tm = min(512, S)
    row = pl.BlockSpec((tm, D), lambda i: (i, 0))
    once = pl.Buffered(1)
    return pl.pallas_call(
        _out_kernel,
        grid=(S // tm,),
        in_specs=[row, pl.BlockSpec((None, D, D), lambda i: (layer, 0, 0), pipeline_mode=once), row,
                  _layer_vec(layer, D, 2), _layer_vec(layer, D), _layer_vec(layer, D, 4),
                  _layer_vec(layer, D, 3),
                  pl.BlockSpec((None, D, 2 * LANES), lambda i: (layer, 0, 0), pipeline_mode=once),
                  _layer_vec(layer, LANES)],
        out_specs=[row, row, pl.BlockSpec((tm, LANES), lambda i: (i, 0)),
                   pl.BlockSpec((SUBLANES, LANES), lambda i: (0, 0))],
        out_shape=[jax.ShapeDtypeStruct((S, D), F32), jax.ShapeDtypeStruct((S, D), F32),
                   jax.ShapeDtypeStruct((S, LANES), F32), jax.ShapeDtypeStruct((SUBLANES, LANES), F32)],
        scratch_shapes=[pltpu.VMEM((1, LANES), F32)],
        compiler_params=_params("arbitrary"),
        name="out_proj_router",
    )(merged, w_out, x, mod, norm2_g, mod, mod, w_router, b_router)


def _dispatch_plan(route, counts_f, n_experts, tm):
    T = route.shape[0]
    counts = counts_f[0, :n_experts].astype(jnp.int32)
    padded = ((counts + tm - 1) // tm) * tm
    pad_end = jnp.cumsum(padded)
    pad_start = pad_end - padded
    n_tiles = (2 * T) // tm + n_experts
    n_active = (pad_end[-1] // tm).astype(jnp.int32)
    tile_start = jnp.arange(n_tiles, dtype=jnp.int32) * tm
    tile_expert = jnp.sum(pad_end[None, :] <= jnp.minimum(tile_start, pad_end[-1] - 1)[:, None],
                          axis=1, dtype=jnp.int32)
    pos = _sorted_positions(route, pad_start)
    tail = ((pad_start + counts) // SUBLANES) * SUBLANES
    fill = jnp.concatenate([tail, (pad_end - tail) // SUBLANES, n_active.reshape(1)])
    ids = jnp.arange(n_experts, dtype=jnp.int32)
    has_rows = counts > 0
    later = (ids[None, :] > ids[:, None]) & has_rows[None, :]
    next_expert = jnp.min(jnp.where(later, ids[None, :], n_experts), axis=1)
    next_expert = jnp.where(next_expert == n_experts, -1, next_expert).astype(jnp.int32)
    expert_ord = jnp.sum((ids[None, :] < ids[:, None]) & has_rows[None, :], axis=1, dtype=jnp.int32)
    return tile_expert, n_active.reshape(1), pos, fill, next_expert, expert_ord


def _positions_kernel(r_ref, start_ref, o_ref):
    r = r_ref[...]
    lane = lax.broadcasted_iota(jnp.int32, r.shape, 1)
    lanef = lane.astype(F32)
    start = start_ref[...]
    pos = []
    for s in range(2):
        e, code = r[:, s:s + 1], r[:, 4 + s:5 + s]
        first = jnp.sum(jnp.where(lanef == e, start, 0.0), axis=1, keepdims=True)
        pos.append(first + (code - e * float(RANK_RADIX)))
    o_ref[...] = jnp.where(lane == 0, pos[0], jnp.where(lane == 1, pos[1], 0.0))


def _sorted_positions(route, pad_start):
    T = route.shape[0]
    tm = min(1024, T)
    start = jnp.pad(pad_start.astype(F32), (0, LANES - pad_start.shape[0]))[None]
    out = pl.pallas_call(
        _positions_kernel,
        grid=(T // tm,),
        in_specs=[pl.BlockSpec((tm, LANES), lambda i: (i, 0)), pl.BlockSpec((1, LANES), lambda i: (0, 0))],
        out_specs=pl.BlockSpec((tm, LANES), lambda i: (i, 0)),
        out_shape=jax.ShapeDtypeStruct((T, LANES), F32),
        compiler_params=_params("arbitrary"),
        name="moe_positions",
    )(route, start)
    return out[:, 0:2].T.reshape(-1).astype(jnp.int32)


def _dispatch_kernel(pos_ref, fill_ref, h_ref, xs_hbm, zero_scr, sem, *, tm, T, n_experts, n_tiles):
    i = pl.program_id(0)

    @pl.when(i == 0)
    def _():
        zero_scr[...] = jnp.zeros_like(zero_scr)
        zero8 = zero_scr.at[pl.ds(0, SUBLANES), :]
        n_active = fill_ref[2 * n_experts]

        def per_expert(e, total):
            def piece(k, carry):
                row = pl.multiple_of(fill_ref[e] + k * SUBLANES, SUBLANES)
                pltpu.make_async_copy(zero8, xs_hbm.at[pl.ds(row, SUBLANES), :], sem.at[1]).start()
                return carry
            n = fill_ref[n_experts + e]
            lax.fori_loop(0, n, piece, 0)
            return total + n

        pieces = lax.fori_loop(0, n_experts, per_expert, 0)

        def unused_tile(t, carry):
            row = pl.multiple_of(t * tm, tm)
            pltpu.make_async_copy(zero_scr, xs_hbm.at[pl.ds(row, tm), :], sem.at[2]).start()
            return carry

        lax.fori_loop(n_active, n_tiles, unused_tile, 0)

        def wait_piece(k, carry):
            pltpu.make_async_copy(zero8, xs_hbm.at[pl.ds(0, SUBLANES), :], sem.at[1]).wait()
            return carry

        lax.fori_loop(0, pieces, wait_piece, 0)

        def wait_tile(t, carry):
            pltpu.make_async_copy(zero_scr, xs_hbm.at[pl.ds(0, tm), :], sem.at[2]).wait()
            return carry

        lax.fori_loop(n_active, n_tiles, wait_tile, 0)

    def body(r, carry):
        for slot in range(2):
            p = pos_ref[slot * T + i * tm + r]
            pltpu.make_async_copy(h_ref.at[pl.ds(r, 1), :], xs_hbm.at[pl.ds(p, 1), :], sem.at[0]).start()
        return carry

    lax.fori_loop(0, tm, body, 0, unroll=8)
    for slot in range(2):
        pltpu.make_async_copy(h_ref, xs_hbm.at[pl.ds(0, tm), :], sem.at[0]).wait()


def _dispatch(h2, pos, fill, *, n_experts, tm):
    T, D = h2.shape
    n_tiles = (2 * T) // tm + n_experts
    grid_spec = pltpu.PrefetchScalarGridSpec(
        num_scalar_prefetch=2,
        grid=(T // tm,),
        in_specs=[pl.BlockSpec((tm, D), lambda i, pos, fill: (i, 0))],
        out_specs=pl.BlockSpec(memory_space=pl.ANY),
        scratch_shapes=[pltpu.VMEM((tm, D), F32), pltpu.SemaphoreType.DMA((3,))],
    )
    return pl.pallas_call(
        functools.partial(_dispatch_kernel, tm=tm, T=T, n_experts=n_experts, n_tiles=n_tiles),
        grid_spec=grid_spec,
        out_shape=jax.ShapeDtypeStruct((n_tiles * tm, D), F32),
        compiler_params=_params("arbitrary"),
        name="moe_dispatch",
    )(pos, fill, h2)


def _expert_kernel(te_ref, na_ref, nx_ref, od_ref, x_ref, wg_hbm, wu_hbm, wd_hbm, o_ref,
                   wg_f, wu_f, wd_f, wg_b, wu_b, wd_b, sem, *, layer):
    i = pl.program_id(0)

    def weight_copies(e, slot):
        return [pltpu.make_async_copy(src.at[layer, e], dst.at[slot], sem.at[slot, k])
                for k, (src, dst) in enumerate(((wg_hbm, wg_f), (wu_hbm, wu_f), (wd_hbm, wd_f)))]

    @pl.when(i >= na_ref[0])
    def _():
        o_ref[...] = jnp.zeros_like(o_ref)

    @pl.when(i < na_ref[0])
    def _():
        e = te_ref[i]

        @pl.when((i == 0) | (e != te_ref[jnp.maximum(i - 1, 0)]))
        def _():
            slot = od_ref[e] % 2

            @pl.when(i == 0)
            def _():
                for cp in weight_copies(e, slot):
                    cp.start()

            for cp in weight_copies(e, slot):
                cp.wait()
            nxt = nx_ref[e]

            @pl.when(nxt >= 0)
            def _():
                for cp in weight_copies(nxt, 1 - slot):
                    cp.start()

            wg_b[...] = wg_f[slot].astype(BF16)
            wu_b[...] = wu_f[slot].astype(BF16)
            wd_b[...] = wd_f[slot].astype(BF16)

        x = x_ref[...].astype(BF16)
        g = jnp.dot(x, wg_b[...], preferred_element_type=F32)
        u = jnp.dot(x, wu_b[...], preferred_element_type=F32)
        a = (g * _sigmoid(g) * u).astype(BF16)
        o_ref[...] = jnp.dot(a, wd_b[...], preferred_element_type=F32)


def _experts(xs, tile_expert, n_active, next_expert, expert_ord, w_gate, w_up, w_down, *, layer, tm):
    D = xs.shape[1]
    Fe = w_gate.shape[-1]
    n_tiles = tile_expert.shape[0]
    hbm = pl.BlockSpec(memory_space=pl.ANY)
    grid_spec = pltpu.PrefetchScalarGridSpec(
        num_scalar_prefetch=4,
        grid=(n_tiles,),
        in_specs=[pl.BlockSpec((tm, D), lambda i, te, na, nx, od: (jnp.minimum(i, na[0] - 1), 0)),
                  hbm, hbm, hbm],
        out_specs=pl.BlockSpec((tm, D), lambda i, te, na, nx, od: (i, 0)),
        scratch_shapes=[pltpu.VMEM((2, D, Fe), F32), pltpu.VMEM((2, D, Fe), F32), pltpu.VMEM((2, Fe, D), F32),
                        pltpu.VMEM((D, Fe), BF16), pltpu.VMEM((D, Fe), BF16), pltpu.VMEM((Fe, D), BF16),
                        pltpu.SemaphoreType.DMA((2, 3))],
    )
    return pl.pallas_call(
        functools.partial(_expert_kernel, layer=layer),
        grid_spec=grid_spec,
        out_shape=jax.ShapeDtypeStruct((n_tiles * tm, D), F32),
        compiler_params=_params("arbitrary"),
        name="routed_experts",
    )(tile_expert, n_active, next_expert, expert_ord, xs, w_gate, w_up, w_down)


def _combine_kernel(pos_ref, ys_hbm, x_ref, g2_ref, r_ref, o_ref, ybuf, sem, *, tm, T):
    i = pl.program_id(0)

    def issue(blk, slot):
        def body(r, carry):
            for k in range(2):
                p = pos_ref[k * T + blk * tm + r]
                pltpu.make_async_copy(ys_hbm.at[pl.ds(p, 1), :], ybuf.at[slot, k, pl.ds(r, 1), :],
                                      sem.at[slot]).start()
            return carry
        lax.fori_loop(0, tm, body, 0, unroll=8)

    @pl.when(i == 0)
    def _():
        issue(0, 0)

    @pl.when(i + 1 < pl.num_programs(0))
    def _():
        issue(i + 1, (i + 1) % 2)

    slot = i % 2
    for k in range(2):
        pltpu.make_async_copy(ys_hbm.at[pl.ds(0, tm), :], ybuf.at[slot, k], sem.at[slot]).wait()
    y = r_ref[:, 2:3] * ybuf[slot, 0] + r_ref[:, 3:4] * ybuf[slot, 1]
    o_ref[...] = x_ref[...] + g2_ref[...] * y


def _combine(y_sorted, pos, x1, mod, route, *, layer, tm):
    T, D = x1.shape
    grid_spec = pltpu.PrefetchScalarGridSpec(
        num_scalar_prefetch=1,
        grid=(T // tm,),
        in_specs=[pl.BlockSpec(memory_space=pl.ANY),
                  pl.BlockSpec((tm, D), lambda i, pos: (i, 0)),
                  _layer_vec(layer, D, 5),
                  pl.BlockSpec((tm, LANES), lambda i, pos: (i, 0))],
        out_specs=pl.BlockSpec((tm, D), lambda i, pos: (i, 0)),
        scratch_shapes=[pltpu.VMEM((2, 2, tm, D), F32), pltpu.SemaphoreType.DMA((2,))],
    )
    return pl.pallas_call(
        functools.partial(_combine_kernel, tm=tm, T=T),
        grid_spec=grid_spec,
        out_shape=jax.ShapeDtypeStruct((T, D), F32),
        compiler_params=_params("arbitrary"),
        name="moe_combine",
    )(pos, y_sorted, x1, mod, route)


def _rope_tables(S, hd):
    rope_dim = hd // 4
    posn = jnp.arange(S, dtype=F32)
    inv = ROPE_THETA ** (-jnp.arange(0, rope_dim, 2, dtype=F32) / rope_dim)
    ang = posn[:, None] * inv[None, :]
    cos, sin = jnp.cos(ang), jnp.sin(ang)
    rest = hd - rope_dim
    cos_t = jnp.concatenate([cos, cos, jnp.ones((S, rest), F32)], axis=1)
    sin_t = jnp.concatenate([-sin, sin, jnp.zeros((S, rest), F32)], axis=1)
    reps = LANES // hd
    return jnp.tile(cos_t, (1, reps)), jnp.tile(sin_t, (1, reps))


def kernel(x, c, ada_w, ada_b, norm1_g, norm2_g, w_in, m_conv_w, m_gate_b, m_norm_g, a_qnorm_g,
           a_knorm_g, a_lambda, a_subln_g, w_branch_m, w_branch_a, w_out, rg_w, rg_b, re_w, re_b,
           e_w_gate, e_w_up, e_w_down):
    B, S, D = x.shape
    assert B == 1
    depth = ada_w.shape[0]
    Wm = w_branch_m.shape[1]
    Wa = w_branch_a.shape[1]
    qk_w = m_conv_w.shape[2]
    n_gate = 4 * M_HEADS
    hd = a_qnorm_g.shape[1]
    n_experts = re_w.shape[2]
    n_route = N_GROUPS + n_experts
    assert Wa // A_HEADS == LANES and 2 * hd == LANES
    c_mv, c_mo, c_aq = qk_w, qk_w + Wm, qk_w + 2 * Wm
    c_ak, c_av, c_gm = c_aq + Wa, c_aq + 2 * Wa, c_aq + 3 * Wa
    c_ga = c_gm + D
    g0 = qk_w + Wm

    w_main = _cast_input_weights(w_in, n_plain_cols=g0, shift=n_gate)
    w_gate = jnp.pad(w_in[:, :, g0:g0 + n_gate], ((0, 0), (0, 0), (0, LANES - n_gate))).astype(BF16)
    w_bm, w_ba, w_o = w_branch_m.astype(BF16), w_branch_a.astype(BF16), w_out.astype(BF16)
    w_r = jnp.pad(jnp.concatenate([rg_w, re_w], axis=2), ((0, 0), (0, 0), (0, LANES - n_route)))
    w_r_hi = w_r.astype(BF16)
    w_router = jnp.concatenate([w_r_hi, (w_r - w_r_hi.astype(F32)).astype(BF16)], axis=2)
    b_router = jnp.pad(jnp.concatenate([rg_b, re_b], axis=1), ((0, 0), (0, LANES - n_route)))[:, None]
    gate_b = jnp.pad(m_gate_b.reshape(depth, 1, n_gate), ((0, 0), (0, 0), (0, LANES - n_gate)))
    gate_b_t = m_gate_b.reshape(depth, n_gate, 1)
    qg = jnp.tile(a_qnorm_g, (1, Wa // hd))[:, None]
    kg = jnp.tile(a_knorm_g, (1, Wa // hd))[:, None]
    n1, n2, mng = norm1_g[:, None], norm2_g[:, None], m_norm_g[:, None]
    subln = a_subln_g[:, :, None]

    xs = x[0]
    mod = _modulation(c, ada_w, ada_b)
    cos_t, sin_t = _rope_tables(S, hd)
    tm_moe = min(MOE_TILE, S)

    for l in range(depth):
        lam_init = 0.8 - 0.6 * math.exp(-0.3 * l)
        z, gates, gates_t = _in_projection(xs, n1, mod, w_main, w_gate, layer=l)
        qk = _qk_conv(z, m_conv_w, float((qk_w // (2 * M_HEADS)) ** -0.5), layer=l)
        h_f, h_b = _mlstm(qk, z, gates, gates_t, gate_b, gate_b_t, layer=l, width=Wm, v_blk=c_mv // Wm)
        q_t, kn, v_t = _qk_prep(z, cos_t, sin_t, qg, kg, layer=l, q_blk=c_aq // Wa, k_blk=c_ak // Wa,
                                v_blk=c_av // Wa, hd=hd, q_scale=float(hd ** -0.5 * math.log2(math.e)))
        y_a = _flash_diff_attention(q_t, kn, v_t, a_lambda, subln, layer=l, lam_init=lam_init)
        merged = _merge(h_f, h_b, mng, y_a, w_bm, w_ba, z, layer=l, mo_blk=c_mo // Wm,
                        gm_col=c_gm, ga_col=c_ga)
        x1, h2, route, counts = _out_projection(merged, w_o, xs, mod, n2, w_router, b_router, layer=l)
        tile_expert, n_active, pos, fill, next_expert, expert_ord = _dispatch_plan(
            route, counts, n_experts, tm_moe)
        x_sorted = _dispatch(h2, pos, fill, n_experts=n_experts, tm=tm_moe)
        y_sorted = _experts(x_sorted, tile_expert, n_active, next_expert, expert_ord,
                            e_w_gate, e_w_up, e_w_down, layer=l, tm=tm_moe)
        xs = _combine(y_sorted, pos, x1, mod, route, layer=l, tm=tm_moe)
    return xs[None]
```

```python
import functools
import math

import jax
import jax.numpy as jnp
from jax import lax
from jax.experimental import pallas as pl
from jax.experimental.pallas import tpu as pltpu

F32 = jnp.float32
BF16 = jnp.bfloat16

M_HEADS = 4
A_HEADS = 8
CONV_WIDTH = 5
ROPE_THETA = 500000.0
N_GROUPS = 4
EXPERTS_PER_GROUP = 8
EPS = 1e-6
NEG = -1e30
LANES = 128
SUBLANES = 8
BF16_SUBLANES = 16
VMEM_LIMIT = 56 * 1024 * 1024
MOE_TILE = 256
RANK_RADIX = 65536


def _sigmoid(x):
    return 1.0 / (1.0 + jnp.exp(-x))


def _log_sigmoid(x):
    return jnp.minimum(x, 0.0) - jnp.log(1.0 + jnp.exp(-jnp.abs(x)))


def _params(*sem):
    return pltpu.CompilerParams(dimension_semantics=sem, vmem_limit_bytes=VMEM_LIMIT)


def _layer_vec(layer, width, blk=0):
    return pl.BlockSpec((None, 1, width), lambda *_: (layer, 0, blk))


def _mod_kernel(c_ref, w_ref, b_ref, o_ref):
    c = c_ref[...]
    ca = c * _sigmoid(c)
    o_ref[...] = jnp.sum(ca * w_ref[...], axis=0, keepdims=True) + b_ref[...]


def _modulation(c, ada_w, ada_b):
    L, D, N = ada_w.shape
    tn = min(1024, N)
    return pl.pallas_call(
        _mod_kernel,
        grid=(L, N // tn),
        in_specs=[pl.BlockSpec((D, 1), lambda l, j: (0, 0)),
                  pl.BlockSpec((None, D, tn), lambda l, j: (l, 0, j)),
                  pl.BlockSpec((None, 1, tn), lambda l, j: (l, 0, j))],
        out_specs=pl.BlockSpec((None, 1, tn), lambda l, j: (l, 0, j)),
        out_shape=jax.ShapeDtypeStruct((L, 1, N), F32),
        compiler_params=_params("arbitrary", "arbitrary"),
        name="adaln_mod",
    )(c.reshape(D, 1), ada_w, ada_b.reshape(L, 1, N))


def _inproj_kernel(x_ref, g_ref, sc_ref, sh_ref, w_ref, wg_ref, z_ref, gate_ref, gate_t_ref, h_scr):
    @pl.when(pl.program_id(1) == 0)
    def _():
        x = x_ref[...]
        ms = jnp.mean(x * x, axis=-1, keepdims=True)
        h = x * lax.rsqrt(ms + EPS) * g_ref[...] * (1.0 + sc_ref[...]) + sh_ref[...]
        hb = h.astype(BF16)
        h_scr[...] = hb
        g = jnp.dot(hb, wg_ref[...], preferred_element_type=F32)
        gate_ref[...] = g
        gate_t_ref[...] = g.T

    z_ref[...] = jnp.dot(h_scr[...], w_ref[...], preferred_element_type=F32).astype(BF16)


def _in_projection(x, norm_g, mod, w_main, w_gate, *, layer):
    S, D = x.shape
    N = w_main.shape[2]
    tm, tn = min(1024, S), 1024
    return pl.pallas_call(
        _inproj_kernel,
        grid=(S // tm, N // tn),
        in_specs=[pl.BlockSpec((tm, D), lambda i, j: (i, 0)),
                  _layer_vec(layer, D), _layer_vec(layer, D, 1), _layer_vec(layer, D, 0),
                  pl.BlockSpec((None, D, tn), lambda i, j: (layer, 0, j)),
                  pl.BlockSpec((None, D, LANES), lambda i, j: (layer, 0, 0))],
        out_specs=[pl.BlockSpec((tm, tn), lambda i, j: (i, j)),
                   pl.BlockSpec((tm, LANES), lambda i, j: (i, 0)),
                   pl.BlockSpec((LANES, tm), lambda i, j: (0, i))],
        out_shape=[jax.ShapeDtypeStruct((S, N), BF16), jax.ShapeDtypeStruct((S, LANES), F32),
                   jax.ShapeDtypeStruct((LANES, S), F32)],
        scratch_shapes=[pltpu.VMEM((tm, D), BF16)],
        compiler_params=_params("arbitrary", "arbitrary"),
        name="norm_inproj",
    )(x, norm_g, mod, mod, w_main, w_gate)


def _conv_kernel(prev_ref, cur_ref, next_ref, w_ref, o_ref, ext_scr, *, ts, q_scale):
    i = pl.program_id(0)
    halo = BF16_SUBLANES
    pad = CONV_WIDTH // 2
    ext_scr[0:halo, :] = jnp.where(i > 0, prev_ref[...].astype(F32), 0.0)
    ext_scr[halo:halo + ts, :] = cur_ref[...].astype(F32)
    ext_scr[halo + ts:2 * halo + ts, :] = jnp.where(i < pl.num_programs(0) - 1,
                                                    next_ref[...].astype(F32), 0.0)
    acc = w_ref[0:1, :] * ext_scr[pl.ds(halo - pad, ts), :]
    for k in range(1, CONV_WIDTH):
        acc = acc + w_ref[k:k + 1, :] * ext_scr[pl.ds(halo - pad + k, ts), :]
    y = acc * _sigmoid(acc)
    C = y.shape[1]
    col = lax.broadcasted_iota(jnp.int32, (1, C), 1)
    o_ref[...] = (y * jnp.where(col < C // 2, q_scale, 1.0)).astype(BF16)


def _qk_conv(z, conv_w, q_scale, *, layer):
    S = z.shape[0]
    C = conv_w.shape[2]
    ts = min(512, S)
    halo = BF16_SUBLANES
    r = ts // halo
    nblk = S // halo
    return pl.pallas_call(
        functools.partial(_conv_kernel, ts=ts, q_scale=q_scale),
        grid=(S // ts,),
        in_specs=[pl.BlockSpec((halo, C), lambda i: (jnp.maximum(i * r - 1, 0), 0)),
                  pl.BlockSpec((ts, C), lambda i: (i, 0)),
                  pl.BlockSpec((halo, C), lambda i: (jnp.minimum((i + 1) * r, nblk - 1), 0)),
                  pl.BlockSpec((None, CONV_WIDTH, C), lambda i: (layer, 0, 0))],
        out_specs=pl.BlockSpec((ts, C), lambda i: (i, 0)),
        out_shape=jax.ShapeDtypeStruct((S, C), BF16),
        scratch_shapes=[pltpu.VMEM((ts + 2 * halo, C), F32)],
        compiler_params=_params("arbitrary"),
        name="qk_conv",
    )(z, z, z, conv_w)


def _mlstm_kernel(qk_f, v_f, g_f, gt_f, qk_b, v_b, g_b, gt_b, gb_ref, gbt_ref, hf_ref, hb_ref,
                  C_scr, n_scr, m_scr, *, dk, dv, L):
    @pl.when(pl.program_id(0) == 0)
    def _():
        C_scr[...] = jnp.zeros_like(C_scr)
        n_scr[...] = jnp.zeros_like(n_scr)
        m_scr[...] = jnp.zeros_like(m_scr)

    row = lax.broadcasted_iota(jnp.int32, (L, L), 0)
    col = lax.broadcasted_iota(jnp.int32, (L, L), 1)
    for reverse, refs in ((False, (qk_f, v_f, g_f, gt_f, hf_ref)), (True, (qk_b, v_b, g_b, gt_b, hb_ref))):
        _mlstm_direction(reverse, *refs, gb_ref, gbt_ref, C_scr, n_scr, m_scr, row, col, dk=dk, dv=dv)


def _mlstm_direction(reverse, qk_ref, v_ref, g_ref, gt_ref, h_ref, gb_ref, gbt_ref, C_scr, n_scr, m_scr,
                     row, col, *, dk, dv):
    H = M_HEADS
    gates = g_ref[...] + gb_ref[...]
    gates_t = gt_ref[...] + gbt_ref[...]
    mask = (col >= row) if reverse else (col <= row)
    mask_t = (row >= col) if reverse else (row <= col)
    nt = (((1,), (1,)), ((), ()))
    tn = (((0,), (0,)), ((), ()))

    for h in range(H):
        gi = (2 * H if reverse else 0) + h
        gf = gi + H
        si = (H if reverse else 0) + h
        i_col = gates[:, gi:gi + 1]
        f_col = _log_sigmoid(gates[:, gf:gf + 1])
        i_row = gates_t[gi:gi + 1, :]
        f_row = _log_sigmoid(gates_t[gf:gf + 1, :])
        b_col = jnp.sum(jnp.where(mask, f_row, 0.0), axis=1, keepdims=True)
        b_row = jnp.sum(jnp.where(mask_t, f_col, 0.0), axis=0, keepdims=True)
        b_end = jnp.sum(f_row, axis=1, keepdims=True)
        m_prev = m_scr[si][:, 0:1]

        dmat = jnp.where(mask, b_col - b_row + i_row, NEG)
        m_inter = b_col + m_prev
        m_t = jnp.maximum(m_inter, jnp.max(dmat, axis=1, keepdims=True))
        q = qk_ref[:, h * dk:(h + 1) * dk]
        k = qk_ref[:, (H + h) * dk:(H + h + 1) * dk]
        v = v_ref[:, h * dv:(h + 1) * dv]
        s = lax.dot_general(q, k, nt, preferred_element_type=F32) * jnp.exp(dmat - m_t)
        inter = jnp.exp(m_inter - m_t)
        c_prev = C_scr[si]
        n_prev = n_scr[si]
        num = (inter * jnp.dot(q, c_prev.astype(BF16), preferred_element_type=F32)
               + jnp.dot(s.astype(BF16), v, preferred_element_type=F32))
        den = (inter * jnp.sum(q.astype(F32) * n_prev, axis=1, keepdims=True)
               + jnp.sum(s, axis=1, keepdims=True))
        hout = num / jnp.maximum(jnp.abs(den), jnp.exp(-m_t))

        g_col = b_end - b_col + i_col
        m_new = jnp.maximum(b_end + m_prev, jnp.max(g_col, axis=0, keepdims=True))
        w_col = jnp.exp(g_col - m_new)
        decay = jnp.exp(b_end + m_prev - m_new)
        kw = k.astype(F32) * w_col
        C_scr[si] = decay * c_prev + lax.dot_general(kw.astype(BF16), v, tn, preferred_element_type=F32)
        n_scr[si] = decay * n_prev + jnp.sum(kw, axis=0, keepdims=True)
        m_scr[si] = jnp.broadcast_to(m_new, (1, LANES))
        h_ref[:, h * dv:(h + 1) * dv] = hout


def _mlstm(qk, z, gates, gates_t, gate_b, gate_b_t, *, layer, width, v_blk):
    S = qk.shape[0]
    H = M_HEADS
    dk = qk.shape[1] // (2 * H)
    W = width
    dv = W // H
    L = min(256, S)
    nc = S // L

    def specs(cmap):
        return [pl.BlockSpec((L, 2 * H * dk), lambda c: (cmap(c), 0)),
                pl.BlockSpec((L, W), lambda c: (cmap(c), v_blk)),
                pl.BlockSpec((L, LANES), lambda c: (cmap(c), 0)),
                pl.BlockSpec((4 * H, L), lambda c: (0, cmap(c)))]

    fwd = lambda c: c
    bwd = lambda c: nc - 1 - c
    return pl.pallas_call(
        functools.partial(_mlstm_kernel, dk=dk, dv=dv, L=L),
        grid=(nc,),
        in_specs=specs(fwd) + specs(bwd) + [_layer_vec(layer, LANES),
                                            pl.BlockSpec((None, 4 * H, 1), lambda c: (layer, 0, 0))],
        out_specs=[pl.BlockSpec((L, W), lambda c: (c, 0)), pl.BlockSpec((L, W), lambda c: (bwd(c), 0))],
        out_shape=[jax.ShapeDtypeStruct((S, W), F32), jax.ShapeDtypeStruct((S, W), F32)],
        scratch_shapes=[pltpu.VMEM((2 * H, dk, dv), F32), pltpu.VMEM((2 * H, 1, dk), F32),
                        pltpu.VMEM((2 * H, 1, LANES), F32)],
        compiler_params=_params("arbitrary"),
        name="mlstm_bidir",
    )(qk, z, gates, gates_t, qk, z, gates, gates_t, gate_b, gate_b_t)


def _qkprep_kernel(q_ref, k_ref, v_ref, cos_ref, sin_ref, qg_ref, kg_ref, qt_ref, kn_ref, vt_ref,
                   *, hd, q_scale):
    ts, W = q_ref.shape
    lane = lax.broadcasted_iota(jnp.int32, (ts, LANES), 1)
    lo = lane < hd
    first_half = (lane % hd) < (hd // 8)
    row_lo = lax.broadcasted_iota(jnp.int32, (LANES, ts), 0) < hd
    cos = cos_ref[...]
    sin = sin_ref[...]

    def norm_rope(x, g):
        sq = x * x
        s_lo = jnp.sum(jnp.where(lo, sq, 0.0), axis=1, keepdims=True)
        s_hi = jnp.sum(jnp.where(lo, 0.0, sq), axis=1, keepdims=True)
        inv = lax.rsqrt(jnp.where(lo, s_lo, s_hi) * (1.0 / hd) + EPS)
        xn = x * inv * g
        up = pltpu.roll(xn, LANES - hd // 8, axis=1)
        dn = pltpu.roll(xn, hd // 8, axis=1)
        return xn * cos + jnp.where(first_half, up, dn) * sin

    for j in range(W // LANES):
        sl = slice(j * LANES, (j + 1) * LANES)
        qn = norm_rope(q_ref[:, sl].astype(F32), qg_ref[:, sl]) * q_scale
        qn_t = qn.T
        qt_ref[0, sl, :] = jnp.where(row_lo, qn_t, 0.0).astype(BF16)
        qt_ref[1, sl, :] = jnp.where(row_lo, 0.0, qn_t).astype(BF16)
        kn_ref[:, sl] = norm_rope(k_ref[:, sl].astype(F32), kg_ref[:, sl]).astype(BF16)
        r0 = j * (LANES + BF16_SUBLANES)
        vt_ref[r0:r0 + LANES, :] = v_ref[:, sl].astype(F32).T.astype(BF16)
        vt_ref[r0 + LANES:r0 + LANES + BF16_SUBLANES, :] = jnp.where(
            lax.broadcasted_iota(jnp.int32, (BF16_SUBLANES, ts), 0) == 0, 1.0, 0.0).astype(BF16)


def _qk_prep(z, cos_t, sin_t, qg, kg, *, layer, q_blk, k_blk, v_blk, hd, q_scale):
    S = z.shape[0]
    W = qg.shape[2]
    ts = min(512, S)
    n_slab = W // LANES
    return pl.pallas_call(
        functools.partial(_qkprep_kernel, hd=hd, q_scale=q_scale),
        grid=(S // ts,),
        in_specs=[pl.BlockSpec((ts, W), lambda i: (i, q_blk)),
                  pl.BlockSpec((ts, W), lambda i: (i, k_blk)),
                  pl.BlockSpec((ts, W), lambda i: (i, v_blk)),
                  pl.BlockSpec((ts, LANES), lambda i: (i, 0)),
                  pl.BlockSpec((ts, LANES), lambda i: (i, 0)),
                  _layer_vec(layer, W), _layer_vec(layer, W)],
        out_specs=[pl.BlockSpec((2, W, ts), lambda i: (0, 0, i)),
                   pl.BlockSpec((ts, W), lambda i: (i, 0)),
                   pl.BlockSpec((W + n_slab * BF16_SUBLANES, ts), lambda i: (0, i))],
        out_shape=[jax.ShapeDtypeStruct((2, W, S), BF16), jax.ShapeDtypeStruct((S, W), BF16),
                   jax.ShapeDtypeStruct((W + n_slab * BF16_SUBLANES, S), BF16)],
        compiler_params=_params("arbitrary"),
        name="attn_qk_prep",
    )(z, z, z, cos_t, sin_t, qg, kg)


ATTN_COL_BLOCK = 256
ATTN_KEY_BLOCK = 256
ATTN_COL_GROUP = 8
ATTN_LOOKAHEAD = 8


def _flash_kernel(qa_ref, qb_ref, k_ref, vt_ref, lam_ref, sg_ref, o_ref, q_scr, m_scr, acc_scr,
                  *, tq, tk, lam_init):
    kj = pl.program_id(2)
    cb_w, kb_w = ATTN_COL_BLOCK, min(ATTN_KEY_BLOCK, tk)
    dv = o_ref.shape[1]

    @pl.when(kj == 0)
    def _():
        q_scr[:, 0:tq] = qa_ref[0]
        q_scr[:, tq:2 * tq] = qb_ref[0]
        m_scr[...] = jnp.full_like(m_scr, NEG)
        acc_scr[...] = jnp.zeros_like(acc_scr)

    group = min(ATTN_COL_GROUP, 2 * tq // cb_w)
    steps = [(c, kb) for kb in range(tk // kb_w) for c in range(group)]

    def col_group(gi, carry):
        cols = [pl.ds(pl.multiple_of((gi * group + c) * cb_w, cb_w), cb_w) for c in range(group)]
        qts = [q_scr[:, cs] for cs in cols]
        state = [(m_scr[:, cs], acc_scr[:, cs]) for cs in cols]

        def scores(c, kb):
            return jnp.dot(k_ref[kb * kb_w:(kb + 1) * kb_w, :], qts[c], preferred_element_type=F32)

        pending = [scores(*st) for st in steps[:ATTN_LOOKAHEAD]]
        for n, (c, kb) in enumerate(steps):
            s = pending.pop(0)
            if n + ATTN_LOOKAHEAD < len(steps):
                pending.append(scores(*steps[n + ATTN_LOOKAHEAD]))
            m, acc = state[c]
            m_new = jnp.maximum(m, jnp.max(s, axis=0, keepdims=True))
            alpha = jnp.exp2(m - m_new)
            p = jnp.exp2((s - m_new).astype(BF16))
            acc = alpha * acc + jnp.dot(vt_ref[:, kb * kb_w:(kb + 1) * kb_w], p,
                                        preferred_element_type=F32)
            state[c] = (m_new, acc)
        for cs, (m, acc) in zip(cols, state):
            m_scr[:, cs], acc_scr[:, cs] = m, acc
        return carry

    lax.fori_loop(0, 2 * tq // (cb_w * group), col_group, 0)

    @pl.when(kj == pl.num_programs(2) - 1)
    def _():
        lam = lam_ref[...]
        lam_full = (jnp.exp(jnp.sum(lam[0:1] * lam[1:2], axis=1, keepdims=True))
                    - jnp.exp(jnp.sum(lam[2:3] * lam[3:4], axis=1, keepdims=True)) + lam_init)
        a = acc_scr[0:dv, :] / acc_scr[dv:dv + 1, :]
        o_t = a[:, 0:tq] - lam_full * a[:, tq:2 * tq]
        ms = jnp.mean(o_t * o_t, axis=0, keepdims=True)
        y_t = o_t * lax.rsqrt(ms + EPS) * sg_ref[...] * (1.0 - lam_init)
        o_ref[...] = y_t.T.astype(BF16)


def _flash_diff_attention(q_t, kn, v_t, lam, subln_g, *, layer, lam_init):
    _, W, S = q_t.shape
    H = A_HEADS
    dv = W // H
    tq = min(2048, S)
    tk = min(1024, S)
    return pl.pallas_call(
        functools.partial(_flash_kernel, tq=tq, tk=tk, lam_init=lam_init),
        grid=(H, S // tq, S // tk),
        in_specs=[pl.BlockSpec((1, dv, tq), lambda h, i, j: (0, h, i)),
                  pl.BlockSpec((1, dv, tq), lambda h, i, j: (1, h, i)),
                  pl.BlockSpec((tk, dv), lambda h, i, j: (j, h)),
                  pl.BlockSpec((dv + BF16_SUBLANES, tk), lambda h, i, j: (h, j)),
                  pl.BlockSpec((None,) + lam.shape[1:], lambda h, i, j: (layer, 0, 0)),
                  pl.BlockSpec((None, dv, 1), lambda h, i, j: (layer, 0, 0))],
        out_specs=pl.BlockSpec((tq, dv), lambda h, i, j: (i, h)),
        out_shape=jax.ShapeDtypeStruct((S, W), BF16),
        scratch_shapes=[pltpu.VMEM((dv, 2 * tq), BF16), pltpu.VMEM((1, 2 * tq), F32),
                        pltpu.VMEM((dv + BF16_SUBLANES, 2 * tq), F32)],
        compiler_params=_params("arbitrary", "arbitrary", "arbitrary"),
        name="flash_diff_attn",
    )(q_t, q_t, kn, v_t, lam, subln_g)


def _merge_kernel(hf_ref, hb_ref, mo_ref, ng_ref, ya_ref, wm_ref, wa_ref, gm_ref, ga_ref, o_ref, ym_scr):
    @pl.when(pl.program_id(1) == 0)
    def _():
        dv = hf_ref.shape[1] // M_HEADS
        for h in range(M_HEADS):
            sl = slice(h * dv, (h + 1) * dv)
            hs = hf_ref[:, sl] + hb_ref[:, sl]
            ms = jnp.mean(hs * hs, axis=-1, keepdims=True)
            y = hs * lax.rsqrt(ms + EPS) * ng_ref[:, sl] * _sigmoid(mo_ref[:, sl].astype(F32))
            ym_scr[:, sl] = y.astype(BF16)

    a = jnp.dot(ym_scr[...], wm_ref[...], preferred_element_type=F32)
    b = jnp.dot(ya_ref[...], wa_ref[...], preferred_element_type=F32)
    o_ref[...] = (_sigmoid(gm_ref[...].astype(F32)) * a + _sigmoid(ga_ref[...].astype(F32)) * b).astype(BF16)


def _merge(h_f, h_b, norm_g, y_a, w_m, w_a, z, *, layer, mo_blk, gm_col, ga_col):
    S, W = y_a.shape
    D = w_m.shape[2]
    tm, tn = min(512, S), 1024
    rows = pl.BlockSpec((tm, W), lambda i, j: (i, 0))
    return pl.pallas_call(
        _merge_kernel,
        grid=(S // tm, D // tn),
        in_specs=[rows, rows, pl.BlockSpec((tm, W), lambda i, j: (i, mo_blk)), _layer_vec(layer, W), rows,
                  pl.BlockSpec((None, W, tn), lambda i, j: (layer, 0, j)),
                  pl.BlockSpec((None, W, tn), lambda i, j: (layer, 0, j)),
                  pl.BlockSpec((tm, tn), lambda i, j: (i, gm_col // tn + j)),
                  pl.BlockSpec((tm, tn), lambda i, j: (i, ga_col // tn + j))],
        out_specs=pl.BlockSpec((tm, tn), lambda i, j: (i, j)),
        out_shape=jax.ShapeDtypeStruct((S, D), BF16),
        scratch_shapes=[pltpu.VMEM((tm, W), BF16)],
        compiler_params=_params("arbitrary", "arbitrary"),
        name="branch_merge",
    )(h_f, h_b, z, norm_g, y_a, w_m, w_a, z, z)


def _route(logits):
    G, E = N_GROUPS, EXPERTS_PER_GROUP
    lane = lax.broadcasted_iota(jnp.int32, logits.shape, 1)
    lanef = lane.astype(F32)
    is_g = lane < G
    gl = jnp.where(is_g, logits, NEG)
    ge = jnp.where(is_g, jnp.exp(gl - jnp.max(gl, axis=1, keepdims=True)), 0.0)
    probs = ge / jnp.sum(ge, axis=1, keepdims=True)
    g_val = jnp.max(probs, axis=1, keepdims=True)
    g_idx = jnp.min(jnp.where(is_g & (probs == g_val), lanef, float(LANES)), axis=1, keepdims=True)
    lo = G + E * g_idx
    in_grp = (lanef >= lo) & (lanef < lo + E)
    el = jnp.where(in_grp, logits, NEG)
    v1 = jnp.max(el, axis=1, keepdims=True)
    i1 = jnp.min(jnp.where(in_grp & (el == v1), lanef, float(LANES)), axis=1, keepdims=True)
    rest = in_grp & (lanef != i1)
    el2 = jnp.where(rest, logits, NEG)
    v2 = jnp.max(el2, axis=1, keepdims=True)
    i2 = jnp.min(jnp.where(rest & (el2 == v2), lanef, float(LANES)), axis=1, keepdims=True)
    t = jnp.exp(v2 - v1)
    w1 = (1.0 / (1.0 + t)) * g_val
    w2 = (t / (1.0 + t)) * g_val
    return jnp.where(lane == 0, i1 - G, jnp.where(lane == 1, i2 - G,
                     jnp.where(lane == 2, w1, jnp.where(lane == 3, w2, 0.0))))


def _out_kernel(mg_ref, w_ref, x_ref, g1_ref, n2_ref, sc_ref, sh_ref, wr_ref, br_ref,
                x1_ref, h2_ref, r_ref, cnt_ref, cnt_scr):
    tm = x_ref.shape[0]

    @pl.when(pl.program_id(0) == 0)
    def _():
        cnt_scr[...] = jnp.zeros_like(cnt_scr)

    y = jnp.dot(mg_ref[...], w_ref[...], preferred_element_type=F32)
    x1 = x_ref[...] + g1_ref[...] * y
    x1_ref[...] = x1
    ms = jnp.mean(x1 * x1, axis=-1, keepdims=True)
    h2 = x1 * lax.rsqrt(ms + EPS) * n2_ref[...] * (1.0 + sc_ref[...]) + sh_ref[...]
    h2_ref[...] = h2
    h_hi = h2.astype(BF16)
    h_lo = (h2 - h_hi.astype(F32)).astype(BF16)
    t = jnp.dot(h_hi, wr_ref[...], preferred_element_type=F32)
    logits = (t[:, :LANES] + t[:, LANES:]
              + jnp.dot(h_lo, wr_ref[:, :LANES], preferred_element_type=F32) + br_ref[...])
    route = _route(logits)

    lane = lax.broadcasted_iota(jnp.int32, (tm, LANES), 1)
    lanef = lane.astype(F32)
    sel1 = lanef == route[:, 0:1]
    sel2 = lanef == route[:, 1:2]
    onehot = jnp.where(sel1 | sel2, 1.0, 0.0)
    earlier = (lax.broadcasted_iota(jnp.int32, (tm, tm), 0)
               > lax.broadcasted_iota(jnp.int32, (tm, tm), 1))
    before = cnt_scr[...] + jnp.dot(jnp.where(earlier, 1.0, 0.0).astype(BF16), onehot.astype(BF16),
                                    preferred_element_type=F32)
    rank1 = jnp.sum(jnp.where(sel1, before, 0.0), axis=1, keepdims=True)
    rank2 = jnp.sum(jnp.where(sel2, before, 0.0), axis=1, keepdims=True)
    code1 = route[:, 0:1] * float(RANK_RADIX) + rank1
    code2 = route[:, 1:2] * float(RANK_RADIX) + rank2
    r_ref[...] = jnp.where(lane == 4, code1, jnp.where(lane == 5, code2, route))
    cnt = cnt_scr[...] + jnp.sum(onehot, axis=0, keepdims=True)
    cnt_scr[...] = cnt
    cnt_ref[...] = jnp.broadcast_to(cnt, cnt_ref.shape)


def _out_projection(merged, w_out, x, mod, norm2_g, w_router, b_router, *, layer):
    S, D = x.shape
    tm = min(512, S)
    row = pl.BlockSpec((tm, D), lambda i: (i, 0))
    once = pl.Buffered(1)
    return pl.pallas_call(
        _out_kernel,
        grid=(S // tm,),
        in_specs=[row, pl.BlockSpec((None, D, D), lambda i: (layer, 0, 0), pipeline_mode=once), row,
                  _layer_vec(layer, D, 2), _layer_vec(layer, D), _layer_vec(layer, D, 4),
                  _layer_vec(layer, D, 3),
                  pl.BlockSpec((None, D, 2 * LANES), lambda i: (layer, 0, 0), pipeline_mode=once),
                  _layer_vec(layer, LANES)],
        out_specs=[row, row, pl.BlockSpec((tm, LANES), lambda i: (i, 0)),
                   pl.BlockSpec((SUBLANES, LANES), lambda i: (0, 0))],
        out_shape=[jax.ShapeDtypeStruct((S, D), F32), jax.ShapeDtypeStruct((S, D), F32),
                   jax.ShapeDtypeStruct((S, LANES), F32), jax.ShapeDtypeStruct((SUBLANES, LANES), F32)],
        scratch_shapes=[pltpu.VMEM((1, LANES), F32)],
        compiler_params=_params("arbitrary"),
        name="out_proj_router",
    )(merged, w_out, x, mod, norm2_g, mod, mod, w_router, b_router)


def _dispatch_plan(route, counts_f, n_experts, tm):
    T = route.shape[0]
    counts = counts_f[0, :n_experts].astype(jnp.int32)
    padded = ((counts + tm - 1) // tm) * tm
    pad_end = jnp.cumsum(padded)
    pad_start = pad_end - padded
    n_tiles = (2 * T) // tm + n_experts
    n_active = (pad_end[-1] // tm).astype(jnp.int32)
    tile_start = jnp.arange(n_tiles, dtype=jnp.int32) * tm
    tile_expert = jnp.sum(pad_end[None, :] <= jnp.minimum(tile_start, pad_end[-1] - 1)[:, None],
                          axis=1, dtype=jnp.int32)
    pos = _sorted_positions(route, pad_start)
    tail = ((pad_start + counts) // SUBLANES) * SUBLANES
    fill = jnp.concatenate([tail, (pad_end - tail) // SUBLANES, n_active.reshape(1)])
    ids = jnp.arange(n_experts, dtype=jnp.int32)
    has_rows = counts > 0
    later = (ids[None, :] > ids[:, None]) & has_rows[None, :]
    next_expert = jnp.min(jnp.where(later, ids[None, :], n_experts), axis=1)
    next_expert = jnp.where(next_expert == n_experts, -1, next_expert).astype(jnp.int32)
    expert_ord = jnp.sum((ids[None, :] < ids[:, None]) & has_rows[None, :], axis=1, dtype=jnp.int32)
    return tile_expert, n_active.reshape(1), pos, fill, next_expert, expert_ord


def _positions_kernel(r_ref, start_ref, o_ref):
    r = r_ref[...]
    lane = lax.broadcasted_iota(jnp.int32, r.shape, 1)
    lanef = lane.astype(F32)
    start = start_ref[...]
    pos = []
    for s in range(2):
        e, code = r[:, s:s + 1], r[:, 4 + s:5 + s]
        first = jnp.sum(jnp.where(lanef == e, start, 0.0), axis=1, keepdims=True)
        pos.append(first + (code - e * float(RANK_RADIX)))
    o_ref[...] = jnp.where(lane == 0, pos[0], jnp.where(lane == 1, pos[1], 0.0))


def _sorted_positions(route, pad_start):
    T = route.shape[0]
    tm = min(1024, T)
    start = jnp.pad(pad_start.astype(F32), (0, LANES - pad_start.shape[0]))[None]
    out = pl.pallas_call(
        _positions_kernel,
        grid=(T // tm,),
        in_specs=[pl.BlockSpec((tm, LANES), lambda i: (i, 0)), pl.BlockSpec((1, LANES), lambda i: (0, 0))],
        out_specs=pl.BlockSpec((tm, LANES), lambda i: (i, 0)),
        out_shape=jax.ShapeDtypeStruct((T, LANES), F32),
        compiler_params=_params("arbitrary"),
        name="moe_positions",
    )(route, start)
    return out[:, 0:2].T.reshape(-1).astype(jnp.int32)


def _dispatch_kernel(pos_ref, fill_ref, h_ref, xs_hbm, zero_scr, sem, *, tm, T, n_experts, n_tiles):
    i = pl.program_id(0)

    @pl.when(i == 0)
    def _():
        zero_scr[...] = jnp.zeros_like(zero_scr)
        zero8 = zero_scr.at[pl.ds(0, SUBLANES), :]
        n_active = fill_ref[2 * n_experts]

        def per_expert(e, total):
            def piece(k, carry):
                row = pl.multiple_of(fill_ref[e] + k * SUBLANES, SUBLANES)
                pltpu.make_async_copy(zero8, xs_hbm.at[pl.ds(row, SUBLANES), :], sem.at[1]).start()
                return carry
            n = fill_ref[n_experts + e]
            lax.fori_loop(0, n, piece, 0)
            return total + n

        pieces = lax.fori_loop(0, n_experts, per_expert, 0)

        def unused_tile(t, carry):
            row = pl.multiple_of(t * tm, tm)
            pltpu.make_async_copy(zero_scr, xs_hbm.at[pl.ds(row, tm), :], sem.at[2]).start()
            return carry

        lax.fori_loop(n_active, n_tiles, unused_tile, 0)

        def wait_piece(k, carry):
            pltpu.make_async_copy(zero8, xs_hbm.at[pl.ds(0, SUBLANES), :], sem.at[1]).wait()
            return carry

        lax.fori_loop(0, pieces, wait_piece, 0)

        def wait_tile(t, carry):
            pltpu.make_async_copy(zero_scr, xs_hbm.at[pl.ds(0, tm), :], sem.at[2]).wait()
            return carry

        lax.fori_loop(n_active, n_tiles, wait_tile, 0)

    def body(r, carry):
        for slot in range(2):
            p = pos_ref[slot * T + i * tm + r]
            pltpu.make_async_copy(h_ref.at[pl.ds(r, 1), :], xs_hbm.at[pl.ds(p, 1), :], sem.at[0]).start()
        return carry

    lax.fori_loop(0, tm, body, 0, unroll=8)
    for slot in range(2):
        pltpu.make_async_copy(h_ref, xs_hbm.at[pl.ds(0, tm), :], sem.at[0]).wait()


def _dispatch(h2, pos, fill, *, n_experts, tm):
    T, D = h2.shape
    n_tiles = (2 * T) // tm + n_experts
    grid_spec = pltpu.PrefetchScalarGridSpec(
        num_scalar_prefetch=2,
        grid=(T // tm,),
        in_specs=[pl.BlockSpec((tm, D), lambda i, pos, fill: (i, 0))],
        out_specs=pl.BlockSpec(memory_space=pl.ANY),
        scratch_shapes=[pltpu.VMEM((tm, D), F32), pltpu.SemaphoreType.DMA((3,))],
    )
    return pl.pallas_call(
        functools.partial(_dispatch_kernel, tm=tm, T=T, n_experts=n_experts, n_tiles=n_tiles),
        grid_spec=grid_spec,
        out_shape=jax.ShapeDtypeStruct((n_tiles * tm, D), F32),
        compiler_params=_params("arbitrary"),
        name="moe_dispatch",
    )(pos, fill, h2)


def _expert_kernel(te_ref, na_ref, nx_ref, od_ref, x_ref, wg_hbm, wu_hbm, wd_hbm, o_ref,
                   wg_f, wu_f, wd_f, wg_b, wu_b, wd_b, sem, *, layer):
    i = pl.program_id(0)

    def weight_copies(e, slot):
        return [pltpu.make_async_copy(src.at[layer, e], dst.at[slot], sem.at[slot, k])
                for k, (src, dst) in enumerate(((wg_hbm, wg_f), (wu_hbm, wu_f), (wd_hbm, wd_f)))]

    @pl.when(i >= na_ref[0])
    def _():
        o_ref[...] = jnp.zeros_like(o_ref)

    @pl.when(i < na_ref[0])
    def _():
        e = te_ref[i]

        @pl.when((i == 0) | (e != te_ref[jnp.maximum(i - 1, 0)]))
        def _():
            slot = od_ref[e] % 2

            @pl.when(i == 0)
            def _():
                for cp in weight_copies(e, slot):
                    cp.start()

            for cp in weight_copies(e, slot):
                cp.wait()
            nxt = nx_ref[e]

            @pl.when(nxt >= 0)
            def _():
                for cp in weight_copies(nxt, 1 - slot):
                    cp.start()

            wg_b[...] = wg_f[slot].astype(BF16)
            wu_b[...] = wu_f[slot].astype(BF16)
            wd_b[...] = wd_f[slot].astype(BF16)

        x = x_ref[...].astype(BF16)
        g = jnp.dot(x, wg_b[...], preferred_element_type=F32)
        u = jnp.dot(x, wu_b[...], preferred_element_type=F32)
        a = (g * _sigmoid(g) * u).astype(BF16)
        o_ref[...] = jnp.dot(a, wd_b[...], preferred_element_type=F32)


def _experts(xs, tile_expert, n_active, next_expert, expert_ord, w_gate, w_up, w_down, *, layer, tm):
    D = xs.shape[1]
    Fe = w_gate.shape[-1]
    n_tiles = tile_expert.shape[0]
    hbm = pl.BlockSpec(memory_space=pl.ANY)
    grid_spec = pltpu.PrefetchScalarGridSpec(
        num_scalar_prefetch=4,
        grid=(n_tiles,),
        in_specs=[pl.BlockSpec((tm, D), lambda i, te, na, nx, od: (jnp.minimum(i, na[0] - 1), 0)),
                  hbm, hbm, hbm],
        out_specs=pl.BlockSpec((tm, D), lambda i, te, na, nx, od: (i, 0)),
        scratch_shapes=[pltpu.VMEM((2, D, Fe), F32), pltpu.VMEM((2, D, Fe), F32), pltpu.VMEM((2, Fe, D), F32),
                        pltpu.VMEM((D, Fe), BF16), pltpu.VMEM((D, Fe), BF16), pltpu.VMEM((Fe, D), BF16),
                        pltpu.SemaphoreType.DMA((2, 3))],
    )
    return pl.pallas_call(
        functools.partial(_expert_kernel, layer=layer),
        grid_spec=grid_spec,
        out_shape=jax.ShapeDtypeStruct((n_tiles * tm, D), F32),
        compiler_params=_params("arbitrary"),
        name="routed_experts",
    )(tile_expert, n_active, next_expert, expert_ord, xs, w_gate, w_up, w_down)


def _combine_kernel(pos_ref, ys_hbm, x_ref, g2_ref, r_ref, o_ref, ybuf, sem, *, tm, T):
    i = pl.program_id(0)

    def issue(blk, slot):
        def body(r, carry):
            for k in range(2):
                p = pos_ref[k * T + blk * tm + r]
                pltpu.make_async_copy(ys_hbm.at[pl.ds(p, 1), :], ybuf.at[slot, k, pl.ds(r, 1), :],
                                      sem.at[slot]).start()
            return carry
        lax.fori_loop(0, tm, body, 0, unroll=8)

    @pl.when(i == 0)
    def _():
        issue(0, 0)

    @pl.when(i + 1 < pl.num_programs(0))
    def _():
        issue(i + 1, (i + 1) % 2)

    slot = i % 2
    for k in range(2):
        pltpu.make_async_copy(ys_hbm.at[pl.ds(0, tm), :], ybuf.at[slot, k], sem.at[slot]).wait()
    y = r_ref[:, 2:3] * ybuf[slot, 0] + r_ref[:, 3:4] * ybuf[slot, 1]
    o_ref[...] = x_ref[...] + g2_ref[...] * y


def _combine(y_sorted, pos, x1, mod, route, *, layer, tm):
    T, D = x1.shape
    grid_spec = pltpu.PrefetchScalarGridSpec(
        num_scalar_prefetch=1,
        grid=(T // tm,),
        in_specs=[pl.BlockSpec(memory_space=pl.ANY),
                  pl.BlockSpec((tm, D), lambda i, pos: (i, 0)),
                  _layer_vec(layer, D, 5),
                  pl.BlockSpec((tm, LANES), lambda i, pos: (i, 0))],
        out_specs=pl.BlockSpec((tm, D), lambda i, pos: (i, 0)),
        scratch_shapes=[pltpu.VMEM((2, 2, tm, D), F32), pltpu.SemaphoreType.DMA((2,))],
    )
    return pl.pallas_call(
        functools.partial(_combine_kernel, tm=tm, T=T),
        grid_spec=grid_spec,
        out_shape=jax.ShapeDtypeStruct((T, D), F32),
        compiler_params=_params("arbitrary"),
        name="moe_combine",
    )(pos, y_sorted, x1, mod, route)


def _rope_tables(S, hd):
    rope_dim = hd // 4
    posn = jnp.arange(S, dtype=F32)
    inv = ROPE_THETA ** (-jnp.arange(0, rope_dim, 2, dtype=F32) / rope_dim)
    ang = posn[:, None] * inv[None, :]
    cos, sin = jnp.cos(ang), jnp.sin(ang)
    rest = hd - rope_dim
    cos_t = jnp.concatenate([cos, cos, jnp.ones((S, rest), F32)], axis=1)
    sin_t = jnp.concatenate([-sin, sin, jnp.zeros((S, rest), F32)], axis=1)
    reps = LANES // hd
    return jnp.tile(cos_t, (1, reps)), jnp.tile(sin_t, (1, reps))


def kernel(x, c, ada_w, ada_b, norm1_g, norm2_g, w_in, m_conv_w, m_gate_b, m_norm_g, a_qnorm_g,
           a_knorm_g, a_lambda, a_subln_g, w_branch_m, w_branch_a, w_out, rg_w, rg_b, re_w, re_b,
           e_w_gate, e_w_up, e_w_down):
    B, S, D = x.shape
    assert B == 1
    depth = ada_w.shape[0]
    Wm = w_branch_m.shape[1]
    Wa = w_branch_a.shape[1]
    qk_w = m_conv_w.shape[2]
    n_gate = 4 * M_HEADS
    hd = a_qnorm_g.shape[1]
    n_experts = re_w.shape[2]
    n_route = N_GROUPS + n_experts
    assert Wa // A_HEADS == LANES and 2 * hd == LANES
    c_mv, c_mo, c_aq = qk_w, qk_w + Wm, qk_w + 2 * Wm
    c_ak, c_av, c_gm = c_aq + Wa, c_aq + 2 * Wa, c_aq + 3 * Wa
    c_ga = c_gm + D
    g0 = qk_w + Wm

    w_main = jnp.concatenate([w_in[:, :, :g0], w_in[:, :, g0 + n_gate:]], axis=2).astype(BF16)
    w_gate = jnp.pad(w_in[:, :, g0:g0 + n_gate], ((0, 0), (0, 0), (0, LANES - n_gate))).astype(BF16)
    w_bm, w_ba, w_o = w_branch_m.astype(BF16), w_branch_a.astype(BF16), w_out.astype(BF16)
    w_r = jnp.pad(jnp.concatenate([rg_w, re_w], axis=2), ((0, 0), (0, 0), (0, LANES - n_route)))
    w_r_hi = w_r.astype(BF16)
    w_router = jnp.concatenate([w_r_hi, (w_r - w_r_hi.astype(F32)).astype(BF16)], axis=2)
    b_router = jnp.pad(jnp.concatenate([rg_b, re_b], axis=1), ((0, 0), (0, LANES - n_route)))[:, None]
    gate_b = jnp.pad(m_gate_b.reshape(depth, 1, n_gate), ((0, 0), (0, 0), (0, LANES - n_gate)))
    gate_b_t = m_gate_b.reshape(depth, n_gate, 1)
    qg = jnp.tile(a_qnorm_g, (1, Wa // hd))[:, None]
    kg = jnp.tile(a_knorm_g, (1, Wa // hd))[:, None]
    n1, n2, mng = norm1_g[:, None], norm2_g[:, None], m_norm_g[:, None]
    subln = a_subln_g[:, :, None]

    xs = x[0]
    mod = _modulation(c, ada_w, ada_b)
    cos_t, sin_t = _rope_tables(S, hd)
    tm_moe = min(MOE_TILE, S)

    for l in range(depth):
        lam_init = 0.8 - 0.6 * math.exp(-0.3 * l)
        z, gates, gates_t = _in_projection(xs, n1, mod, w_main, w_gate, layer=l)
        qk = _qk_conv(z, m_conv_w, float((qk_w // (2 * M_HEADS)) ** -0.5), layer=l)
        h_f, h_b = _mlstm(qk, z, gates, gates_t, gate_b, gate_b_t, layer=l, width=Wm, v_blk=c_mv // Wm)
        q_t, kn, v_t = _qk_prep(z, cos_t, sin_t, qg, kg, layer=l, q_blk=c_aq // Wa, k_blk=c_ak // Wa,
                                v_blk=c_av // Wa, hd=hd, q_scale=float(hd ** -0.5 * math.log2(math.e)))
        y_a = _flash_diff_attention(q_t, kn, v_t, a_lambda, subln, layer=l, lam_init=lam_init)
        merged = _merge(h_f, h_b, mng, y_a, w_bm, w_ba, z, layer=l, mo_blk=c_mo // Wm,
                        gm_col=c_gm, ga_col=c_ga)
        x1, h2, route, counts = _out_projection(merged, w_o, xs, mod, n2, w_router, b_router, layer=l)
        tile_expert, n_active, pos, fill, next_expert, expert_ord = _dispatch_plan(
            route, counts, n_experts, tm_moe)
        x_sorted = _dispatch(h2, pos, fill, n_experts=n_experts, tm=tm_moe)
        y_sorted = _experts(x_sorted, tile_expert, n_active, next_expert, expert_ord,
                            e_w_gate, e_w_up, e_w_down, layer=l, tm=tm_moe)
        xs = _combine(y_sorted, pos, x1, mod, route, layer=l, tm=tm_moe)
    return xs[None]
```

```python
import functools
import math

import jax
import jax.numpy as jnp
from jax import lax
from jax.experimental import pallas as pl
from jax.experimental.pallas import tpu as pltpu

F32 = jnp.float32
BF16 = jnp.bfloat16

M_HEADS = 4
A_HEADS = 8
CONV_WIDTH = 5
ROPE_THETA = 500000.0
N_GROUPS = 4
EXPERTS_PER_GROUP = 8
EPS = 1e-6
NEG = -1e30
LANES = 128
SUBLANES = 8
BF16_SUBLANES = 16
VMEM_LIMIT = 56 * 1024 * 1024
MOE_TILE = 256
RANK_RADIX = 65536


def _sigmoid(x):
    return 1.0 / (1.0 + jnp.exp(-x))


def _log_sigmoid(x):
    return jnp.minimum(x, 0.0) - jnp.log(1.0 + jnp.exp(-jnp.abs(x)))


def _params(*sem):
    return pltpu.CompilerParams(dimension_semantics=sem, vmem_limit_bytes=VMEM_LIMIT)


def _layer_vec(layer, width, blk=0):
    return pl.BlockSpec((None, 1, width), lambda *_: (layer, 0, blk))


def _mod_kernel(c_ref, w_ref, b_ref, o_ref):
    c = c_ref[...]
    ca = c * _sigmoid(c)
    o_ref[...] = jnp.sum(ca * w_ref[...], axis=0, keepdims=True) + b_ref[...]


def _modulation(c, ada_w, ada_b):
    L, D, N = ada_w.shape
    tn = min(1024, N)
    return pl.pallas_call(
        _mod_kernel,
        grid=(L, N // tn),
        in_specs=[pl.BlockSpec((D, 1), lambda l, j: (0, 0)),
                  pl.BlockSpec((None, D, tn), lambda l, j: (l, 0, j)),
                  pl.BlockSpec((None, 1, tn), lambda l, j: (l, 0, j))],
        out_specs=pl.BlockSpec((None, 1, tn), lambda l, j: (l, 0, j)),
        out_shape=jax.ShapeDtypeStruct((L, 1, N), F32),
        compiler_params=_params("arbitrary", "arbitrary"),
        name="adaln_mod",
    )(c.reshape(D, 1), ada_w, ada_b.reshape(L, 1, N))


def _inproj_kernel(x_ref, g_ref, sc_ref, sh_ref, w_ref, wg_ref, z_ref, gate_ref, gate_t_ref, h_scr):
    @pl.when(pl.program_id(1) == 0)
    def _():
        x = x_ref[...]
        ms = jnp.mean(x * x, axis=-1, keepdims=True)
        h = x * lax.rsqrt(ms + EPS) * g_ref[...] * (1.0 + sc_ref[...]) + sh_ref[...]
        hb = h.astype(BF16)
        h_scr[...] = hb
        g = jnp.dot(hb, wg_ref[...], preferred_element_type=F32)
        gate_ref[...] = g
        gate_t_ref[...] = g.T

    z_ref[...] = jnp.dot(h_scr[...], w_ref[...], preferred_element_type=F32).astype(BF16)


def _in_projection(x, norm_g, mod, w_main, w_gate, *, layer):
    S, D = x.shape
    N = w_main.shape[2]
    tm, tn = min(1024, S), 1024
    return pl.pallas_call(
        _inproj_kernel,
        grid=(S // tm, N // tn),
        in_specs=[pl.BlockSpec((tm, D), lambda i, j: (i, 0)),
                  _layer_vec(layer, D), _layer_vec(layer, D, 1), _layer_vec(layer, D, 0),
                  pl.BlockSpec((None, D, tn), lambda i, j: (layer, 0, j)),
                  pl.BlockSpec((None, D, LANES), lambda i, j: (layer, 0, 0))],
        out_specs=[pl.BlockSpec((tm, tn), lambda i, j: (i, j)),
                   pl.BlockSpec((tm, LANES), lambda i, j: (i, 0)),
                   pl.BlockSpec((LANES, tm), lambda i, j: (0, i))],
        out_shape=[jax.ShapeDtypeStruct((S, N), BF16), jax.ShapeDtypeStruct((S, LANES), F32),
                   jax.ShapeDtypeStruct((LANES, S), F32)],
        scratch_shapes=[pltpu.VMEM((tm, D), BF16)],
        compiler_params=_params("arbitrary", "arbitrary"),
        name="norm_inproj",
    )(x, norm_g, mod, mod, w_main, w_gate)


def _conv_kernel(prev_ref, cur_ref, next_ref, w_ref, o_ref, ext_scr, *, ts, q_scale):
    i = pl.program_id(0)
    halo = BF16_SUBLANES
    pad = CONV_WIDTH // 2
    ext_scr[0:halo, :] = jnp.where(i > 0, prev_ref[...].astype(F32), 0.0)
    ext_scr[halo:halo + ts, :] = cur_ref[...].astype(F32)
    ext_scr[halo + ts:2 * halo + ts, :] = jnp.where(i < pl.num_programs(0) - 1,
                                                    next_ref[...].astype(F32), 0.0)
    acc = w_ref[0:1, :] * ext_scr[pl.ds(halo - pad, ts), :]
    for k in range(1, CONV_WIDTH):
        acc = acc + w_ref[k:k + 1, :] * ext_scr[pl.ds(halo - pad + k, ts), :]
    y = acc * _sigmoid(acc)
    C = y.shape[1]
    col = lax.broadcasted_iota(jnp.int32, (1, C), 1)
    o_ref[...] = (y * jnp.where(col < C // 2, q_scale, 1.0)).astype(BF16)


def _qk_conv(z, conv_w, q_scale, *, layer):
    S = z.shape[0]
    C = conv_w.shape[2]
    ts = min(512, S)
    halo = BF16_SUBLANES
    r = ts // halo
    nblk = S // halo
    return pl.pallas_call(
        functools.partial(_conv_kernel, ts=ts, q_scale=q_scale),
        grid=(S // ts,),
        in_specs=[pl.BlockSpec((halo, C), lambda i: (jnp.maximum(i * r - 1, 0), 0)),
                  pl.BlockSpec((ts, C), lambda i: (i, 0)),
                  pl.BlockSpec((halo, C), lambda i: (jnp.minimum((i + 1) * r, nblk - 1), 0)),
                  pl.BlockSpec((None, CONV_WIDTH, C), lambda i: (layer, 0, 0))],
        out_specs=pl.BlockSpec((ts, C), lambda i: (i, 0)),
        out_shape=jax.ShapeDtypeStruct((S, C), BF16),
        scratch_shapes=[pltpu.VMEM((ts + 2 * halo, C), F32)],
        compiler_params=_params("arbitrary"),
        name="qk_conv",
    )(z, z, z, conv_w)


def _mlstm_kernel(qk_f, v_f, g_f, gt_f, qk_b, v_b, g_b, gt_b, gb_ref, gbt_ref, hf_ref, hb_ref,
                  C_scr, n_scr, m_scr, *, dk, dv, L):
    @pl.when(pl.program_id(0) == 0)
    def _():
        C_scr[...] = jnp.zeros_like(C_scr)
        n_scr[...] = jnp.zeros_like(n_scr)
        m_scr[...] = jnp.zeros_like(m_scr)

    row = lax.broadcasted_iota(jnp.int32, (L, L), 0)
    col = lax.broadcasted_iota(jnp.int32, (L, L), 1)
    for reverse, refs in ((False, (qk_f, v_f, g_f, gt_f, hf_ref)), (True, (qk_b, v_b, g_b, gt_b, hb_ref))):
        _mlstm_direction(reverse, *refs, gb_ref, gbt_ref, C_scr, n_scr, m_scr, row, col, dk=dk, dv=dv)


def _mlstm_direction(reverse, qk_ref, v_ref, g_ref, gt_ref, h_ref, gb_ref, gbt_ref, C_scr, n_scr, m_scr,
                     row, col, *, dk, dv):
    H = M_HEADS
    gates = g_ref[...] + gb_ref[...]
    gates_t = gt_ref[...] + gbt_ref[...]
    mask = (col >= row) if reverse else (col <= row)
    mask_t = (row >= col) if reverse else (row <= col)
    nt = (((1,), (1,)), ((), ()))
    tn = (((0,), (0,)), ((), ()))

    for h in range(H):
        gi = (2 * H if reverse else 0) + h
        gf = gi + H
        si = (H if reverse else 0) + h
        i_col = gates[:, gi:gi + 1]
        f_col = _log_sigmoid(gates[:, gf:gf + 1])
        i_row = gates_t[gi:gi + 1, :]
        f_row = _log_sigmoid(gates_t[gf:gf + 1, :])
        b_col = jnp.sum(jnp.where(mask, f_row, 0.0), axis=1, keepdims=True)
        b_row = jnp.sum(jnp.where(mask_t, f_col, 0.0), axis=0, keepdims=True)
        b_end = jnp.sum(f_row, axis=1, keepdims=True)
        m_prev = m_scr[si][:, 0:1]

        dmat = jnp.where(mask, b_col - b_row + i_row, NEG)
        m_inter = b_col + m_prev
        m_t = jnp.maximum(m_inter, jnp.max(dmat, axis=1, keepdims=True))
        q = qk_ref[:, h * dk:(h + 1) * dk]
        k = qk_ref[:, (H + h) * dk:(H + h + 1) * dk]
        v = v_ref[:, h * dv:(h + 1) * dv]
        s = lax.dot_general(q, k, nt, preferred_element_type=F32) * jnp.exp(dmat - m_t)
        inter = jnp.exp(m_inter - m_t)
        c_prev = C_scr[si]
        n_prev = n_scr[si]
        num = (inter * jnp.dot(q, c_prev.astype(BF16), preferred_element_type=F32)
               + jnp.dot(s.astype(BF16), v, preferred_element_type=F32))
        den = (inter * jnp.sum(q.astype(F32) * n_prev, axis=1, keepdims=True)
               + jnp.sum(s, axis=1, keepdims=True))
        hout = num / jnp.maximum(jnp.abs(den), jnp.exp(-m_t))

        g_col = b_end - b_col + i_col
        m_new = jnp.maximum(b_end + m_prev, jnp.max(g_col, axis=0, keepdims=True))
        w_col = jnp.exp(g_col - m_new)
        decay = jnp.exp(b_end + m_prev - m_new)
        kw = k.astype(F32) * w_col
        C_scr[si] = decay * c_prev + lax.dot_general(kw.astype(BF16), v, tn, preferred_element_type=F32)
        n_scr[si] = decay * n_prev + jnp.sum(kw, axis=0, keepdims=True)
        m_scr[si] = jnp.broadcast_to(m_new, (1, LANES))
        h_ref[:, h * dv:(h + 1) * dv] = hout


def _mlstm(qk, z, gates, gates_t, gate_b, gate_b_t, *, layer, width, v_blk):
    S = qk.shape[0]
    H = M_HEADS
    dk = qk.shape[1] // (2 * H)
    W = width
    dv = W // H
    L = min(256, S)
    nc = S // L

    def specs(cmap):
        return [pl.BlockSpec((L, 2 * H * dk), lambda c: (cmap(c), 0)),
                pl.BlockSpec((L, W), lambda c: (cmap(c), v_blk)),
                pl.BlockSpec((L, LANES), lambda c: (cmap(c), 0)),
                pl.BlockSpec((4 * H, L), lambda c: (0, cmap(c)))]

    fwd = lambda c: c
    bwd = lambda c: nc - 1 - c
    return pl.pallas_call(
        functools.partial(_mlstm_kernel, dk=dk, dv=dv, L=L),
        grid=(nc,),
        in_specs=specs(fwd) + specs(bwd) + [_layer_vec(layer, LANES),
                                            pl.BlockSpec((None, 4 * H, 1), lambda c: (layer, 0, 0))],
        out_specs=[pl.BlockSpec((L, W), lambda c: (c, 0)), pl.BlockSpec((L, W), lambda c: (bwd(c), 0))],
        out_shape=[jax.ShapeDtypeStruct((S, W), F32), jax.ShapeDtypeStruct((S, W), F32)],
        scratch_shapes=[pltpu.VMEM((2 * H, dk, dv), F32), pltpu.VMEM((2 * H, 1, dk), F32),
                        pltpu.VMEM((2 * H, 1, LANES), F32)],
        compiler_params=_params("arbitrary"),
        name="mlstm_bidir",
    )(qk, z, gates, gates_t, qk, z, gates, gates_t, gate_b, gate_b_t)


def _qkprep_kernel(q_ref, k_ref, v_ref, cos_ref, sin_ref, qg_ref, kg_ref, qt_ref, kn_ref, vt_ref,
                   *, hd, q_scale):
    ts, W = q_ref.shape
    lane = lax.broadcasted_iota(jnp.int32, (ts, LANES), 1)
    lo = lane < hd
    first_half = (lane % hd) < (hd // 8)
    row_lo = lax.broadcasted_iota(jnp.int32, (LANES, ts), 0) < hd
    cos = cos_ref[...]
    sin = sin_ref[...]

    def norm_rope(x, g):
        sq = x * x
        s_lo = jnp.sum(jnp.where(lo, sq, 0.0), axis=1, keepdims=True)
        s_hi = jnp.sum(jnp.where(lo, 0.0, sq), axis=1, keepdims=True)
        inv = lax.rsqrt(jnp.where(lo, s_lo, s_hi) * (1.0 / hd) + EPS)
        xn = x * inv * g
        up = pltpu.roll(xn, LANES - hd // 8, axis=1)
        dn = pltpu.roll(xn, hd // 8, axis=1)
        return xn * cos + jnp.where(first_half, up, dn) * sin

    for j in range(W // LANES):
        sl = slice(j * LANES, (j + 1) * LANES)
        qn = norm_rope(q_ref[:, sl].astype(F32), qg_ref[:, sl]) * q_scale
        qn_t = qn.T
        qt_ref[0, sl, :] = jnp.where(row_lo, qn_t, 0.0).astype(BF16)
        qt_ref[1, sl, :] = jnp.where(row_lo, 0.0, qn_t).astype(BF16)
        kn_ref[:, sl] = norm_rope(k_ref[:, sl].astype(F32), kg_ref[:, sl]).astype(BF16)
        r0 = j * (LANES + BF16_SUBLANES)
        vt_ref[r0:r0 + LANES, :] = v_ref[:, sl].astype(F32).T.astype(BF16)
        vt_ref[r0 + LANES:r0 + LANES + BF16_SUBLANES, :] = jnp.where(
            lax.broadcasted_iota(jnp.int32, (BF16_SUBLANES, ts), 0) == 0, 1.0, 0.0).astype(BF16)


def _qk_prep(z, cos_t, sin_t, qg, kg, *, layer, q_blk, k_blk, v_blk, hd, q_scale):
    S = z.shape[0]
    W = qg.shape[2]
    ts = min(512, S)
    n_slab = W // LANES
    return pl.pallas_call(
        functools.partial(_qkprep_kernel, hd=hd, q_scale=q_scale),
        grid=(S // ts,),
        in_specs=[pl.BlockSpec((ts, W), lambda i: (i, q_blk)),
                  pl.BlockSpec((ts, W), lambda i: (i, k_blk)),
                  pl.BlockSpec((ts, W), lambda i: (i, v_blk)),
                  pl.BlockSpec((ts, LANES), lambda i: (i, 0)),
                  pl.BlockSpec((ts, LANES), lambda i: (i, 0)),
                  _layer_vec(layer, W), _layer_vec(layer, W)],
        out_specs=[pl.BlockSpec((2, W, ts), lambda i: (0, 0, i)),
                   pl.BlockSpec((ts, W), lambda i: (i, 0)),
                   pl.BlockSpec((W + n_slab * BF16_SUBLANES, ts), lambda i: (0, i))],
        out_shape=[jax.ShapeDtypeStruct((2, W, S), BF16), jax.ShapeDtypeStruct((S, W), BF16),
                   jax.ShapeDtypeStruct((W + n_slab * BF16_SUBLANES, S), BF16)],
        compiler_params=_params("arbitrary"),
        name="attn_qk_prep",
    )(z, z, z, cos_t, sin_t, qg, kg)


ATTN_COL_BLOCK = 256
ATTN_KEY_BLOCK = 256
ATTN_COL_GROUP = 8
ATTN_LOOKAHEAD = 8


def _flash_kernel(qa_ref, qb_ref, k_ref, vt_ref, lam_ref, sg_ref, o_ref, q_scr, m_scr, acc_scr,
                  *, tq, tk, lam_init):
    kj = pl.program_id(2)
    cb_w, kb_w = ATTN_COL_BLOCK, min(ATTN_KEY_BLOCK, tk)
    dv = o_ref.shape[1]

    @pl.when(kj == 0)
    def _():
        q_scr[:, 0:tq] = qa_ref[0]
        q_scr[:, tq:2 * tq] = qb_ref[0]
        m_scr[...] = jnp.full_like(m_scr, NEG)
        acc_scr[...] = jnp.zeros_like(acc_scr)

    group = min(ATTN_COL_GROUP, 2 * tq // cb_w)
    steps = [(c, kb) for kb in range(tk // kb_w) for c in range(group)]

    def col_group(gi, carry):
        cols = [pl.ds(pl.multiple_of((gi * group + c) * cb_w, cb_w), cb_w) for c in range(group)]
        qts = [q_scr[:, cs] for cs in cols]
        state = [(m_scr[:, cs], acc_scr[:, cs]) for cs in cols]

        def scores(c, kb):
            return jnp.dot(k_ref[kb * kb_w:(kb + 1) * kb_w, :], qts[c], preferred_element_type=F32)

        pending = [scores(*st) for st in steps[:ATTN_LOOKAHEAD]]
        for n, (c, kb) in enumerate(steps):
            s = pending.pop(0)
            if n + ATTN_LOOKAHEAD < len(steps):
                pending.append(scores(*steps[n + ATTN_LOOKAHEAD]))
            m, acc = state[c]
            m_new = jnp.maximum(m, jnp.max(s, axis=0, keepdims=True))
            alpha = jnp.exp2(m - m_new)
            p = jnp.exp2((s - m_new).astype(BF16))
            acc = alpha * acc + jnp.dot(vt_ref[:, kb * kb_w:(kb + 1) * kb_w], p,
                                        preferred_element_type=F32)
            state[c] = (m_new, acc)
        for cs, (m, acc) in zip(cols, state):
            m_scr[:, cs], acc_scr[:, cs] = m, acc
        return carry

    lax.fori_loop(0, 2 * tq // (cb_w * group), col_group, 0)

    @pl.when(kj == pl.num_programs(2) - 1)
    def _():
        lam = lam_ref[...]
        lam_full = (jnp.exp(jnp.sum(lam[0:1] * lam[1:2], axis=1, keepdims=True))
                    - jnp.exp(jnp.sum(lam[2:3] * lam[3:4], axis=1, keepdims=True)) + lam_init)
        a = acc_scr[0:dv, :] / acc_scr[dv:dv + 1, :]
        o_t = a[:, 0:tq] - lam_full * a[:, tq:2 * tq]
        ms = jnp.mean(o_t * o_t, axis=0, keepdims=True)
        y_t = o_t * lax.rsqrt(ms + EPS) * sg_ref[...] * (1.0 - lam_init)
        o_ref[...] = y_t.T.astype(BF16)


def _flash_diff_attention(q_t, kn, v_t, lam, subln_g, *, layer, lam_init):
    _, W, S = q_t.shape
    H = A_HEADS
    dv = W // H
    tq = min(2048, S)
    tk = min(2048, S)
    return pl.pallas_call(
        functools.partial(_flash_kernel, tq=tq, tk=tk, lam_init=lam_init),
        grid=(H, S // tq, S // tk),
        in_specs=[pl.BlockSpec((1, dv, tq), lambda h, i, j: (0, h, i)),
                  pl.BlockSpec((1, dv, tq), lambda h, i, j: (1, h, i)),
                  pl.BlockSpec((tk, dv), lambda h, i, j: (j, h)),
                  pl.BlockSpec((dv + BF16_SUBLANES, tk), lambda h, i, j: (h, j)),
                  pl.BlockSpec((None,) + lam.shape[1:], lambda h, i, j: (layer, 0, 0)),
                  pl.BlockSpec((None, dv, 1), lambda h, i, j: (layer, 0, 0))],
        out_specs=pl.BlockSpec((tq, dv), lambda h, i, j: (i, h)),
        out_shape=jax.ShapeDtypeStruct((S, W), BF16),
        scratch_shapes=[pltpu.VMEM((dv, 2 * tq), BF16), pltpu.VMEM((1, 2 * tq), F32),
                        pltpu.VMEM((dv + BF16_SUBLANES, 2 * tq), F32)],
        compiler_params=_params("arbitrary", "arbitrary", "arbitrary"),
        name="flash_diff_attn",
    )(q_t, q_t, kn, v_t, lam, subln_g)


def _merge_kernel(hf_ref, hb_ref, mo_ref, ng_ref, ya_ref, wm_ref, wa_ref, gm_ref, ga_ref, o_ref, ym_scr):
    @pl.when(pl.program_id(1) == 0)
    def _():
        dv = hf_ref.shape[1] // M_HEADS
        for h in range(M_HEADS):
            sl = slice(h * dv, (h + 1) * dv)
            hs = hf_ref[:, sl] + hb_ref[:, sl]
            ms = jnp.mean(hs * hs, axis=-1, keepdims=True)
            y = hs * lax.rsqrt(ms + EPS) * ng_ref[:, sl] * _sigmoid(mo_ref[:, sl].astype(F32))
            ym_scr[:, sl] = y.astype(BF16)

    a = jnp.dot(ym_scr[...], wm_ref[...], preferred_element_type=F32)
    b = jnp.dot(ya_ref[...], wa_ref[...], preferred_element_type=F32)
    o_ref[...] = (_sigmoid(gm_ref[...].astype(F32)) * a + _sigmoid(ga_ref[...].astype(F32)) * b).astype(BF16)


def _merge(h_f, h_b, norm_g, y_a, w_m, w_a, z, *, layer, mo_blk, gm_col, ga_col):
    S, W = y_a.shape
    D = w_m.shape[2]
    tm, tn = min(512, S), 1024
    rows = pl.BlockSpec((tm, W), lambda i, j: (i, 0))
    return pl.pallas_call(
        _merge_kernel,
        grid=(S // tm, D // tn),
        in_specs=[rows, rows, pl.BlockSpec((tm, W), lambda i, j: (i, mo_blk)), _layer_vec(layer, W), rows,
                  pl.BlockSpec((None, W, tn), lambda i, j: (layer, 0, j)),
                  pl.BlockSpec((None, W, tn), lambda i, j: (layer, 0, j)),
                  pl.BlockSpec((tm, tn), lambda i, j: (i, gm_col // tn + j)),
                  pl.BlockSpec((tm, tn), lambda i, j: (i, ga_col // tn + j))],
        out_specs=pl.BlockSpec((tm, tn), lambda i, j: (i, j)),
        out_shape=jax.ShapeDtypeStruct((S, D), BF16),
        scratch_shapes=[pltpu.VMEM((tm, W), BF16)],
        compiler_params=_params("arbitrary", "arbitrary"),
        name="branch_merge",
    )(h_f, h_b, z, norm_g, y_a, w_m, w_a, z, z)


def _route(logits):
    G, E = N_GROUPS, EXPERTS_PER_GROUP
    lane = lax.broadcasted_iota(jnp.int32, logits.shape, 1)
    lanef = lane.astype(F32)
    is_g = lane < G
    gl = jnp.where(is_g, logits, NEG)
    ge = jnp.where(is_g, jnp.exp(gl - jnp.max(gl, axis=1, keepdims=True)), 0.0)
    probs = ge / jnp.sum(ge, axis=1, keepdims=True)
    g_val = jnp.max(probs, axis=1, keepdims=True)
    g_idx = jnp.min(jnp.where(is_g & (probs == g_val), lanef, float(LANES)), axis=1, keepdims=True)
    lo = G + E * g_idx
    in_grp = (lanef >= lo) & (lanef < lo + E)
    el = jnp.where(in_grp, logits, NEG)
    v1 = jnp.max(el, axis=1, keepdims=True)
    i1 = jnp.min(jnp.where(in_grp & (el == v1), lanef, float(LANES)), axis=1, keepdims=True)
    rest = in_grp & (lanef != i1)
    el2 = jnp.where(rest, logits, NEG)
    v2 = jnp.max(el2, axis=1, keepdims=True)
    i2 = jnp.min(jnp.where(rest & (el2 == v2), lanef, float(LANES)), axis=1, keepdims=True)
    t = jnp.exp(v2 - v1)
    w1 = (1.0 / (1.0 + t)) * g_val
    w2 = (t / (1.0 + t)) * g_val
    return jnp.where(lane == 0, i1 - G, jnp.where(lane == 1, i2 - G,
                     jnp.where(lane == 2, w1, jnp.where(lane == 3, w2, 0.0))))


def _out_kernel(mg_ref, w_ref, x_ref, g1_ref, n2_ref, sc_ref, sh_ref, wr_ref, br_ref,
                x1_ref, h2_ref, r_ref, cnt_ref, cnt_scr):
    tm = x_ref.shape[0]

    @pl.when(pl.program_id(0) == 0)
    def _():
        cnt_scr[...] = jnp.zeros_like(cnt_scr)

    y = jnp.dot(mg_ref[...], w_ref[...], preferred_element_type=F32)
    x1 = x_ref[...] + g1_ref[...] * y
    x1_ref[...] = x1
    ms = jnp.mean(x1 * x1, axis=-1, keepdims=True)
    h2 = x1 * lax.rsqrt(ms + EPS) * n2_ref[...] * (1.0 + sc_ref[...]) + sh_ref[...]
    h2_ref[...] = h2
    h_hi = h2.astype(BF16)
    h_lo = (h2 - h_hi.astype(F32)).astype(BF16)
    t = jnp.dot(h_hi, wr_ref[...], preferred_element_type=F32)
    logits = (t[:, :LANES] + t[:, LANES:]
              + jnp.dot(h_lo, wr_ref[:, :LANES], preferred_element_type=F32) + br_ref[...])
    route = _route(logits)

    lane = lax.broadcasted_iota(jnp.int32, (tm, LANES), 1)
    lanef = lane.astype(F32)
    sel1 = lanef == route[:, 0:1]
    sel2 = lanef == route[:, 1:2]
    onehot = jnp.where(sel1 | sel2, 1.0, 0.0)
    earlier = (lax.broadcasted_iota(jnp.int32, (tm, tm), 0)
               > lax.broadcasted_iota(jnp.int32, (tm, tm), 1))
    before = cnt_scr[...] + jnp.dot(jnp.where(earlier, 1.0, 0.0).astype(BF16), onehot.astype(BF16),
                                    preferred_element_type=F32)
    rank1 = jnp.sum(jnp.where(sel1, before, 0.0), axis=1, keepdims=True)
    rank2 = jnp.sum(jnp.where(sel2, before, 0.0), axis=1, keepdims=True)
    code1 = route[:, 0:1] * float(RANK_RADIX) + rank1
    code2 = route[:, 1:2] * float(RANK_RADIX) + rank2
    r_ref[...] = jnp.where(lane == 4, code1, jnp.where(lane == 5, code2, route))
    cnt = cnt_scr[...] + jnp.sum(onehot, axis=0, keepdims=True)
    cnt_scr[...] = cnt
    cnt_ref[...] = jnp.broadcast_to(cnt, cnt_ref.shape)


def _out_projection(merged, w_out, x, mod, norm2_g, w_router, b_router, *, layer):
    S, D = x.shape
    tm = min(512, S)
    row = pl.BlockSpec((tm, D), lambda i: (i, 0))
    once = pl.Buffered(1)
    return pl.pallas_call(
        _out_kernel,
        grid=(S // tm,),
        in_specs=[row, pl.BlockSpec((None, D, D), lambda i: (layer, 0, 0), pipeline_mode=once), row,
                  _layer_vec(layer, D, 2), _layer_vec(layer, D), _layer_vec(layer, D, 4),
                  _layer_vec(layer, D, 3),
                  pl.BlockSpec((None, D, 2 * LANES), lambda i: (layer, 0, 0), pipeline_mode=once),
                  _layer_vec(layer, LANES)],
        out_specs=[row, row, pl.BlockSpec((tm, LANES), lambda i: (i, 0)),
                   pl.BlockSpec((SUBLANES, LANES), lambda i: (0, 0))],
        out_shape=[jax.ShapeDtypeStruct((S, D), F32), jax.ShapeDtypeStruct((S, D), F32),
                   jax.ShapeDtypeStruct((S, LANES), F32), jax.ShapeDtypeStruct((SUBLANES, LANES), F32)],
        scratch_shapes=[pltpu.VMEM((1, LANES), F32)],
        compiler_params=_params("arbitrary"),
        name="out_proj_router",
    )(merged, w_out, x, mod, norm2_g, mod, mod, w_router, b_router)


def _dispatch_plan(route, counts_f, n_experts, tm):
    T = route.shape[0]
    counts = counts_f[0, :n_experts].astype(jnp.int32)
    padded = ((counts + tm - 1) // tm) * tm
    pad_end = jnp.cumsum(padded)
    pad_start = pad_end - padded
    n_tiles = (2 * T) // tm + n_experts
    n_active = (pad_end[-1] // tm).astype(jnp.int32)
    tile_start = jnp.arange(n_tiles, dtype=jnp.int32) * tm
    tile_expert = jnp.sum(pad_end[None, :] <= jnp.minimum(tile_start, pad_end[-1] - 1)[:, None],
                          axis=1, dtype=jnp.int32)
    pos = _sorted_positions(route, pad_start)
    tail = ((pad_start + counts) // SUBLANES) * SUBLANES
    fill = jnp.concatenate([tail, (pad_end - tail) // SUBLANES, n_active.reshape(1)])
    ids = jnp.arange(n_experts, dtype=jnp.int32)
    has_rows = counts > 0
    later = (ids[None, :] > ids[:, None]) & has_rows[None, :]
    next_expert = jnp.min(jnp.where(later, ids[None, :], n_experts), axis=1)
    next_expert = jnp.where(next_expert == n_experts, -1, next_expert).astype(jnp.int32)
    expert_ord = jnp.sum((ids[None, :] < ids[:, None]) & has_rows[None, :], axis=1, dtype=jnp.int32)
    return tile_expert, n_active.reshape(1), pos, fill, next_expert, expert_ord


def _positions_kernel(r_ref, start_ref, o_ref):
    r = r_ref[...]
    lane = lax.broadcasted_iota(jnp.int32, r.shape, 1)
    lanef = lane.astype(F32)
    start = start_ref[...]
    pos = []
    for s in range(2):
        e, code = r[:, s:s + 1], r[:, 4 + s:5 + s]
        first = jnp.sum(jnp.where(lanef == e, start, 0.0), axis=1, keepdims=True)
        pos.append(first + (code - e * float(RANK_RADIX)))
    o_ref[...] = jnp.where(lane == 0, pos[0], jnp.where(lane == 1, pos[1], 0.0))


def _sorted_positions(route, pad_start):
    T = route.shape[0]
    tm = min(1024, T)
    start = jnp.pad(pad_start.astype(F32), (0, LANES - pad_start.shape[0]))[None]
    out = pl.pallas_call(
        _positions_kernel,
        grid=(T // tm,),
        in_specs=[pl.BlockSpec((tm, LANES), lambda i: (i, 0)), pl.BlockSpec((1, LANES), lambda i: (0, 0))],
        out_specs=pl.BlockSpec((tm, LANES), lambda i: (i, 0)),
        out_shape=jax.ShapeDtypeStruct((T, LANES), F32),
        compiler_params=_params("arbitrary"),
        name="moe_positions",
    )(route, start)
    return out[:, 0:2].T.reshape(-1).astype(jnp.int32)


def _dispatch_kernel(pos_ref, fill_ref, h_ref, xs_hbm, zero_scr, sem, *, tm, T, n_experts, n_tiles):
    i = pl.program_id(0)

    @pl.when(i == 0)
    def _():
        zero_scr[...] = jnp.zeros_like(zero_scr)
        zero8 = zero_scr.at[pl.ds(0, SUBLANES), :]
        n_active = fill_ref[2 * n_experts]

        def per_expert(e, total):
            def piece(k, carry):
                row = pl.multiple_of(fill_ref[e] + k * SUBLANES, SUBLANES)
                pltpu.make_async_copy(zero8, xs_hbm.at[pl.ds(row, SUBLANES), :], sem.at[1]).start()
                return carry
            n = fill_ref[n_experts + e]
            lax.fori_loop(0, n, piece, 0)
            return total + n

        pieces = lax.fori_loop(0, n_experts, per_expert, 0)

        def unused_tile(t, carry):
            row = pl.multiple_of(t * tm, tm)
            pltpu.make_async_copy(zero_scr, xs_hbm.at[pl.ds(row, tm), :], sem.at[2]).start()
            return carry

        lax.fori_loop(n_active, n_tiles, unused_tile, 0)

        def wait_piece(k, carry):
            pltpu.make_async_copy(zero8, xs_hbm.at[pl.ds(0, SUBLANES), :], sem.at[1]).wait()
            return carry

        lax.fori_loop(0, pieces, wait_piece, 0)

        def wait_tile(t, carry):
            pltpu.make_async_copy(zero_scr, xs_hbm.at[pl.ds(0, tm), :], sem.at[2]).wait()
            return carry

        lax.fori_loop(n_active, n_tiles, wait_tile, 0)

    def body(r, carry):
        for slot in range(2):
            p = pos_ref[slot * T + i * tm + r]
            pltpu.make_async_copy(h_ref.at[pl.ds(r, 1), :], xs_hbm.at[pl.ds(p, 1), :], sem.at[0]).start()
        return carry

    lax.fori_loop(0, tm, body, 0, unroll=8)
    for slot in range(2):
        pltpu.make_async_copy(h_ref, xs_hbm.at[pl.ds(0, tm), :], sem.at[0]).wait()


def _dispatch(h2, pos, fill, *, n_experts, tm):
    T, D = h2.shape
    n_tiles = (2 * T) // tm + n_experts
    grid_spec = pltpu.PrefetchScalarGridSpec(
        num_scalar_prefetch=2,
        grid=(T // tm,),
        in_specs=[pl.BlockSpec((tm, D), lambda i, pos, fill: (i, 0))],
        out_specs=pl.BlockSpec(memory_space=pl.ANY),
        scratch_shapes=[pltpu.VMEM((tm, D), F32), pltpu.SemaphoreType.DMA((3,))],
    )
    return pl.pallas_call(
        functools.partial(_dispatch_kernel, tm=tm, T=T, n_experts=n_experts, n_tiles=n_tiles),
        grid_spec=grid_spec,
        out_shape=jax.ShapeDtypeStruct((n_tiles * tm, D), F32),
        compiler_params=_params("arbitrary"),
        name="moe_dispatch",
    )(pos, fill, h2)


def _expert_kernel(te_ref, na_ref, nx_ref, od_ref, x_ref, wg_hbm, wu_hbm, wd_hbm, o_ref,
                   wg_f, wu_f, wd_f, wg_b, wu_b, wd_b, sem, *, layer):
    i = pl.program_id(0)

    def weight_copies(e, slot):
        return [pltpu.make_async_copy(src.at[layer, e], dst.at[slot], sem.at[slot, k])
                for k, (src, dst) in enumerate(((wg_hbm, wg_f), (wu_hbm, wu_f), (wd_hbm, wd_f)))]

    @pl.when(i >= na_ref[0])
    def _():
        o_ref[...] = jnp.zeros_like(o_ref)

    @pl.when(i < na_ref[0])
    def _():
        e = te_ref[i]

        @pl.when((i == 0) | (e != te_ref[jnp.maximum(i - 1, 0)]))
        def _():
            slot = od_ref[e] % 2

            @pl.when(i == 0)
            def _():
                for cp in weight_copies(e, slot):
                    cp.start()

            for cp in weight_copies(e, slot):
                cp.wait()
            nxt = nx_ref[e]

            @pl.when(nxt >= 0)
            def _():
                for cp in weight_copies(nxt, 1 - slot):
                    cp.start()

            wg_b[...] = wg_f[slot].astype(BF16)
            wu_b[...] = wu_f[slot].astype(BF16)
            wd_b[...] = wd_f[slot].astype(BF16)

        x = x_ref[...].astype(BF16)
        g = jnp.dot(x, wg_b[...], preferred_element_type=F32)
        u = jnp.dot(x, wu_b[...], preferred_element_type=F32)
        a = (g * _sigmoid(g) * u).astype(BF16)
        o_ref[...] = jnp.dot(a, wd_b[...], preferred_element_type=F32)


def _experts(xs, tile_expert, n_active, next_expert, expert_ord, w_gate, w_up, w_down, *, layer, tm):
    D = xs.shape[1]
    Fe = w_gate.shape[-1]
    n_tiles = tile_expert.shape[0]
    hbm = pl.BlockSpec(memory_space=pl.ANY)
    grid_spec = pltpu.PrefetchScalarGridSpec(
        num_scalar_prefetch=4,
        grid=(n_tiles,),
        in_specs=[pl.BlockSpec((tm, D), lambda i, te, na, nx, od: (jnp.minimum(i, na[0] - 1), 0)),
                  hbm, hbm, hbm],
        out_specs=pl.BlockSpec((tm, D), lambda i, te, na, nx, od: (i, 0)),
        scratch_shapes=[pltpu.VMEM((2, D, Fe), F32), pltpu.VMEM((2, D, Fe), F32), pltpu.VMEM((2, Fe, D), F32),
                        pltpu.VMEM((D, Fe), BF16), pltpu.VMEM((D, Fe), BF16), pltpu.VMEM((Fe, D), BF16),
                        pltpu.SemaphoreType.DMA((2, 3))],
    )
    return pl.pallas_call(
        functools.partial(_expert_kernel, layer=layer),
        grid_spec=grid_spec,
        out_shape=jax.ShapeDtypeStruct((n_tiles * tm, D), F32),
        compiler_params=_params("arbitrary"),
        name="routed_experts",
    )(tile_expert, n_active, next_expert, expert_ord, xs, w_gate, w_up, w_down)


def _combine_kernel(pos_ref, ys_hbm, x_ref, g2_ref, r_ref, o_ref, ybuf, sem, *, tm, T):
    i = pl.program_id(0)

    def issue(blk, slot):
        def body(r, carry):
            for k in range(2):
                p = pos_ref[k * T + blk * tm + r]
                pltpu.make_async_copy(ys_hbm.at[pl.ds(p, 1), :], ybuf.at[slot, k, pl.ds(r, 1), :],
                                      sem.at[slot]).start()
            return carry
        lax.fori_loop(0, tm, body, 0, unroll=8)

    @pl.when(i == 0)
    def _():
        issue(0, 0)

    @pl.when(i + 1 < pl.num_programs(0))
    def _():
        issue(i + 1, (i + 1) % 2)

    slot = i % 2
    for k in range(2):
        pltpu.make_async_copy(ys_hbm.at[pl.ds(0, tm), :], ybuf.at[slot, k], sem.at[slot]).wait()
    y = r_ref[:, 2:3] * ybuf[slot, 0] + r_ref[:, 3:4] * ybuf[slot, 1]
    o_ref[...] = x_ref[...] + g2_ref[...] * y


def _combine(y_sorted, pos, x1, mod, route, *, layer, tm):
    T, D = x1.shape
    grid_spec = pltpu.PrefetchScalarGridSpec(
        num_scalar_prefetch=1,
        grid=(T // tm,),
        in_specs=[pl.BlockSpec(memory_space=pl.ANY),
                  pl.BlockSpec((tm, D), lambda i, pos: (i, 0)),
                  _layer_vec(layer, D, 5),
                  pl.BlockSpec((tm, LANES), lambda i, pos: (i, 0))],
        out_specs=pl.BlockSpec((tm, D), lambda i, pos: (i, 0)),
        scratch_shapes=[pltpu.VMEM((2, 2, tm, D), F32), pltpu.SemaphoreType.DMA((2,))],
    )
    return pl.pallas_call(
        functools.partial(_combine_kernel, tm=tm, T=T),
        grid_spec=grid_spec,
        out_shape=jax.ShapeDtypeStruct((T, D), F32),
        compiler_params=_params("arbitrary"),
        name="moe_combine",
    )(pos, y_sorted, x1, mod, route)


def _rope_tables(S, hd):
    rope_dim = hd // 4
    posn = jnp.arange(S, dtype=F32)
    inv = ROPE_THETA ** (-jnp.arange(0, rope_dim, 2, dtype=F32) / rope_dim)
    ang = posn[:, None] * inv[None, :]
    cos, sin = jnp.cos(ang), jnp.sin(ang)
    rest = hd - rope_dim
    cos_t = jnp.concatenate([cos, cos, jnp.ones((S, rest), F32)], axis=1)
    sin_t = jnp.concatenate([-sin, sin, jnp.zeros((S, rest), F32)], axis=1)
    reps = LANES // hd
    return jnp.tile(cos_t, (1, reps)), jnp.tile(sin_t, (1, reps))


def kernel(x, c, ada_w, ada_b, norm1_g, norm2_g, w_in, m_conv_w, m_gate_b, m_norm_g, a_qnorm_g,
           a_knorm_g, a_lambda, a_subln_g, w_branch_m, w_branch_a, w_out, rg_w, rg_b, re_w, re_b,
           e_w_gate, e_w_up, e_w_down):
    B, S, D = x.shape
    assert B == 1
    depth = ada_w.shape[0]
    Wm = w_branch_m.shape[1]
    Wa = w_branch_a.shape[1]
    qk_w = m_conv_w.shape[2]
    n_gate = 4 * M_HEADS
    hd = a_qnorm_g.shape[1]
    n_experts = re_w.shape[2]
    n_route = N_GROUPS + n_experts
    assert Wa // A_HEADS == LANES and 2 * hd == LANES
    c_mv, c_mo, c_aq = qk_w, qk_w + Wm, qk_w + 2 * Wm
    c_ak, c_av, c_gm = c_aq + Wa, c_aq + 2 * Wa, c_aq + 3 * Wa
    c_ga = c_gm + D
    g0 = qk_w + Wm

    w_main = jnp.concatenate([w_in[:, :, :g0], w_in[:, :, g0 + n_gate:]], axis=2).astype(BF16)
    w_gate = jnp.pad(w_in[:, :, g0:g0 + n_gate], ((0, 0), (0, 0), (0, LANES - n_gate))).astype(BF16)
    w_bm, w_ba, w_o = w_branch_m.astype(BF16), w_branch_a.astype(BF16), w_out.astype(BF16)
    w_r = jnp.pad(jnp.concatenate([rg_w, re_w], axis=2), ((0, 0), (0, 0), (0, LANES - n_route)))
    w_r_hi = w_r.astype(BF16)
    w_router = jnp.concatenate([w_r_hi, (w_r - w_r_hi.astype(F32)).astype(BF16)], axis=2)
    b_router = jnp.pad(jnp.concatenate([rg_b, re_b], axis=1), ((0, 0), (0, LANES - n_route)))[:, None]
    gate_b = jnp.pad(m_gate_b.reshape(depth, 1, n_gate), ((0, 0), (0, 0), (0, LANES - n_gate)))
    gate_b_t = m_gate_b.reshape(depth, n_gate, 1)
    qg = jnp.tile(a_qnorm_g, (1, Wa // hd))[:, None]
    kg = jnp.tile(a_knorm_g, (1, Wa // hd))[:, None]
    n1, n2, mng = norm1_g[:, None], norm2_g[:, None], m_norm_g[:, None]
    subln = a_subln_g[:, :, None]

    xs = x[0]
    mod = _modulation(c, ada_w, ada_b)
    cos_t, sin_t = _rope_tables(S, hd)
    tm_moe = min(MOE_TILE, S)

    for l in range(depth):
        lam_init = 0.8 - 0.6 * math.exp(-0.3 * l)
        z, gates, gates_t = _in_projection(xs, n1, mod, w_main, w_gate, layer=l)
        qk = _qk_conv(z, m_conv_w, float((qk_w // (2 * M_HEADS)) ** -0.5), layer=l)
        h_f, h_b = _mlstm(qk, z, gates, gates_t, gate_b, gate_b_t, layer=l, width=Wm, v_blk=c_mv // Wm)
        q_t, kn, v_t = _qk_prep(z, cos_t, sin_t, qg, kg, layer=l, q_blk=c_aq // Wa, k_blk=c_ak // Wa,
                                v_blk=c_av // Wa, hd=hd, q_scale=float(hd ** -0.5 * math.log2(math.e)))
        y_a = _flash_diff_attention(q_t, kn, v_t, a_lambda, subln, layer=l, lam_init=lam_init)
        merged = _merge(h_f, h_b, mng, y_a, w_bm, w_ba, z, layer=l, mo_blk=c_mo // Wm,
                        gm_col=c_gm, ga_col=c_ga)
        x1, h2, route, counts = _out_projection(merged, w_o, xs, mod, n2, w_router, b_router, layer=l)
        tile_expert, n_active, pos, fill, next_expert, expert_ord = _dispatch_plan(
            route, counts, n_experts, tm_moe)
        x_sorted = _dispatch(h2, pos, fill, n_experts=n_experts, tm=tm_moe)
        y_sorted = _experts(x_sorted, tile_expert, n_active, next_expert, expert_ord,
                            e_w_gate, e_w_up, e_w_down, layer=l, tm=tm_moe)
        xs = _combine(y_sorted, pos, x1, mod, route, layer=l, tm=tm_moe)
    return xs[None]
```
